```python
import jax, jax.numpy as jnp
from jax import lax
import numpy as np

D_MODEL = 1024
BATCH = 4
SEQ = 4096
DEPTH = 2

N_MEM = 256
ATT_GROUPS = ((128, 1), (512, 4), (2048, 16))
N_ATT_GROUPS = len(ATT_GROUPS)
ATT_HEADS = 4
ATT_HEAD_DIM = 128
ATT_WIDTH = ATT_HEADS * ATT_HEAD_DIM
Q_BLOCK = 128
HGRN_EXPAND = 128
HGRN_WIDTH = D_MODEL
HGRN_HEADS = HGRN_WIDTH // HGRN_EXPAND
HGRN_CHUNK = 64
D_FF = 2816
X_HEADS = 4
X_HEAD_DIM = D_MODEL // X_HEADS
ROPE_THETA = 10000.0
EPS = 1e-6
IN_SIZES = (N_ATT_GROUPS * ATT_WIDTH,) * 3 + (HGRN_WIDTH,) * 4 + (D_MODEL,) * 2
N_IN = sum(IN_SIZES)

kernel_name = "hybrid_dilated_attn_hgrn2_macaron_block"


def rms_norm(x, g):
    xf = x.astype(jnp.float32)
    y = xf * lax.rsqrt(jnp.mean(xf * xf, axis=-1, keepdims=True) + EPS)
    return (y * g.astype(jnp.float32)).astype(x.dtype)


def swiglu(h, w_gu, w_down):
    gate, up = jnp.split(h @ w_gu, 2, axis=-1)
    return (jax.nn.silu(gate) * up) @ w_down


def rope_tables(positions, dim, dtype):
    inv_freq = ROPE_THETA ** (-jnp.arange(0, dim, 2, dtype=jnp.float32) / dim)
    ang = positions.astype(jnp.float32)[..., None] * inv_freq
    cos = jnp.cos(ang)[:, :, None, None, :].astype(dtype)
    sin = jnp.sin(ang)[:, :, None, None, :].astype(dtype)
    return cos, sin


def apply_rope(x, cos, sin):
    x1, x2 = jnp.split(x, 2, axis=-1)
    return jnp.concatenate([x1 * cos - x2 * sin, x2 * cos + x1 * sin], axis=-1)


def dilated_attention(q, k, v):
    B, T, G, H, dh = q.shape
    n_blk = T // Q_BLOCK
    scale = dh ** -0.5
    k_groups = [k[:, :, gi] for gi in range(G)]
    v_groups = [v[:, :, gi] for gi in range(G)]

    def block(b):
        t0 = b * Q_BLOCK
        tq = t0 + jnp.arange(Q_BLOCK)
        q_blk = lax.dynamic_slice_in_dim(q, t0, Q_BLOCK, axis=1)
        outs, lses = [], []
        for gi, (window, dil) in enumerate(ATT_GROUPS):
            j = jnp.arange(window // dil + 1)
            idx = tq[:, None] - dil * j[None, :]
            valid = idx >= 0
            idx = jnp.maximum(idx, 0)
            k_sel = k_groups[gi][:, idx]
            v_sel = v_groups[gi][:, idx]
            s = jnp.einsum('bqhd,bqjhd->bqhj', q_blk[:, :, gi], k_sel,
                           preferred_element_type=jnp.float32) * scale
            s = jnp.where(valid[None, :, None, :], s, -jnp.inf)
            m = jnp.max(s, axis=-1, keepdims=True)
            p = jnp.exp(s - m)
            den = jnp.sum(p, axis=-1)
            o = jnp.einsum('bqhj,bqjhd->bqhd', p, v_sel.astype(jnp.float32)) / den[..., None]
            outs.append(o)
            lses.append(m[..., 0] + jnp.log(den))
        w = jax.nn.softmax(jnp.stack(lses, axis=-1), axis=-1)
        o = jnp.einsum('bqhgd,bqhg->bqhd', jnp.stack(outs, axis=-2), w)
        return o.astype(q.dtype)

    o = lax.map(block, jnp.arange(n_blk))
    return o.transpose(1, 0, 2, 3, 4).reshape(B, T, H * dh)


def hgrn2_recurrence(q, f_logit, i, lb):
    B, T, H, N = q.shape
    C = HGRN_CHUNK
    NC = T // C
    lb = lb.reshape(H, N).astype(jnp.float32)
    f = lb + (1.0 - lb) * jax.nn.sigmoid(f_logit.astype(jnp.float32))
    g = jnp.log(f)
    k = 1.0 - f
    qs = q.astype(jnp.float32) * (N ** -0.5)

    def to_chunks(a):
        return a.reshape(B, NC, C, H, a.shape[-1]).transpose(1, 0, 3, 2, 4)

    causal = jnp.tril(jnp.ones((C, C), dtype=bool))

    def step(S, inp):
        qc, kc, vc, gc = inp
        b = jnp.cumsum(gc, axis=2)
        diff = b[:, :, :, None, :] - b[:, :, None, :, :]
        decay = jnp.exp(jnp.where(causal[:, :, None], diff, -jnp.inf))
        A = jnp.einsum('bhtn,bhsn,bhtsn->bhts', qc, kc, decay)
        o = (jnp.einsum('bhts,bhsv->bhtv', A, vc)
             + jnp.einsum('bhtn,bhnv->bhtv', qc * jnp.exp(b), S))
        b_last = b[:, :, -1:, :]
        S = (jnp.exp(b_last[:, :, 0, :])[..., None] * S
             + jnp.einsum('bhsn,bhsv->bhnv', kc * jnp.exp(b_last - b), vc))
        return S, o

    S0 = jnp.zeros((B, H, N, i.shape[-1]), jnp.float32)
    _, o = lax.scan(step, S0, (to_chunks(qs), to_chunks(k),
                               to_chunks(i.astype(jnp.float32)), to_chunks(g)))
    return o.transpose(1, 0, 3, 2, 4).reshape(B, T, H, -1)


def cross_attention(h, mem_n, wq, wkv, wo):
    B, T, _ = h.shape
    M = mem_n.shape[1]
    q = (h @ wq).reshape(B, T, X_HEADS, X_HEAD_DIM)
    k, v = jnp.split(mem_n @ wkv, 2, axis=-1)
    k = k.reshape(B, M, X_HEADS, X_HEAD_DIM)
    v = v.reshape(B, M, X_HEADS, X_HEAD_DIM)
    s = jnp.einsum('bthd,bmhd->bhtm', q, k, preferred_element_type=jnp.float32) * (X_HEAD_DIM ** -0.5)
    p = jax.nn.softmax(s, axis=-1)
    o = jnp.einsum('bhtm,bmhd->bthd', p, v.astype(jnp.float32)).astype(h.dtype)
    return o.reshape(B, T, X_HEADS * X_HEAD_DIM) @ wo


def setup_inputs(seed: int = 0) -> dict:
    key = jax.random.key(seed)
    ks = iter(jax.random.split(key, 32))
    f32 = jnp.float32

    def w(shape, fan_in):
        return jax.random.normal(next(ks), shape, f32) * (fan_in ** -0.5)

    def gain(shape):
        return 1.0 + 0.02 * jax.random.normal(next(ks), shape, f32)

    L = DEPTH
    return {
        "x": jax.random.normal(next(ks), (BATCH, SEQ, D_MODEL), f32),
        "mem": jax.random.normal(next(ks), (BATCH, N_MEM, D_MODEL), f32),
        "positions": jnp.broadcast_to(jnp.arange(SEQ, dtype=jnp.int32), (BATCH, SEQ)),
        "ffn1_norm": gain((L, D_MODEL)),
        "ffn1_w_gu": w((L, D_MODEL, 2 * D_FF), D_MODEL),
        "ffn1_w_down": w((L, D_FF, D_MODEL), D_FF),
        "mix_norm": gain((L, D_MODEL)),
        "w_in": w((L, D_MODEL, N_IN), D_MODEL),
        "hgrn_lower_bounds": 0.1 * jax.random.normal(next(ks), (L, HGRN_WIDTH), f32),
        "hgrn_head_norm": gain((L, HGRN_WIDTH)),
        "w_att_branch": w((L, ATT_WIDTH, D_MODEL), ATT_WIDTH),
        "w_hgrn_branch": w((L, HGRN_WIDTH, D_MODEL), HGRN_WIDTH),
        "w_mix_out": w((L, D_MODEL, D_MODEL), D_MODEL),
        "xattn_norm": gain((L, D_MODEL)),
        "mem_norm": gain((L, D_MODEL)),
        "xattn_wq": w((L, D_MODEL, X_HEADS * X_HEAD_DIM), D_MODEL),
        "xattn_wkv": w((L, D_MODEL, 2 * X_HEADS * X_HEAD_DIM), D_MODEL),
        "xattn_wo": w((L, X_HEADS * X_HEAD_DIM, D_MODEL), X_HEADS * X_HEAD_DIM),
        "ffn2_norm": gain((L, D_MODEL)),
        "ffn2_w_gu": w((L, D_MODEL, 2 * D_FF), D_MODEL),
        "ffn2_w_down": w((L, D_FF, D_MODEL), D_FF),
        "final_norm": gain((D_MODEL,)),
    }


def reference(x, mem, positions, ffn1_norm, ffn1_w_gu, ffn1_w_down, mix_norm, w_in,
              hgrn_lower_bounds, hgrn_head_norm, w_att_branch, w_hgrn_branch, w_mix_out,
              xattn_norm, mem_norm, xattn_wq, xattn_wkv, xattn_wo,
              ffn2_norm, ffn2_w_gu, ffn2_w_down, final_norm):
    B, T, _ = x.shape
    cos, sin = rope_tables(positions, ATT_HEAD_DIM, x.dtype)
    lb_p = jax.nn.softmax(hgrn_lower_bounds.astype(jnp.float32), axis=0)
    lb_all = jnp.cumsum(lb_p, axis=0) - lb_p[0]
    split_at = [sum(IN_SIZES[:n]) for n in range(1, len(IN_SIZES))]

    for l in range(DEPTH):
        x = x + 0.5 * swiglu(rms_norm(x, ffn1_norm[l]), ffn1_w_gu[l], ffn1_w_down[l])

        h = rms_norm(x, mix_norm[l])
        z = h @ w_in[l]
        q_a, k_a, v_a, q_b, f_b, i_b, og_b, gate_a, gate_b = jnp.split(z, split_at, axis=-1)
        att_shape = (B, T, N_ATT_GROUPS, ATT_HEADS, ATT_HEAD_DIM)
        q_a = apply_rope(q_a.reshape(att_shape), cos, sin)
        k_a = apply_rope(k_a.reshape(att_shape), cos, sin)
        v_a = v_a.reshape(att_shape)
        y_a = dilated_attention(q_a, k_a, v_a) @ w_att_branch[l]

        hg_shape = (B, T, HGRN_HEADS, HGRN_EXPAND)
        o_b = hgrn2_recurrence(q_b.reshape(hg_shape), f_b.reshape(hg_shape),
                               i_b.reshape(hg_shape), lb_all[l])
        o_b = rms_norm(o_b, hgrn_head_norm[l].reshape(HGRN_HEADS, HGRN_EXPAND))
        o_b = o_b.reshape(B, T, HGRN_WIDTH).astype(x.dtype) * jax.nn.silu(og_b)
        y_b = o_b @ w_hgrn_branch[l]

        merged = jax.nn.sigmoid(gate_a) * y_a + jax.nn.sigmoid(gate_b) * y_b
        x = x + merged @ w_mix_out[l]

        x = x + cross_attention(rms_norm(x, xattn_norm[l]), rms_norm(mem, mem_norm[l]),
                                xattn_wq[l], xattn_wkv[l], xattn_wo[l])

        x = x + 0.5 * swiglu(rms_norm(x, ffn2_norm[l]), ffn2_w_gu[l], ffn2_w_down[l])

    return rms_norm(x, final_norm)
```

```python
import functools
import math

import jax
import jax.numpy as jnp
from jax import lax
from jax.experimental import pallas as pl
from jax.experimental.pallas import tpu as pltpu

F32 = jnp.float32
BF16 = jnp.bfloat16

ATT_GROUPS = ((128, 1), (512, 4), (2048, 16))
ATT_HEADS = 4
ATT_HEAD_DIM = 128
ATT_GROUP_WIDTH = ATT_HEADS * ATT_HEAD_DIM
HGRN_HEAD_DIM = 128
X_HEADS = 4
ROPE_THETA = 10000.0
EPS = 1e-6

LANES = 128
SUBLANES = 8
VMEM_LIMIT_BYTES = 56 * 1024 * 1024

TOKEN_TILE = 512
ATT_BLOCK = 128
HGRN_CHUNK = 64
NEG_BIG = -1e30


def _params(*sem):
    return pltpu.CompilerParams(dimension_semantics=sem,
                                vmem_limit_bytes=VMEM_LIMIT_BYTES)


def _rms(x, g):
    ms = jnp.mean(x * x, axis=-1, keepdims=True)
    return x * lax.rsqrt(ms + EPS) * g


def _dot(a, b):
    return jnp.dot(a, b, preferred_element_type=F32)


def _dot_nt(a, b):
    return lax.dot_general(a, b, (((1,), (1,)), ((), ())), preferred_element_type=F32)


def _dot_tn(a, b):
    return lax.dot_general(a, b, (((0,), (0,)), ((), ())), preferred_element_type=F32)


def _rope_table_kernel(pos_ref, cos_ref, sin_ref):
    half = ATT_HEAD_DIM // 2
    lane = lax.broadcasted_iota(jnp.int32, (1, ATT_HEAD_DIM), 1)
    idx = jnp.where(lane >= half, lane - half, lane).astype(F32)
    inv_freq = jnp.exp(idx * (-2.0 * math.log(ROPE_THETA) / ATT_HEAD_DIM))
    ang = pos_ref[...] * inv_freq
    sign = jnp.where(lane >= half, 1.0, -1.0)
    cos_ref[...] = jnp.cos(ang)
    sin_ref[...] = jnp.sin(ang) * sign


def _rope_tables(pos_col):
    n = pos_col.shape[0]
    tm = min(n, 2048)
    out = jax.ShapeDtypeStruct((n, ATT_HEAD_DIM), F32)
    return pl.pallas_call(
        _rope_table_kernel,
        out_shape=(out, out),
        grid=(n // tm,),
        in_specs=[pl.BlockSpec((tm, 1), lambda i: (i, 0))],
        out_specs=(pl.BlockSpec((tm, ATT_HEAD_DIM), lambda i: (i, 0)),) * 2,
        compiler_params=_params("parallel"),
        name="rope_tables",
    )(pos_col)


def _ffn_kernel(*refs, final):
    if final:
        x_ref, g_ref, wg_ref, wu_ref, wd_ref, fg_ref, o_ref, h_ref, acc_ref = refs
    else:
        x_ref, g_ref, wg_ref, wu_ref, wd_ref, o_ref, h_ref, acc_ref = refs
    j = pl.program_id(1)

    @pl.when(j == 0)
    def _():
        h_ref[...] = _rms(x_ref[...], g_ref[...]).astype(BF16)

    h = h_ref[...]
    gate = _dot(h, wg_ref[...])
    up = _dot(h, wu_ref[...])
    act = (gate * jax.nn.sigmoid(gate) * up).astype(BF16)
    part = _dot(act, wd_ref[...])

    @pl.when(j == 0)
    def _():
        acc_ref[...] = part

    @pl.when(j > 0)
    def _():
        acc_ref[...] += part

    @pl.when(j == pl.num_programs(1) - 1)
    def _():
        y = x_ref[...] + 0.5 * acc_ref[...]
        if final:
            y = _rms(y, fg_ref[...])
        o_ref[...] = y


def _ffn_tile(d_ff):
    best = LANES
    for mult in range(1, d_ff // LANES + 1):
        t = mult * LANES
        if d_ff % t == 0 and t <= 1536:
            best = t
    return best


def _ffn(x, g, w_gate, w_up, w_down, final_g=None):
    n, d = x.shape
    d_ff = w_gate.shape[1]
    tm = min(TOKEN_TILE, n)
    tf = _ffn_tile(d_ff)
    final = final_g is not None
    in_specs = [
        pl.BlockSpec((tm, d), lambda i, j: (i, 0)),
        pl.BlockSpec((1, d), lambda i, j: (0, 0)),
        pl.BlockSpec((d, tf), lambda i, j: (0, j)),
        pl.BlockSpec((d, tf), lambda i, j: (0, j)),
        pl.BlockSpec((tf, d), lambda i, j: (j, 0)),
    ]
    args = [x, g, w_gate, w_up, w_down]
    if final:
        in_specs.append(pl.BlockSpec((1, d), lambda i, j: (0, 0)))
        args.append(final_g)
    return pl.pallas_call(
        functools.partial(_ffn_kernel, final=final),
        out_shape=jax.ShapeDtypeStruct((n, d), F32),
        grid=(n // tm, d_ff // tf),
        in_specs=in_specs,
        out_specs=pl.BlockSpec((tm, d), lambda i, j: (i, 0)),
        scratch_shapes=[pltpu.VMEM((tm, d), BF16), pltpu.VMEM((tm, d), F32)],
        compiler_params=_params("parallel", "arbitrary"),
        name="ffn_final" if final else "ffn",
    )(*args)


def _proj_attn_kernel(x_ref, g_ref, cos_ref, sin_ref, wq_ref, wk_ref, wv_ref,
                      q_ref, k_ref, v_ref, h_ref):
    @pl.when(pl.program_id(1) == 0)
    def _():
        h_ref[...] = _rms(x_ref[...], g_ref[...]).astype(BF16)

    h = h_ref[...]
    cos = cos_ref[...]
    sin = sin_ref[...]
    scale = ATT_HEAD_DIM ** -0.5
    zq = _dot(h, wq_ref[...])
    zk = _dot(h, wk_ref[...])
    for hd in range(ATT_HEADS):
        sl = slice(hd * ATT_HEAD_DIM, (hd + 1) * ATT_HEAD_DIM)
        xq = zq[:, sl]
        xk = zk[:, sl]
        rq = xq * cos + pltpu.roll(xq, ATT_HEAD_DIM // 2, 1) * sin
        rk = xk * cos + pltpu.roll(xk, ATT_HEAD_DIM // 2, 1) * sin
        q_ref[:, sl] = (rq * scale).astype(BF16)
        k_ref[:, sl] = rk.astype(BF16)
    v_ref[...] = _dot(h, wv_ref[...]).astype(BF16)


def _proj_attn(x, g, cos2, sin2, wq, wk, wv):
    n, d = x.shape
    tm = min(TOKEN_TILE, n)
    width = wq.shape[1]
    gw = ATT_GROUP_WIDTH
    out = jax.ShapeDtypeStruct((n, width), BF16)
    row = lambda i, j: (i, 0)
    col = lambda i, j: (0, j)
    tile = lambda i, j: (i, j)
    return pl.pallas_call(
        _proj_attn_kernel,
        out_shape=(out, out, out),
        grid=(n // tm, width // gw),
        in_specs=[
            pl.BlockSpec((tm, d), row),
            pl.BlockSpec((1, d), lambda i, j: (0, 0)),
            pl.BlockSpec((tm, ATT_HEAD_DIM), row),
            pl.BlockSpec((tm, ATT_HEAD_DIM), row),
            pl.BlockSpec((d, gw), col),
            pl.BlockSpec((d, gw), col),
            pl.BlockSpec((d, gw), col),
        ],
        out_specs=(pl.BlockSpec((tm, gw), tile),) * 3,
        scratch_shapes=[pltpu.VMEM((tm, d), BF16)],
        compiler_params=_params("parallel", "arbitrary"),
        name="proj_attn",
    )(x, g, cos2, sin2, wq, wk, wv)


def _proj_hgrn_kernel(x_ref, g_ref, lbraw_ref, wq_ref, wf_ref, wi_ref, wo_ref,
                      q_ref, lf_ref, i_ref, og_ref, h_ref, *, layer):
    @pl.when(pl.program_id(1) == 0)
    def _():
        h_ref[...] = _rms(x_ref[...], g_ref[...]).astype(BF16)

    h = h_ref[...]
    raw = lbraw_ref[...]
    p = jnp.exp(raw - jnp.max(raw, axis=0, keepdims=True))
    p = p / jnp.sum(p, axis=0, keepdims=True)
    lb = jnp.sum(p[:layer + 1], axis=0, keepdims=True) - p[0:1]

    q_ref[...] = (_dot(h, wq_ref[...]) * (HGRN_HEAD_DIM ** -0.5)).astype(BF16)
    f = lb + (1.0 - lb) * jax.nn.sigmoid(_dot(h, wf_ref[...]))
    lf_ref[...] = jnp.log(f)
    i_ref[...] = _dot(h, wi_ref[...]).astype(BF16)
    og = _dot(h, wo_ref[...])
    og_ref[...] = (og * jax.nn.sigmoid(og)).astype(BF16)


def _proj_hgrn(x, g, lb_raw, wq, wf, wi, wo, layer):
    n, d = x.shape
    tm = min(TOKEN_TILE, n)
    width = wq.shape[1]
    tn = min(512, width)
    depth = lb_raw.shape[0]
    row = lambda i, j: (i, 0)
    col = lambda i, j: (0, j)
    tile = lambda i, j: (i, j)
    bf = jax.ShapeDtypeStruct((n, width), BF16)
    return pl.pallas_call(
        functools.partial(_proj_hgrn_kernel, layer=layer),
        out_shape=(bf, jax.ShapeDtypeStruct((n, width), F32), bf, bf),
        grid=(n // tm, width // tn),
        in_specs=[
            pl.BlockSpec((tm, d), row),
            pl.BlockSpec((1, d), lambda i, j: (0, 0)),
            pl.BlockSpec((depth, tn), col),
            pl.BlockSpec((d, tn), col),
            pl.BlockSpec((d, tn), col),
            pl.BlockSpec((d, tn), col),
            pl.BlockSpec((d, tn), col),
        ],
        out_specs=(pl.BlockSpec((tm, tn), tile),) * 4,
        scratch_shapes=[pltpu.VMEM((tm, d), BF16)],
        compiler_params=_params("parallel", "arbitrary"),
        name="proj_hgrn",
    )(x, g, lb_raw, wq, wf, wi, wo)


def _attn_kernel(q_ref, kp_ref, kc_ref, vp_ref, vc_ref, o_ref, st_ref):
    blk = pl.program_id(2)
    row = lax.broadcasted_iota(jnp.int32, (ATT_BLOCK, ATT_BLOCK), 0)
    colm = lax.broadcasted_iota(jnp.int32, (ATT_BLOCK, ATT_BLOCK), 1)
    mask_c = colm <= row
    mask_p = jnp.logical_and(colm >= row, blk > 0)
    lane = lax.broadcasted_iota(jnp.int32, (ATT_BLOCK, LANES), 1)
    stats = jnp.zeros((ATT_BLOCK, LANES), F32)
    for hd in range(ATT_HEADS):
        sl = slice(hd * ATT_HEAD_DIM, (hd + 1) * ATT_HEAD_DIM)
        q = q_ref[:, sl]
        s_c = jnp.where(mask_c, _dot_nt(q, kc_ref[:, sl]), NEG_BIG)
        s_p = jnp.where(mask_p, _dot_nt(q, kp_ref[:, sl]), NEG_BIG)
        m = jnp.maximum(jnp.max(s_c, axis=-1, keepdims=True),
                        jnp.max(s_p, axis=-1, keepdims=True))
        p_c = jnp.exp(s_c - m)
        p_p = jnp.exp(s_p - m)
        den = jnp.sum(p_c, axis=-1, keepdims=True) + jnp.sum(p_p, axis=-1, keepdims=True)
        o = _dot(p_c.astype(BF16), vc_ref[:, sl]) + _dot(p_p.astype(BF16), vp_ref[:, sl])
        o_ref[:, sl] = (o / den).astype(BF16)
        stats = jnp.where(lane == hd, m + jnp.log(den), stats)
    st_ref[...] = stats


def _attn_group(q, k, v, group, dil, batch):
    n, width = q.shape
    seq = n // batch
    gw = ATT_GROUP_WIDTH
    per_tok = width // gw
    cls = seq // dil
    assert cls % ATT_BLOCK == 0
    view = lambda a: a.reshape(batch, cls, dil * a.shape[1])
    cur = lambda b, r, i: (b, i, r * per_tok + group)
    prev = lambda b, r, i: (b, jnp.maximum(i - 1, 0), r * per_tok + group)
    outm = lambda b, r, i: (b, i, r)
    qkv_spec = lambda im: pl.BlockSpec((None, ATT_BLOCK, gw), im)
    o, st = pl.pallas_call(
        _attn_kernel,
        out_shape=(jax.ShapeDtypeStruct((batch, cls, dil * gw), BF16),
                   jax.ShapeDtypeStruct((batch, cls, dil * LANES), F32)),
        grid=(batch, dil, cls // ATT_BLOCK),
        in_specs=[qkv_spec(cur), qkv_spec(prev), qkv_spec(cur), qkv_spec(prev), qkv_spec(cur)],
        out_specs=(pl.BlockSpec((None, ATT_BLOCK, gw), outm),
                   pl.BlockSpec((None, ATT_BLOCK, LANES), outm)),
        compiler_params=_params("parallel", "parallel", "arbitrary"),
        name=f"attn_dil{dil}",
    )(view(q), view(k), view(k), view(v), view(v))
    return o.reshape(n, gw), st.reshape(n, LANES)


def _hgrn_kernel(q_ref, lf_ref, i_ref, og_ref, hn_ref, o_ref, st_ref, *, chunk, heads):
    c = chunk
    w = heads * HGRN_HEAD_DIM
    hdim = HGRN_HEAD_DIM

    @pl.when(pl.program_id(1) == 0)
    def _():
        st_ref[...] = jnp.zeros_like(st_ref)

    q = q_ref[...].astype(F32)
    lf = lf_ref[...]
    v_bf = i_ref[...]

    r_i = lax.broadcasted_iota(jnp.int32, (c, c), 0)
    c_i = lax.broadcasted_iota(jnp.int32, (c, c), 1)
    tri = jnp.where(c_i <= r_i, 1.0, 0.0).astype(BF16)
    lf_hi = lf.astype(BF16)
    rem = lf - lf_hi.astype(F32)
    lf_mid = rem.astype(BF16)
    lf_lo = (rem - lf_mid.astype(F32)).astype(BF16)
    b = _dot(tri, lf_hi) + _dot(tri, lf_mid) + _dot(tri, lf_lo)

    kk = 1.0 - jnp.exp(lf)
    b_last = b[c - 1:c, :]
    q_in = (q * jnp.exp(b)).astype(BF16)
    k_out = (kk * jnp.exp(b_last - b)).astype(BF16)
    st_decay = jnp.exp(b_last)

    rows = lax.broadcasted_iota(jnp.int32, (c, w), 0)

    levels = []
    half = SUBLANES
    while half < c:
        span = 2 * half
        pieces = [jnp.broadcast_to(b[s + half - 1:s + half, :], (span, w))
                  for s in range(0, c, span)]
        bref = pieces[0] if len(pieces) == 1 else jnp.concatenate(pieces, axis=0)
        upper = (rows & (span - 1)) >= half
        e = jnp.exp(jnp.where(upper, b - bref, bref - b))
        q_up = jnp.where(upper, q * e, 0.0).astype(BF16)
        k_lo = jnp.where(upper, 0.0, kk * e).astype(BF16)
        same = (r_i & ~(span - 1)) == (c_i & ~(span - 1))
        levels.append((q_up, k_lo, same))
        half = span

    nb = c // SUBLANES
    sub = lax.broadcasted_iota(jnp.int32, (nb, SUBLANES, w), 1)
    q3 = q.reshape(nb, SUBLANES, w)
    k3 = kk.reshape(nb, SUBLANES, w)
    b3 = b.reshape(nb, SUBLANES, w)
    near = []
    for dist in range(SUBLANES):
        if dist == 0:
            prod = q3 * k3
        else:
            k_r = pltpu.roll(k3, dist, 1)
            b_r = pltpu.roll(b3, dist, 1)
            prod = jnp.where(sub >= dist, q3 * k_r * jnp.exp(b3 - b_r), 0.0)
        near.append(prod.reshape(c, w))

    gain = hn_ref[...]
    for hd in range(heads):
        sl = slice(hd * hdim, (hd + 1) * hdim)
        a = jnp.zeros((c, c), F32)
        for q_up, k_lo, same in levels:
            a = a + jnp.where(same, _dot_nt(q_up[:, sl], k_lo[:, sl]), 0.0)
        for dist in range(SUBLANES):
            col_val = jnp.sum(near[dist][:, sl], axis=-1, keepdims=True)
            a = a + jnp.where(c_i == r_i - dist, col_val, 0.0)
        v_h = v_bf[:, sl]
        st_h = st_ref[hd]
        o_h = _dot(a.astype(BF16), v_h) + _dot_nt(q_in[:, sl], st_h.astype(BF16))
        st_ref[hd] = st_h * st_decay[:, sl] + _dot_tn(v_h, k_out[:, sl])
        o_n = _rms(o_h, gain[:, sl])
        o_ref[:, sl] = (o_n * og_ref[:, sl].astype(F32)).astype(BF16)


def _hgrn(q, lf, i, og, head_gain, batch):
    n, w = q.shape
    seq = n // batch
    c = min(HGRN_CHUNK, seq)
    heads = w // HGRN_HEAD_DIM
    nc = seq // c
    blk = lambda b, t: (b * nc + t, 0)
    spec = pl.BlockSpec((c, w), blk)
    return pl.pallas_call(
        functools.partial(_hgrn_kernel, chunk=c, heads=heads),
        out_shape=jax.ShapeDtypeStruct((n, w), BF16),
        grid=(batch, nc),
        in_specs=[spec, spec, spec, spec, pl.BlockSpec((1, w), lambda b, t: (0, 0))],
        out_specs=spec,
        scratch_shapes=[pltpu.VMEM((heads, HGRN_HEAD_DIM, HGRN_HEAD_DIM), F32)],
        compiler_params=_params("parallel", "arbitrary"),
        name="hgrn",
    )(q, lf, i, og, head_gain)


def _mix_kernel(x_ref, g_ref, o0_ref, o1_ref, o2_ref, s0_ref, s1_ref, s2_ref, ob_ref,
                wga_ref, wgb_ref, wa_ref, wb_ref, wo_ref, out_ref):
    x = x_ref[...]
    h = _rms(x, g_ref[...]).astype(BF16)
    gate_a = jax.nn.sigmoid(_dot(h, wga_ref[...]))
    gate_b = jax.nn.sigmoid(_dot(h, wgb_ref[...]))

    s0, s1, s2 = s0_ref[...], s1_ref[...], s2_ref[...]
    mx = jnp.maximum(jnp.maximum(s0, s1), s2)
    e0, e1, e2 = jnp.exp(s0 - mx), jnp.exp(s1 - mx), jnp.exp(s2 - mx)
    inv = 1.0 / (e0 + e1 + e2)
    w0, w1, w2 = e0 * inv, e1 * inv, e2 * inv
    parts = []
    for hd in range(ATT_HEADS):
        sl = slice(hd * ATT_HEAD_DIM, (hd + 1) * ATT_HEAD_DIM)
        parts.append(o0_ref[:, sl].astype(F32) * w0[:, hd:hd + 1]
                     + o1_ref[:, sl].astype(F32) * w1[:, hd:hd + 1]
                     + o2_ref[:, sl].astype(F32) * w2[:, hd:hd + 1])
    att = jnp.concatenate(parts, axis=-1).astype(BF16)

    y_a = _dot(att, wa_ref[...])
    y_b = _dot(ob_ref[...], wb_ref[...])
    merged = (gate_a * y_a + gate_b * y_b).astype(BF16)
    out_ref[...] = x + _dot(merged, wo_ref[...])


def _mix(x, g, o_groups, s_groups, ob, wga, wgb, wa, wb, wo):
    n, d = x.shape
    tm = min(TOKEN_TILE, n)
    row = lambda i: (i, 0)
    full = lambda a: pl.BlockSpec(a.shape, lambda i: (0, 0))
    rows = lambda a: pl.BlockSpec((tm, a.shape[1]), row)
    args = [x, g, *o_groups, *s_groups, ob, wga, wgb, wa, wb, wo]
    in_specs = ([rows(x), full(g)] + [rows(a) for a in o_groups] + [rows(a) for a in s_groups]
                + [rows(ob)] + [full(a) for a in (wga, wgb, wa, wb, wo)])
    return pl.pallas_call(
        _mix_kernel,
        out_shape=jax.ShapeDtypeStruct((n, d), F32),
        grid=(n // tm,),
        in_specs=in_specs,
        out_specs=pl.BlockSpec((tm, d), row),
        compiler_params=_params("parallel"),
        name="mix_out",
    )(*args)


def _mem_kv_kernel(m_ref, g_ref, wk_ref, wv_ref, k_ref, v_ref):
    h = _rms(m_ref[...], g_ref[...]).astype(BF16)
    k_ref[...] = _dot(h, wk_ref[...]).astype(BF16)
    v_ref[...] = _dot(h, wv_ref[...]).astype(BF16)


def _mem_kv(mem, g, wk, wv):
    n, d = mem.shape
    tm = min(TOKEN_TILE, n)
    row = lambda i: (i, 0)
    full = lambda a: pl.BlockSpec(a.shape, lambda i: (0, 0))
    out = jax.ShapeDtypeStruct((n, wk.shape[1]), BF16)
    return pl.pallas_call(
        _mem_kv_kernel,
        out_shape=(out, out),
        grid=(n // tm,),
        in_specs=[pl.BlockSpec((tm, d), row), full(g), full(wk), full(wv)],
        out_specs=(pl.BlockSpec((tm, wk.shape[1]), row),) * 2,
        compiler_params=_params("parallel"),
        name="mem_kv",
    )(mem, g, wk, wv)


def _xattn_kernel(x_ref, g_ref, k_ref, v_ref, wq_ref, wo_ref, out_ref):
    x = x_ref[...]
    h = _rms(x, g_ref[...]).astype(BF16)
    dh = wq_ref.shape[1] // X_HEADS
    q = (_dot(h, wq_ref[...]) * (dh ** -0.5)).astype(BF16)
    outs = []
    for hd in range(X_HEADS):
        sl = slice(hd * dh, (hd + 1) * dh)
        s = _dot_nt(q[:, sl], k_ref[:, sl])
        m = jnp.max(s, axis=-1, keepdims=True)
        p = jnp.exp(s - m)
        den = jnp.sum(p, axis=-1, keepdims=True)
        outs.append(_dot(p.astype(BF16), v_ref[:, sl]) / den)
    o = jnp.concatenate(outs, axis=-1).astype(BF16)
    out_ref[...] = x + _dot(o, wo_ref[...])


def _xattn(x, g, k, v, wq, wo, batch):
    n, d = x.shape
    seq = n // batch
    tm = min(TOKEN_TILE, seq)
    n_mem = k.shape[0] // batch
    per_b = seq // tm
    row = lambda b, i: (b * per_b + i, 0)
    full = lambda a: pl.BlockSpec(a.shape, lambda b, i: (0, 0))
    kv = pl.BlockSpec((n_mem, k.shape[1]), lambda b, i: (b, 0))
    return pl.pallas_call(
        _xattn_kernel,
        out_shape=jax.ShapeDtypeStruct((n, d), F32),
        grid=(batch, per_b),
        in_specs=[pl.BlockSpec((tm, d), row), full(g), kv, kv, full(wq), full(wo)],
        out_specs=pl.BlockSpec((tm, d), row),
        compiler_params=_params("parallel", "parallel"),
        name="xattn",
    )(x, g, k, v, wq, wo)


def kernel(x, mem, positions, ffn1_norm, ffn1_w_gu, ffn1_w_down, mix_norm, w_in,
           hgrn_lower_bounds, hgrn_head_norm, w_att_branch, w_hgrn_branch, w_mix_out,
           xattn_norm, mem_norm, xattn_wq, xattn_wkv, xattn_wo,
           ffn2_norm, ffn2_w_gu, ffn2_w_down, final_norm):
    batch, seq, d = x.shape
    depth = w_in.shape[0]
    n = batch * seq
    d_ff = ffn1_w_down.shape[1]
    att_w = len(ATT_GROUPS) * ATT_GROUP_WIDTH
    hg_w = hgrn_head_norm.shape[1]
    x_w = xattn_wq.shape[2]

    xf = x.reshape(n, d)
    memf = mem.reshape(batch * mem.shape[1], d)
    cos2, sin2 = _rope_tables(positions.reshape(n, 1).astype(F32))

    sizes = (att_w,) * 3 + (hg_w,) * 4 + (d,) * 2
    offs = [sum(sizes[:i]) for i in range(len(sizes) + 1)]

    def w_in_part(l, idx):
        return w_in[l, :, offs[idx]:offs[idx + 1]].astype(BF16)

    row2 = lambda a, l: a[l].reshape(1, -1)

    for l in range(depth):
        xf = _ffn(xf, row2(ffn1_norm, l), ffn1_w_gu[l, :, :d_ff].astype(BF16),
                  ffn1_w_gu[l, :, d_ff:].astype(BF16), ffn1_w_down[l].astype(BF16))

        g_mix = row2(mix_norm, l)
        q_a, k_a, v_a = _proj_attn(xf, g_mix, cos2, sin2,
                                   w_in_part(l, 0), w_in_part(l, 1), w_in_part(l, 2))
        q_b, lf_b, i_b, og_b = _proj_hgrn(xf, g_mix, hgrn_lower_bounds,
                                          w_in_part(l, 3), w_in_part(l, 4),
                                          w_in_part(l, 5), w_in_part(l, 6), l)
        o_groups, s_groups = [], []
        for gi, (_, dil) in enumerate(ATT_GROUPS):
            o_g, s_g = _attn_group(q_a, k_a, v_a, gi, dil, batch)
            o_groups.append(o_g)
            s_groups.append(s_g)
        o_b = _hgrn(q_b, lf_b, i_b, og_b, row2(hgrn_head_norm, l), batch)
        xf = _mix(xf, g_mix, o_groups, s_groups, o_b,
                  w_in_part(l, 7), w_in_part(l, 8),
                  w_att_branch[l].astype(BF16), w_hgrn_branch[l].astype(BF16),
                  w_mix_out[l].astype(BF16))

        k_x, v_x = _mem_kv(memf, row2(mem_norm, l), xattn_wkv[l, :, :x_w].astype(BF16),
                           xattn_wkv[l, :, x_w:].astype(BF16))
        xf = _xattn(xf, row2(xattn_norm, l), k_x, v_x,
                    xattn_wq[l].astype(BF16), xattn_wo[l].astype(BF16), batch)

        last = l == depth - 1
        xf = _ffn(xf, row2(ffn2_norm, l), ffn2_w_gu[l, :, :d_ff].astype(BF16),
                  ffn2_w_gu[l, :, d_ff:].astype(BF16), ffn2_w_down[l].astype(BF16),
                  final_g=final_norm.reshape(1, -1) if last else None)

    return xf.reshape(batch, seq, d)
```

```python
import functools
import math

import jax
import jax.numpy as jnp
from jax import lax
from jax.experimental import pallas as pl
from jax.experimental.pallas import tpu as pltpu

F32 = jnp.float32
BF16 = jnp.bfloat16

ATT_GROUPS = ((128, 1), (512, 4), (2048, 16))
ATT_HEADS = 4
ATT_HEAD_DIM = 128
ATT_GROUP_WIDTH = ATT_HEADS * ATT_HEAD_DIM
HGRN_HEAD_DIM = 128
X_HEADS = 4
ROPE_THETA = 10000.0
EPS = 1e-6

LANES = 128
SUBLANES = 8
VMEM_LIMIT_BYTES = 56 * 1024 * 1024

TOKEN_TILE = 512
ATT_BLOCK = 128
HGRN_CHUNK = 64
NEG_BIG = -1e30


def _params(*sem):
    return pltpu.CompilerParams(dimension_semantics=sem,
                                vmem_limit_bytes=VMEM_LIMIT_BYTES)


def _rms(x, g):
    ms = jnp.mean(x * x, axis=-1, keepdims=True)
    return x * lax.rsqrt(ms + EPS) * g


def _dot(a, b):
    return jnp.dot(a, b, preferred_element_type=F32)


def _dot_nt(a, b):
    return lax.dot_general(a, b, (((1,), (1,)), ((), ())), preferred_element_type=F32)


def _dot_tn(a, b):
    return lax.dot_general(a, b, (((0,), (0,)), ((), ())), preferred_element_type=F32)


def _rope_table_kernel(pos_ref, cos_ref, sin_ref):
    half = ATT_HEAD_DIM // 2
    lane = lax.broadcasted_iota(jnp.int32, (1, ATT_HEAD_DIM), 1)
    idx = jnp.where(lane >= half, lane - half, lane).astype(F32)
    inv_freq = jnp.exp(idx * (-2.0 * math.log(ROPE_THETA) / ATT_HEAD_DIM))
    ang = pos_ref[...] * inv_freq
    sign = jnp.where(lane >= half, 1.0, -1.0)
    cos_ref[...] = jnp.cos(ang)
    sin_ref[...] = jnp.sin(ang) * sign


def _rope_tables(pos_col):
    n = pos_col.shape[0]
    tm = min(n, 2048)
    out = jax.ShapeDtypeStruct((n, ATT_HEAD_DIM), F32)
    return pl.pallas_call(
        _rope_table_kernel,
        out_shape=(out, out),
        grid=(n // tm,),
        in_specs=[pl.BlockSpec((tm, 1), lambda i: (i, 0))],
        out_specs=(pl.BlockSpec((tm, ATT_HEAD_DIM), lambda i: (i, 0)),) * 2,
        compiler_params=_params("parallel"),
        name="rope_tables",
    )(pos_col)


def _ffn_kernel(*refs, final):
    if final:
        x_ref, g_ref, wg_ref, wu_ref, wd_ref, fg_ref, o_ref, h_ref, acc_ref = refs
    else:
        x_ref, g_ref, wg_ref, wu_ref, wd_ref, o_ref, h_ref, acc_ref = refs
    j = pl.program_id(1)

    @pl.when(j == 0)
    def _():
        h_ref[...] = _rms(x_ref[...], g_ref[...]).astype(BF16)

    h = h_ref[...]
    gate = _dot(h, wg_ref[...])
    up = _dot(h, wu_ref[...])
    act = (gate * jax.nn.sigmoid(gate) * up).astype(BF16)
    part = _dot(act, wd_ref[...])

    @pl.when(j == 0)
    def _():
        acc_ref[...] = part

    @pl.when(j > 0)
    def _():
        acc_ref[...] += part

    @pl.when(j == pl.num_programs(1) - 1)
    def _():
        y = x_ref[...] + 0.5 * acc_ref[...]
        if final:
            y = _rms(y, fg_ref[...])
        o_ref[...] = y


def _ffn_tile(d_ff):
    best = LANES
    for mult in range(1, d_ff // LANES + 1):
        t = mult * LANES
        if d_ff % t == 0 and t <= 1536:
            best = t
    return best


def _ffn(x, g, w_gate, w_up, w_down, final_g=None):
    n, d = x.shape
    d_ff = w_gate.shape[1]
    tm = min(TOKEN_TILE, n)
    tf = _ffn_tile(d_ff)
    final = final_g is not None
    in_specs = [
        pl.BlockSpec((tm, d), lambda i, j: (i, 0)),
        pl.BlockSpec((1, d), lambda i, j: (0, 0)),
        pl.BlockSpec((d, tf), lambda i, j: (0, j)),
        pl.BlockSpec((d, tf), lambda i, j: (0, j)),
        pl.BlockSpec((tf, d), lambda i, j: (j, 0)),
    ]
    args = [x, g, w_gate, w_up, w_down]
    if final:
        in_specs.append(pl.BlockSpec((1, d), lambda i, j: (0, 0)))
        args.append(final_g)
    return pl.pallas_call(
        functools.partial(_ffn_kernel, final=final),
        out_shape=jax.ShapeDtypeStruct((n, d), F32),
        grid=(n // tm, d_ff // tf),
        in_specs=in_specs,
        out_specs=pl.BlockSpec((tm, d), lambda i, j: (i, 0)),
        scratch_shapes=[pltpu.VMEM((tm, d), BF16), pltpu.VMEM((tm, d), F32)],
        compiler_params=_params("parallel", "arbitrary"),
        name="ffn_final" if final else "ffn",
    )(*args)


def _store_by_class(out_ref, scr_ref, val, dil):
    if dil == 1:
        out_ref[0] = val.astype(out_ref.dtype)
        return
    rows = val.shape[0] // dil
    for c in range(val.shape[1] // LANES):
        cols = slice(c * LANES, (c + 1) * LANES)
        scr_ref[c] = val[:, cols]
        for r in range(dil):
            out_ref[r, :, cols] = scr_ref[c, pl.ds(r, rows, stride=dil), :].astype(out_ref.dtype)


def _proj_attn_kernel(x_ref, g_ref, cos_ref, sin_ref, wq_ref, wk_ref, wv_ref,
                      q_ref, k_ref, v_ref, *scr, dil):
    h = _rms(x_ref[...], g_ref[...]).astype(BF16)
    cos = cos_ref[...]
    sin = sin_ref[...]
    scale = ATT_HEAD_DIM ** -0.5
    zq = _dot(h, wq_ref[...])
    zk = _dot(h, wk_ref[...])
    rq, rk = [], []
    for hd in range(ATT_HEADS):
        sl = slice(hd * ATT_HEAD_DIM, (hd + 1) * ATT_HEAD_DIM)
        xq = zq[:, sl]
        xk = zk[:, sl]
        rq.append((xq * cos + pltpu.roll(xq, ATT_HEAD_DIM // 2, 1) * sin) * scale)
        rk.append(xk * cos + pltpu.roll(xk, ATT_HEAD_DIM // 2, 1) * sin)
    scr = scr if scr else (None,) * 3
    _store_by_class(q_ref, scr[0], jnp.concatenate(rq, axis=-1), dil)
    _store_by_class(k_ref, scr[1], jnp.concatenate(rk, axis=-1), dil)
    _store_by_class(v_ref, scr[2], _dot(h, wv_ref[...]), dil)


def _proj_attn(x, g, cos2, sin2, wq, wk, wv, dil, batch):
    n, d = x.shape
    seq = n // batch
    tm = min(TOKEN_TILE, seq)
    per_b = seq // tm
    gw = wq.shape[1]
    out = jax.ShapeDtypeStruct((batch, dil, seq // dil, gw), BF16)
    row = lambda b, i: (b * per_b + i, 0)
    full = lambda a: pl.BlockSpec(a.shape, lambda b, i: (0, 0))
    scratch = [] if dil == 1 else [pltpu.VMEM((gw // LANES, tm, LANES), F32)] * 3
    return pl.pallas_call(
        functools.partial(_proj_attn_kernel, dil=dil),
        out_shape=(out, out, out),
        grid=(batch, per_b),
        in_specs=[
            pl.BlockSpec((tm, d), row),
            full(g),
            pl.BlockSpec((tm, ATT_HEAD_DIM), row),
            pl.BlockSpec((tm, ATT_HEAD_DIM), row),
            full(wq), full(wk), full(wv),
        ],
        out_specs=(pl.BlockSpec((None, dil, tm // dil, gw), lambda b, i: (b, 0, i, 0)),) * 3,
        scratch_shapes=scratch,
        compiler_params=_params("parallel", "parallel"),
        name=f"proj_attn_dil{dil}",
    )(x, g, cos2, sin2, wq, wk, wv)


def _proj_hgrn_kernel(x_ref, g_ref, lbraw_ref, wq_ref, wf_ref, wi_ref, wo_ref,
                      q_ref, lf_ref, i_ref, og_ref, h_ref, *, layer):
    @pl.when(pl.program_id(1) == 0)
    def _():
        h_ref[...] = _rms(x_ref[...], g_ref[...]).astype(BF16)

    h = h_ref[...]
    raw = lbraw_ref[...]
    p = jnp.exp(raw - jnp.max(raw, axis=0, keepdims=True))
    p = p / jnp.sum(p, axis=0, keepdims=True)
    lb = jnp.sum(p[:layer + 1], axis=0, keepdims=True) - p[0:1]

    q_ref[...] = (_dot(h, wq_ref[...]) * (HGRN_HEAD_DIM ** -0.5)).astype(BF16)
    f = lb + (1.0 - lb) * jax.nn.sigmoid(_dot(h, wf_ref[...]))
    lf_ref[...] = jnp.log(f)
    i_ref[...] = _dot(h, wi_ref[...]).astype(BF16)
    og = _dot(h, wo_ref[...])
    og_ref[...] = (og * jax.nn.sigmoid(og)).astype(BF16)


def _proj_hgrn(x, g, lb_raw, wq, wf, wi, wo, layer):
    n, d = x.shape
    tm = min(TOKEN_TILE, n)
    width = wq.shape[1]
    tn = min(512, width)
    depth = lb_raw.shape[0]
    row = lambda i, j: (i, 0)
    col = lambda i, j: (0, j)
    tile = lambda i, j: (i, j)
    bf = jax.ShapeDtypeStruct((n, width), BF16)
    return pl.pallas_call(
        functools.partial(_proj_hgrn_kernel, layer=layer),
        out_shape=(bf, jax.ShapeDtypeStruct((n, width), F32), bf, bf),
        grid=(n // tm, width // tn),
        in_specs=[
            pl.BlockSpec((tm, d), row),
            pl.BlockSpec((1, d), lambda i, j: (0, 0)),
            pl.BlockSpec((depth, tn), col),
            pl.BlockSpec((d, tn), col),
            pl.BlockSpec((d, tn), col),
            pl.BlockSpec((d, tn), col),
            pl.BlockSpec((d, tn), col),
        ],
        out_specs=(pl.BlockSpec((tm, tn), tile),) * 4,
        scratch_shapes=[pltpu.VMEM((tm, d), BF16)],
        compiler_params=_params("parallel", "arbitrary"),
        name="proj_hgrn",
    )(x, g, lb_raw, wq, wf, wi, wo)


def _attn_kernel(q_ref, kp_ref, kc_ref, vp_ref, vc_ref, o_ref, st_ref, *, nblk):
    first = pl.program_id(2) == 0
    row = lax.broadcasted_iota(jnp.int32, (ATT_BLOCK, 2 * ATT_BLOCK), 0)
    col = lax.broadcasted_iota(jnp.int32, (ATT_BLOCK, 2 * ATT_BLOCK), 1)
    band = jnp.logical_and(col >= row, col <= row + ATT_BLOCK)
    low = jnp.where(first, ATT_BLOCK, 0)
    band_first = jnp.logical_and(col >= jnp.maximum(row, low), col <= row + ATT_BLOCK)
    ones = jnp.ones((2 * ATT_BLOCK, ATT_HEAD_DIM), BF16)
    lane = lax.broadcasted_iota(jnp.int32, (ATT_BLOCK, LANES), 1)
    for blk in range(nblk):
        rows = slice(blk * ATT_BLOCK, (blk + 1) * ATT_BLOCK)
        stats = jnp.zeros((ATT_BLOCK, LANES), F32)
        for hd in range(ATT_HEADS):
            sl = slice(hd * ATT_HEAD_DIM, (hd + 1) * ATT_HEAD_DIM)
            if blk == 0:
                k2 = jnp.concatenate([kp_ref[:, sl], kc_ref[0:ATT_BLOCK, sl]], axis=0)
                v2 = jnp.concatenate([vp_ref[:, sl], vc_ref[0:ATT_BLOCK, sl]], axis=0)
                mask = band_first
            else:
                win = slice((blk - 1) * ATT_BLOCK, (blk + 1) * ATT_BLOCK)
                k2 = kc_ref[win, sl]
                v2 = vc_ref[win, sl]
                mask = band
            s = jnp.where(mask, _dot_nt(q_ref[rows, sl], k2), NEG_BIG)
            m = jnp.max(s, axis=-1, keepdims=True)
            p = jnp.exp(s - m).astype(BF16)
            o_den = _dot(p, jnp.concatenate([v2, ones], axis=1))
            den = o_den[:, ATT_HEAD_DIM:]
            o_ref[rows, sl] = (o_den[:, :ATT_HEAD_DIM] / den).astype(BF16)
            stats = jnp.where(lane == hd, m + jnp.log(den), stats)
        st_ref[rows, :] = stats


def _attn_group(q, k, v):
    batch, dil, cls, gw = q.shape
    assert cls % ATT_BLOCK == 0
    nblk = min(4, cls // ATT_BLOCK)
    qb = nblk * ATT_BLOCK
    cur = lambda b, r, i: (b, r, i, 0)
    prev = lambda b, r, i: (b, r, jnp.maximum(i * nblk - 1, 0), 0)
    cur_spec = pl.BlockSpec((None, None, qb, gw), cur)
    prev_spec = pl.BlockSpec((None, None, ATT_BLOCK, gw), prev)
    return pl.pallas_call(
        functools.partial(_attn_kernel, nblk=nblk),
        out_shape=(jax.ShapeDtypeStruct((batch, dil, cls, gw), BF16),
                   jax.ShapeDtypeStruct((batch, dil, cls, LANES), F32)),
        grid=(batch, dil, cls // qb),
        in_specs=[cur_spec, prev_spec, cur_spec, prev_spec, cur_spec],
        out_specs=(pl.BlockSpec((None, None, qb, gw), cur),
                   pl.BlockSpec((None, None, qb, LANES), cur)),
        compiler_params=_params("parallel", "parallel", "arbitrary"),
        name=f"attn_dil{dil}",
    )(q, k, k, v, v)


def _hgrn_kernel(q_ref, lf_ref, i_ref, og_ref, hn_ref, o_ref, st_ref, *, chunk, heads):
    c = chunk
    w = heads * HGRN_HEAD_DIM
    hdim = HGRN_HEAD_DIM

    @pl.when(pl.program_id(1) == 0)
    def _():
        st_ref[...] = jnp.zeros_like(st_ref)

    q = q_ref[...].astype(F32)
    lf = lf_ref[...]
    v_bf = i_ref[...]

    r_i = lax.broadcasted_iota(jnp.int32, (c, c), 0)
    c_i = lax.broadcasted_iota(jnp.int32, (c, c), 1)
    tri = jnp.where(c_i <= r_i, 1.0, 0.0).astype(BF16)
    lf_hi = lf.astype(BF16)
    rem = lf - lf_hi.astype(F32)
    lf_mid = rem.astype(BF16)
    lf_lo = (rem - lf_mid.astype(F32)).astype(BF16)
    b = _dot(tri, lf_hi) + _dot(tri, lf_mid) + _dot(tri, lf_lo)

    kk = 1.0 - jnp.exp(lf)
    b_last = b[c - 1:c, :]
    q_in = (q * jnp.exp(b)).astype(BF16)
    k_out = (kk * jnp.exp(b_last - b)).astype(BF16)
    st_decay = jnp.exp(b_last)

    rows = lax.broadcasted_iota(jnp.int32, (c, w), 0)

    levels = []
    half = SUBLANES
    while half < c:
        span = 2 * half
        pieces = [jnp.broadcast_to(b[s + half - 1:s + half, :], (span, w))
                  for s in range(0, c, span)]
        bref = pieces[0] if len(pieces) == 1 else jnp.concatenate(pieces, axis=0)
        upper = (rows & (span - 1)) >= half
        e = jnp.exp(jnp.where(upper, b - bref, bref - b))
        q_up = jnp.where(upper, q * e, 0.0).astype(BF16)
        k_lo = jnp.where(upper, 0.0, kk * e).astype(BF16)
        same = (r_i & ~(span - 1)) == (c_i & ~(span - 1))
        levels.append((q_up, k_lo, same))
        half = span

    nb = c // SUBLANES
    sub = lax.broadcasted_iota(jnp.int32, (nb, SUBLANES, w), 1)
    q3 = q.reshape(nb, SUBLANES, w)
    k3 = kk.reshape(nb, SUBLANES, w)
    b3 = b.reshape(nb, SUBLANES, w)
    near = []
    for dist in range(SUBLANES):
        if dist == 0:
            prod = q3 * k3
        else:
            k_r = pltpu.roll(k3, dist, 1)
            b_r = pltpu.roll(b3, dist, 1)
            prod = jnp.where(sub >= dist, q3 * k_r * jnp.exp(b3 - b_r), 0.0)
        near.append(prod.reshape(c, w))

    gain = hn_ref[...]
    for hd in range(heads):
        sl = slice(hd * hdim, (hd + 1) * hdim)
        a = jnp.zeros((c, c), F32)
        for q_up, k_lo, same in levels:
            a = a + jnp.where(same, _dot_nt(q_up[:, sl], k_lo[:, sl]), 0.0)
        for dist in range(SUBLANES):
            col_val = jnp.sum(near[dist][:, sl], axis=-1, keepdims=True)
            a = a + jnp.where(c_i == r_i - dist, col_val, 0.0)
        v_h = v_bf[:, sl]
        st_h = st_ref[hd]
        o_h = _dot(a.astype(BF16), v_h) + _dot_nt(q_in[:, sl], st_h.astype(BF16))
        st_ref[hd] = st_h * st_decay[:, sl] + _dot_tn(v_h, k_out[:, sl])
        o_n = _rms(o_h, gain[:, sl])
        o_ref[:, sl] = (o_n * og_ref[:, sl].astype(F32)).astype(BF16)


def _hgrn(q, lf, i, og, head_gain, batch):
    n, w = q.shape
    seq = n // batch
    c = min(HGRN_CHUNK, seq)
    heads = w // HGRN_HEAD_DIM
    nc = seq // c
    blk = lambda b, t: (b * nc + t, 0)
    spec = pl.BlockSpec((c, w), blk)
    return pl.pallas_call(
        functools.partial(_hgrn_kernel, chunk=c, heads=heads),
        out_shape=jax.ShapeDtypeStruct((n, w), BF16),
        grid=(batch, nc),
        in_specs=[spec, spec, spec, spec, pl.BlockSpec((1, w), lambda b, t: (0, 0))],
        out_specs=spec,
        scratch_shapes=[pltpu.VMEM((heads, HGRN_HEAD_DIM, HGRN_HEAD_DIM), F32)],
        compiler_params=_params("parallel", "arbitrary"),
        name="hgrn",
    )(q, lf, i, og, head_gain)


def _load_by_token(src_ref, scr_ref):
    dil, rows, width = src_ref.shape
    if dil == 1:
        return src_ref[0].astype(F32)
    parts = []
    for c in range(width // LANES):
        cols = slice(c * LANES, (c + 1) * LANES)
        for r in range(dil):
            scr_ref[c, pl.ds(r, rows, stride=dil), :] = src_ref[r, :, cols].astype(F32)
        parts.append(scr_ref[c])
    return parts[0] if len(parts) == 1 else jnp.concatenate(parts, axis=-1)


def _mix_kernel(x_ref, g_ref, o0_ref, o1_ref, o2_ref, s0_ref, s1_ref, s2_ref, ob_ref,
                wga_ref, wgb_ref, wa_ref, wb_ref, wo_ref, out_ref, *scr):
    x = x_ref[...]
    h = _rms(x, g_ref[...]).astype(BF16)
    gate_a = jax.nn.sigmoid(_dot(h, wga_ref[...]))
    gate_b = jax.nn.sigmoid(_dot(h, wgb_ref[...]))

    scr = list(scr)
    take = lambda ref: _load_by_token(ref, scr.pop(0) if ref.shape[0] > 1 else None)
    o0, o1, o2 = take(o0_ref), take(o1_ref), take(o2_ref)
    s0, s1, s2 = take(s0_ref), take(s1_ref), take(s2_ref)
    mx = jnp.maximum(jnp.maximum(s0, s1), s2)
    e0, e1, e2 = jnp.exp(s0 - mx), jnp.exp(s1 - mx), jnp.exp(s2 - mx)
    inv = 1.0 / (e0 + e1 + e2)
    w0, w1, w2 = e0 * inv, e1 * inv, e2 * inv
    parts = []
    for hd in range(ATT_HEADS):
        sl = slice(hd * ATT_HEAD_DIM, (hd + 1) * ATT_HEAD_DIM)
        parts.append(o0[:, sl] * w0[:, hd:hd + 1] + o1[:, sl] * w1[:, hd:hd + 1]
                     + o2[:, sl] * w2[:, hd:hd + 1])
    att = jnp.concatenate(parts, axis=-1).astype(BF16)

    y_a = _dot(att, wa_ref[...])
    y_b = _dot(ob_ref[...], wb_ref[...])
    merged = (gate_a * y_a + gate_b * y_b).astype(BF16)
    out_ref[...] = x + _dot(merged, wo_ref[...])


def _mix(x, g, o_groups, s_groups, ob, wga, wgb, wa, wb, wo):
    n, d = x.shape
    batch = o_groups[0].shape[0]
    seq = n // batch
    tm = min(TOKEN_TILE, seq)
    per_b = seq // tm
    row = lambda b, i: (b * per_b + i, 0)
    full = lambda a: pl.BlockSpec(a.shape, lambda b, i: (0, 0))
    rows = lambda a: pl.BlockSpec((tm, a.shape[1]), row)
    by_class = lambda a: pl.BlockSpec((None, a.shape[1], tm // a.shape[1], a.shape[3]),
                                      lambda b, i: (b, 0, i, 0))
    grouped = [*o_groups, *s_groups]
    args = [x, g, *grouped, ob, wga, wgb, wa, wb, wo]
    in_specs = ([rows(x), full(g)] + [by_class(a) for a in grouped]
                + [rows(ob)] + [full(a) for a in (wga, wgb, wa, wb, wo)])
    scratch = [pltpu.VMEM((a.shape[3] // LANES, tm, LANES), F32)
               for a in grouped if a.shape[1] > 1]
    return pl.pallas_call(
        _mix_kernel,
        out_shape=jax.ShapeDtypeStruct((n, d), F32),
        grid=(batch, per_b),
        in_specs=in_specs,
        out_specs=pl.BlockSpec((tm, d), row),
        scratch_shapes=scratch,
        compiler_params=_params("parallel", "parallel"),
        name="mix_out",
    )(*args)


def _mem_kv_kernel(m_ref, g_ref, wk_ref, wv_ref, k_ref, v_ref):
    h = _rms(m_ref[...], g_ref[...]).astype(BF16)
    k_ref[...] = _dot(h, wk_ref[...]).astype(BF16)
    v_ref[...] = _dot(h, wv_ref[...]).astype(BF16)


def _mem_kv(mem, g, wk, wv):
    n, d = mem.shape
    tm = min(TOKEN_TILE, n)
    row = lambda i: (i, 0)
    full = lambda a: pl.BlockSpec(a.shape, lambda i: (0, 0))
    out = jax.ShapeDtypeStruct((n, wk.shape[1]), BF16)
    return pl.pallas_call(
        _mem_kv_kernel,
        out_shape=(out, out),
        grid=(n // tm,),
        in_specs=[pl.BlockSpec((tm, d), row), full(g), full(wk), full(wv)],
        out_specs=(pl.BlockSpec((tm, wk.shape[1]), row),) * 2,
        compiler_params=_params("parallel"),
        name="mem_kv",
    )(mem, g, wk, wv)


def _xattn_kernel(x_ref, g_ref, k_ref, v_ref, wq_ref, wo_ref, out_ref):
    x = x_ref[...]
    h = _rms(x, g_ref[...]).astype(BF16)
    dh = wq_ref.shape[1] // X_HEADS
    q = (_dot(h, wq_ref[...]) * (dh ** -0.5)).astype(BF16)
    outs = []
    for hd in range(X_HEADS):
        sl = slice(hd * dh, (hd + 1) * dh)
        s = _dot_nt(q[:, sl], k_ref[:, sl])
        m = jnp.max(s, axis=-1, keepdims=True)
        p = jnp.exp(s - m)
        den = jnp.sum(p, axis=-1, keepdims=True)
        outs.append(_dot(p.astype(BF16), v_ref[:, sl]) / den)
    o = jnp.concatenate(outs, axis=-1).astype(BF16)
    out_ref[...] = x + _dot(o, wo_ref[...])


def _xattn(x, g, k, v, wq, wo, batch):
    n, d = x.shape
    seq = n // batch
    tm = min(TOKEN_TILE, seq)
    n_mem = k.shape[0] // batch
    per_b = seq // tm
    row = lambda b, i: (b * per_b + i, 0)
    full = lambda a: pl.BlockSpec(a.shape, lambda b, i: (0, 0))
    kv = pl.BlockSpec((n_mem, k.shape[1]), lambda b, i: (b, 0))
    return pl.pallas_call(
        _xattn_kernel,
        out_shape=jax.ShapeDtypeStruct((n, d), F32),
        grid=(batch, per_b),
        in_specs=[pl.BlockSpec((tm, d), row), full(g), kv, kv, full(wq), full(wo)],
        out_specs=pl.BlockSpec((tm, d), row),
        compiler_params=_params("parallel", "parallel"),
        name="xattn",
    )(x, g, k, v, wq, wo)


def kernel(x, mem, positions, ffn1_norm, ffn1_w_gu, ffn1_w_down, mix_norm, w_in,
           hgrn_lower_bounds, hgrn_head_norm, w_att_branch, w_hgrn_branch, w_mix_out,
           xattn_norm, mem_norm, xattn_wq, xattn_wkv, xattn_wo,
           ffn2_norm, ffn2_w_gu, ffn2_w_down, final_norm):
    batch, seq, d = x.shape
    depth = w_in.shape[0]
    n = batch * seq
    d_ff = ffn1_w_down.shape[1]
    att_w = len(ATT_GROUPS) * ATT_GROUP_WIDTH
    hg_w = hgrn_head_norm.shape[1]
    x_w = xattn_wq.shape[2]

    xf = x.reshape(n, d)
    memf = mem.reshape(batch * mem.shape[1], d)
    cos2, sin2 = _rope_tables(positions.reshape(n, 1).astype(F32))

    sizes = (att_w,) * 3 + (hg_w,) * 4 + (d,) * 2
    offs = [sum(sizes[:i]) for i in range(len(sizes) + 1)]

    def w_in_part(l, idx):
        return w_in[l, :, offs[idx]:offs[idx + 1]].astype(BF16)

    row2 = lambda a, l: a[l].reshape(1, -1)

    for l in range(depth):
        xf = _ffn(xf, row2(ffn1_norm, l), ffn1_w_gu[l, :, :d_ff].astype(BF16),
                  ffn1_w_gu[l, :, d_ff:].astype(BF16), ffn1_w_down[l].astype(BF16))

        g_mix = row2(mix_norm, l)
        o_groups, s_groups = [], []
        for gi, (_, dil) in enumerate(ATT_GROUPS):
            gcols = slice(gi * ATT_GROUP_WIDTH, (gi + 1) * ATT_GROUP_WIDTH)
            q_a, k_a, v_a = _proj_attn(xf, g_mix, cos2, sin2, w_in_part(l, 0)[:, gcols],
                                       w_in_part(l, 1)[:, gcols], w_in_part(l, 2)[:, gcols],
                                       dil, batch)
            o_g, s_g = _attn_group(q_a, k_a, v_a)
            o_groups.append(o_g)
            s_groups.append(s_g)
        q_b, lf_b, i_b, og_b = _proj_hgrn(xf, g_mix, hgrn_lower_bounds,
                                          w_in_part(l, 3), w_in_part(l, 4),
                                          w_in_part(l, 5), w_in_part(l, 6), l)
        o_b = _hgrn(q_b, lf_b, i_b, og_b, row2(hgrn_head_norm, l), batch)
        xf = _mix(xf, g_mix, o_groups, s_groups, o_b,
                  w_in_part(l, 7), w_in_part(l, 8),
                  w_att_branch[l].astype(BF16), w_hgrn_branch[l].astype(BF16),
                  w_mix_out[l].astype(BF16))

        k_x, v_x = _mem_kv(memf, row2(mem_norm, l), xattn_wkv[l, :, :x_w].astype(BF16),
                           xattn_wkv[l, :, x_w:].astype(BF16))
        xf = _xattn(xf, row2(xattn_norm, l), k_x, v_x,
                    xattn_wq[l].astype(BF16), xattn_wo[l].astype(BF16), batch)

        last = l == depth - 1
        xf = _ffn(xf, row2(ffn2_norm, l), ffn2_w_gu[l, :, :d_ff].astype(BF16),
                  ffn2_w_gu[l, :, d_ff:].astype(BF16), ffn2_w_down[l].astype(BF16),
                  final_g=final_norm.reshape(1, -1) if last else None)

    return xf.reshape(batch, seq, d)
```

```python
import functools
import math

import jax
import jax.numpy as jnp
from jax import lax
from jax.experimental import pallas as pl
from jax.experimental.pallas import tpu as pltpu

F32 = jnp.float32
BF16 = jnp.bfloat16

ATT_GROUPS = ((128, 1), (512, 4), (2048, 16))
ATT_HEADS = 4
ATT_HEAD_DIM = 128
ATT_GROUP_WIDTH = ATT_HEADS * ATT_HEAD_DIM
HGRN_HEAD_DIM = 128
X_HEADS = 4
ROPE_THETA = 10000.0
EPS = 1e-6

LANES = 128
SUBLANES = 8
MXU_TILE = 256
VMEM_LIMIT_BYTES = 56 * 1024 * 1024

TOKEN_TILE = 512
FFN_CHUNK_TILES = 3
ATT_BLOCK = 128
HGRN_CHUNK = 128
HGRN_SAFE_EXPONENT = 80.0
NEG_BIG = -1e30


def _params(*sem):
    return pltpu.CompilerParams(dimension_semantics=sem,
                                vmem_limit_bytes=VMEM_LIMIT_BYTES)


def _rms(x, g):
    ms = jnp.mean(x * x, axis=-1, keepdims=True)
    return x * lax.rsqrt(ms + EPS) * g


def _dot(a, b):
    return jnp.dot(a, b, preferred_element_type=F32)


def _dot_nt(a, b):
    return lax.dot_general(a, b, (((1,), (1,)), ((), ())), preferred_element_type=F32)


def _dot_tn(a, b):
    return lax.dot_general(a, b, (((0,), (0,)), ((), ())), preferred_element_type=F32)


def _rope_table_kernel(pos_ref, cos_ref, sin_ref):
    half = ATT_HEAD_DIM // 2
    lane = lax.broadcasted_iota(jnp.int32, (1, ATT_HEAD_DIM), 1)
    idx = jnp.where(lane >= half, lane - half, lane).astype(F32)
    inv_freq = jnp.exp(idx * (-2.0 * math.log(ROPE_THETA) / ATT_HEAD_DIM))
    ang = pos_ref[...] * inv_freq
    sign = jnp.where(lane >= half, 1.0, -1.0)
    cos_ref[...] = jnp.cos(ang)
    sin_ref[...] = jnp.sin(ang) * sign


def _rope_tables(pos_col):
    n = pos_col.shape[0]
    tm = min(n, 2048)
    out = jax.ShapeDtypeStruct((n, ATT_HEAD_DIM), F32)
    return pl.pallas_call(
        _rope_table_kernel,
        out_shape=(out, out),
        grid=(n // tm,),
        in_specs=[pl.BlockSpec((tm, 1), lambda i: (i, 0))],
        out_specs=(pl.BlockSpec((tm, ATT_HEAD_DIM), lambda i: (i, 0)),) * 2,
        compiler_params=_params("parallel"),
        name="rope_tables",
    )(pos_col)


def _ffn_chunks(d_ff):
    assert d_ff % MXU_TILE == 0
    tiles = d_ff // MXU_TILE
    n_chunks = -(-tiles // FFN_CHUNK_TILES)
    bounds = [MXU_TILE * (tiles * k // n_chunks) for k in range(n_chunks + 1)]
    return list(zip(bounds[:-1], bounds[1:]))


def _ffn_kernel(*refs, final):
    if final:
        x_ref, g_ref, wg_ref, wu_ref, wd_ref, fg_ref, o_ref = refs
    else:
        x_ref, g_ref, wg_ref, wu_ref, wd_ref, o_ref = refs
    x = x_ref[...]
    h = _rms(x, g_ref[...]).astype(BF16)
    acc = None
    for lo, hi in _ffn_chunks(wg_ref.shape[1]):
        gate = _dot(h, wg_ref[:, lo:hi])
        up = _dot(h, wu_ref[:, lo:hi])
        act = (gate * jax.nn.sigmoid(gate) * up).astype(BF16)
        part = _dot(act, wd_ref[lo:hi, :])
        acc = part if acc is None else acc + part
    y = x + 0.5 * acc
    if final:
        y = _rms(y, fg_ref[...])
    o_ref[...] = y


def _resident(a):
    return pl.BlockSpec(a.shape, lambda *_: (0,) * a.ndim, pipeline_mode=pl.Buffered(1))


def _ffn(x, g, w_gate, w_up, w_down, final_g=None):
    n, d = x.shape
    tm = min(TOKEN_TILE, n)
    final = final_g is not None
    row = pl.BlockSpec((tm, d), lambda i: (i, 0))
    in_specs = [row, _resident(g), _resident(w_gate), _resident(w_up), _resident(w_down)]
    args = [x, g, w_gate, w_up, w_down]
    if final:
        in_specs.append(_resident(final_g))
        args.append(final_g)
    return pl.pallas_call(
        functools.partial(_ffn_kernel, final=final),
        out_shape=jax.ShapeDtypeStruct((n, d), F32),
        grid=(n // tm,),
        in_specs=in_specs,
        out_specs=row,
        compiler_params=_params("parallel"),
        name="ffn_final" if final else "ffn",
    )(*args)


def _store_by_class(out_ref, scr_ref, val, dil):
    if dil == 1:
        out_ref[0] = val.astype(out_ref.dtype)
        return
    rows = val.shape[0] // dil
    for c in range(val.shape[1] // LANES):
        cols = slice(c * LANES, (c + 1) * LANES)
        scr_ref[c] = val[:, cols]
        for r in range(dil):
            out_ref[r, :, cols] = scr_ref[c, pl.ds(r, rows, stride=dil), :].astype(out_ref.dtype)


def _proj_attn_kernel(x_ref, g_ref, cos_ref, sin_ref, wq_ref, wk_ref, wv_ref,
                      q_ref, k_ref, v_ref, *scr, dil):
    h = _rms(x_ref[...], g_ref[...]).astype(BF16)
    cos = cos_ref[...]
    sin = sin_ref[...]
    scale = ATT_HEAD_DIM ** -0.5
    zq = _dot(h, wq_ref[...])
    zk = _dot(h, wk_ref[...])
    rq, rk = [], []
    for hd in range(ATT_HEADS):
        sl = slice(hd * ATT_HEAD_DIM, (hd + 1) * ATT_HEAD_DIM)
        xq = zq[:, sl]
        xk = zk[:, sl]
        rq.append((xq * cos + pltpu.roll(xq, ATT_HEAD_DIM // 2, 1) * sin) * scale)
        rk.append(xk * cos + pltpu.roll(xk, ATT_HEAD_DIM // 2, 1) * sin)
    scr = scr if scr else (None,) * 3
    _store_by_class(q_ref, scr[0], jnp.concatenate(rq, axis=-1), dil)
    _store_by_class(k_ref, scr[1], jnp.concatenate(rk, axis=-1), dil)
    _store_by_class(v_ref, scr[2], _dot(h, wv_ref[...]), dil)


def _proj_attn(x, g, cos2, sin2, wq, wk, wv, dil, batch):
    n, d = x.shape
    seq = n // batch
    tm = min(TOKEN_TILE, seq)
    per_b = seq // tm
    gw = wq.shape[1]
    out = jax.ShapeDtypeStruct((batch, dil, seq // dil, gw), BF16)
    row = lambda b, i: (b * per_b + i, 0)
    full = _resident
    scratch = [] if dil == 1 else [pltpu.VMEM((gw // LANES, tm, LANES), F32)] * 3
    return pl.pallas_call(
        functools.partial(_proj_attn_kernel, dil=dil),
        out_shape=(out, out, out),
        grid=(batch, per_b),
        in_specs=[
            pl.BlockSpec((tm, d), row),
            full(g),
            pl.BlockSpec((tm, ATT_HEAD_DIM), row),
            pl.BlockSpec((tm, ATT_HEAD_DIM), row),
            full(wq), full(wk), full(wv),
        ],
        out_specs=(pl.BlockSpec((None, dil, tm // dil, gw), lambda b, i: (b, 0, i, 0)),) * 3,
        scratch_shapes=scratch,
        compiler_params=_params("parallel", "parallel"),
        name=f"proj_attn_dil{dil}",
    )(x, g, cos2, sin2, wq, wk, wv)


def _proj_hgrn_kernel(x_ref, g_ref, lbraw_ref, wq_ref, wf_ref, wi_ref, wo_ref,
                      q_ref, lf_ref, i_ref, og_ref, *, layer):
    h = _rms(x_ref[...], g_ref[...]).astype(BF16)
    raw = lbraw_ref[...]
    p = jnp.exp(raw - jnp.max(raw, axis=0, keepdims=True))
    p = p / jnp.sum(p, axis=0, keepdims=True)
    lb = jnp.sum(p[:layer + 1], axis=0, keepdims=True) - p[0:1]

    q_ref[...] = (_dot(h, wq_ref[...]) * (HGRN_HEAD_DIM ** -0.5)).astype(BF16)
    f = lb + (1.0 - lb) * jax.nn.sigmoid(_dot(h, wf_ref[...]))
    lf_ref[...] = jnp.log(f)
    i_ref[...] = _dot(h, wi_ref[...]).astype(BF16)
    og = _dot(h, wo_ref[...])
    og_ref[...] = (og * jax.nn.sigmoid(og)).astype(BF16)


def _proj_hgrn(x, g, lb_raw, wq, wf, wi, wo, layer):
    n, d = x.shape
    tm = min(TOKEN_TILE, n)
    width = wq.shape[1]
    row = lambda i: (i, 0)
    bf = jax.ShapeDtypeStruct((n, width), BF16)
    return pl.pallas_call(
        functools.partial(_proj_hgrn_kernel, layer=layer),
        out_shape=(bf, jax.ShapeDtypeStruct((n, width), F32), bf, bf),
        grid=(n // tm,),
        in_specs=[pl.BlockSpec((tm, d), row)]
        + [_resident(a) for a in (g, lb_raw, wq, wf, wi, wo)],
        out_specs=(pl.BlockSpec((tm, width), row),) * 4,
        compiler_params=_params("parallel"),
        name="proj_hgrn",
    )(x, g, lb_raw, wq, wf, wi, wo)


def _attn_kernel(q_ref, kp_ref, kc_ref, vp_ref, vc_ref, o_ref, st_ref, *, nblk):
    first = pl.program_id(2) == 0
    row = lax.broadcasted_iota(jnp.int32, (ATT_BLOCK, 2 * ATT_BLOCK), 0)
    col = lax.broadcasted_iota(jnp.int32, (ATT_BLOCK, 2 * ATT_BLOCK), 1)
    band = jnp.logical_and(col >= row, col <= row + ATT_BLOCK)
    low = jnp.where(first, ATT_BLOCK, 0)
    band_first = jnp.logical_and(col >= jnp.maximum(row, low), col <= row + ATT_BLOCK)
    ones = jnp.ones((2 * ATT_BLOCK, ATT_HEAD_DIM), BF16)
    lane = lax.broadcasted_iota(jnp.int32, (ATT_BLOCK, LANES), 1)
    for blk in range(nblk):
        rows = slice(blk * ATT_BLOCK, (blk + 1) * ATT_BLOCK)
        stats = jnp.zeros((ATT_BLOCK, LANES), F32)
        for hd in range(ATT_HEADS):
            sl = slice(hd * ATT_HEAD_DIM, (hd + 1) * ATT_HEAD_DIM)
            if blk == 0:
                k2 = jnp.concatenate([kp_ref[:, sl], kc_ref[0:ATT_BLOCK, sl]], axis=0)
                v2 = jnp.concatenate([vp_ref[:, sl], vc_ref[0:ATT_BLOCK, sl]], axis=0)
                mask = band_first
            else:
                win = slice((blk - 1) * ATT_BLOCK, (blk + 1) * ATT_BLOCK)
                k2 = kc_ref[win, sl]
                v2 = vc_ref[win, sl]
                mask = band
            s = jnp.where(mask, _dot_nt(q_ref[rows, sl], k2), NEG_BIG)
            m = jnp.max(s, axis=-1, keepdims=True)
            p = jnp.exp(s - m).astype(BF16)
            o_den = _dot(p, jnp.concatenate([v2, ones], axis=1))
            den = o_den[:, ATT_HEAD_DIM:]
            o_ref[rows, sl] = (o_den[:, :ATT_HEAD_DIM] / den).astype(BF16)
            stats = jnp.where(lane == hd, m + jnp.log(den), stats)
        st_ref[rows, :] = stats


def _attn_group(q, k, v):
    batch, dil, cls, gw = q.shape
    assert cls % ATT_BLOCK == 0
    nblk = min(4, cls // ATT_BLOCK)
    qb = nblk * ATT_BLOCK
    cur = lambda b, r, i: (b, r, i, 0)
    prev = lambda b, r, i: (b, r, jnp.maximum(i * nblk - 1, 0), 0)
    cur_spec = pl.BlockSpec((None, None, qb, gw), cur)
    prev_spec = pl.BlockSpec((None, None, ATT_BLOCK, gw), prev)
    return pl.pallas_call(
        functools.partial(_attn_kernel, nblk=nblk),
        out_shape=(jax.ShapeDtypeStruct((batch, dil, cls, gw), BF16),
                   jax.ShapeDtypeStruct((batch, dil, cls, LANES), F32)),
        grid=(batch, dil, cls // qb),
        in_specs=[cur_spec, prev_spec, cur_spec, prev_spec, cur_spec],
        out_specs=(pl.BlockSpec((None, None, qb, gw), cur),
                   pl.BlockSpec((None, None, qb, LANES), cur)),
        compiler_params=_params("parallel", "parallel", "arbitrary"),
        name=f"attn_dil{dil}",
    )(q, k, k, v, v)


def _hgrn_scores_midref(a_ref, q, kk, b, heads):
    c = b.shape[0]
    b_mid = b[c // 2 - 1:c // 2, :]
    q_s = (q * jnp.exp(b - b_mid)).astype(BF16)
    k_s = (kk * jnp.exp(b_mid - b)).astype(BF16)
    r_i = lax.broadcasted_iota(jnp.int32, (c, c), 0)
    c_i = lax.broadcasted_iota(jnp.int32, (c, c), 1)
    causal = c_i <= r_i
    for hd in range(heads):
        sl = slice(hd * HGRN_HEAD_DIM, (hd + 1) * HGRN_HEAD_DIM)
        a_ref[hd] = jnp.where(causal, _dot_nt(q_s[:, sl], k_s[:, sl]), 0.0).astype(BF16)


def _hgrn_scores_bounded(a_ref, q, kk, b, heads):
    c, w = b.shape
    r_i = lax.broadcasted_iota(jnp.int32, (c, c), 0)
    c_i = lax.broadcasted_iota(jnp.int32, (c, c), 1)
    rows = lax.broadcasted_iota(jnp.int32, (c, w), 0)

    levels = []
    half = SUBLANES
    while half < c:
        span = 2 * half
        pieces = [jnp.broadcast_to(b[s + half - 1:s + half, :], (span, w))
                  for s in range(0, c, span)]
        bref = pieces[0] if len(pieces) == 1 else jnp.concatenate(pieces, axis=0)
        upper = (rows & (span - 1)) >= half
        e = jnp.exp(jnp.where(upper, b - bref, bref - b))
        q_up = jnp.where(upper, q * e, 0.0).astype(BF16)
        k_lo = jnp.where(upper, 0.0, kk * e).astype(BF16)
        same = (r_i & ~(span - 1)) == (c_i & ~(span - 1))
        levels.append((q_up, k_lo, same))
        half = span

    nb = c // SUBLANES
    sub = lax.broadcasted_iota(jnp.int32, (nb, SUBLANES, w), 1)
    q3 = q.reshape(nb, SUBLANES, w)
    k3 = kk.reshape(nb, SUBLANES, w)
    b3 = b.reshape(nb, SUBLANES, w)
    near = []
    for dist in range(SUBLANES):
        if dist == 0:
            prod = q3 * k3
        else:
            k_r = pltpu.roll(k3, dist, 1)
            b_r = pltpu.roll(b3, dist, 1)
            prod = jnp.where(sub >= dist, q3 * k_r * jnp.exp(b3 - b_r), 0.0)
        near.append(prod.reshape(c, w))

    for hd in range(heads):
        sl = slice(hd * HGRN_HEAD_DIM, (hd + 1) * HGRN_HEAD_DIM)
        a = jnp.zeros((c, c), F32)
        for q_up, k_lo, same in levels:
            a = a + jnp.where(same, _dot_nt(q_up[:, sl], k_lo[:, sl]), 0.0)
        for dist in range(SUBLANES):
            col_val = jnp.sum(near[dist][:, sl], axis=-1, keepdims=True)
            a = a + jnp.where(c_i == r_i - dist, col_val, 0.0)
        a_ref[hd] = a.astype(BF16)


def _hgrn_kernel(q_ref, lf_ref, i_ref, og_ref, hn_ref, o_ref, st_ref, a_ref, *, chunk, heads):
    c = chunk

    @pl.when(pl.program_id(1) == 0)
    def _():
        st_ref[...] = jnp.zeros_like(st_ref)

    q = q_ref[...].astype(F32)
    lf = lf_ref[...]
    v_bf = i_ref[...]

    r_i = lax.broadcasted_iota(jnp.int32, (c, c), 0)
    c_i = lax.broadcasted_iota(jnp.int32, (c, c), 1)
    tri = jnp.where(c_i <= r_i, 1.0, 0.0).astype(BF16)
    lf_hi = lf.astype(BF16)
    rem = lf - lf_hi.astype(F32)
    lf_mid = rem.astype(BF16)
    lf_lo = (rem - lf_mid.astype(F32)).astype(BF16)
    b = _dot(tri, lf_hi) + _dot(tri, lf_mid) + _dot(tri, lf_lo)

    kk = 1.0 - jnp.exp(lf)
    b_last = b[c - 1:c, :]
    b_mid = b[c // 2 - 1:c // 2, :]
    q_in = (q * jnp.exp(b)).astype(BF16)
    k_out = (kk * jnp.exp(b_last - b)).astype(BF16)
    st_decay = jnp.exp(b_last)

    reach = jnp.max(jnp.maximum(b[0:1, :] - b_mid, b_mid - b_last))
    midref_ok = reach < HGRN_SAFE_EXPONENT

    @pl.when(midref_ok)
    def _():
        _hgrn_scores_midref(a_ref, q, kk, b, heads)

    @pl.when(jnp.logical_not(midref_ok))
    def _():
        _hgrn_scores_bounded(a_ref, q, kk, b, heads)

    gain = hn_ref[...]
    for hd in range(heads):
        sl = slice(hd * HGRN_HEAD_DIM, (hd + 1) * HGRN_HEAD_DIM)
        v_h = v_bf[:, sl]
        st_h = st_ref[hd]
        o_h = _dot(a_ref[hd], v_h) + _dot_nt(q_in[:, sl], st_h.astype(BF16))
        st_ref[hd] = st_h * st_decay[:, sl] + _dot_tn(v_h, k_out[:, sl])
        o_n = _rms(o_h, gain[:, sl])
        o_ref[:, sl] = (o_n * og_ref[:, sl].astype(F32)).astype(BF16)


def _hgrn(q, lf, i, og, head_gain, batch):
    n, w = q.shape
    seq = n // batch
    c = min(HGRN_CHUNK, seq)
    heads = w // HGRN_HEAD_DIM
    nc = seq // c
    blk = lambda b, t: (b * nc + t, 0)
    spec = pl.BlockSpec((c, w), blk)
    return pl.pallas_call(
        functools.partial(_hgrn_kernel, chunk=c, heads=heads),
        out_shape=jax.ShapeDtypeStruct((n, w), BF16),
        grid=(batch, nc),
        in_specs=[spec, spec, spec, spec, pl.BlockSpec((1, w), lambda b, t: (0, 0))],
        out_specs=spec,
        scratch_shapes=[pltpu.VMEM((heads, HGRN_HEAD_DIM, HGRN_HEAD_DIM), F32),
                        pltpu.VMEM((heads, c, c), BF16)],
        compiler_params=_params("parallel", "arbitrary"),
        name="hgrn",
    )(q, lf, i, og, head_gain)


def _load_by_token(src_ref, scr_ref):
    dil, rows, width = src_ref.shape
    if dil == 1:
        return src_ref[0].astype(F32)
    parts = []
    for c in range(width // LANES):
        cols = slice(c * LANES, (c + 1) * LANES)
        for r in range(dil):
            scr_ref[c, pl.ds(r, rows, stride=dil), :] = src_ref[r, :, cols].astype(F32)
        parts.append(scr_ref[c])
    return parts[0] if len(parts) == 1 else jnp.concatenate(parts, axis=-1)


def _mix_kernel(x_ref, g_ref, o0_ref, o1_ref, o2_ref, s0_ref, s1_ref, s2_ref, ob_ref,
                wga_ref, wgb_ref, wa_ref, wb_ref, wo_ref, out_ref, *scr):
    x = x_ref[...]
    h = _rms(x, g_ref[...]).astype(BF16)
    gate_a = jax.nn.sigmoid(_dot(h, wga_ref[...]))
    gate_b = jax.nn.sigmoid(_dot(h, wgb_ref[...]))

    scr = list(scr)
    take = lambda ref: _load_by_token(ref, scr.pop(0) if ref.shape[0] > 1 else None)
    o0, o1, o2 = take(o0_ref), take(o1_ref), take(o2_ref)
    s0, s1, s2 = take(s0_ref), take(s1_ref), take(s2_ref)
    mx = jnp.maximum(jnp.maximum(s0, s1), s2)
    e0, e1, e2 = jnp.exp(s0 - mx), jnp.exp(s1 - mx), jnp.exp(s2 - mx)
    inv = 1.0 / (e0 + e1 + e2)
    w0, w1, w2 = e0 * inv, e1 * inv, e2 * inv
    parts = []
    for hd in range(ATT_HEADS):
        sl = slice(hd * ATT_HEAD_DIM, (hd + 1) * ATT_HEAD_DIM)
        parts.append(o0[:, sl] * w0[:, hd:hd + 1] + o1[:, sl] * w1[:, hd:hd + 1]
                     + o2[:, sl] * w2[:, hd:hd + 1])
    att = jnp.concatenate(parts, axis=-1).astype(BF16)

    y_a = _dot(att, wa_ref[...])
    y_b = _dot(ob_ref[...], wb_ref[...])
    merged = (gate_a * y_a + gate_b * y_b).astype(BF16)
    out_ref[...] = x + _dot(merged, wo_ref[...])


def _mix(x, g, o_groups, s_groups, ob, wga, wgb, wa, wb, wo):
    n, d = x.shape
    batch = o_groups[0].shape[0]
    seq = n // batch
    tm = min(TOKEN_TILE, seq)
    per_b = seq // tm
    row = lambda b, i: (b * per_b + i, 0)
    full = _resident
    rows = lambda a: pl.BlockSpec((tm, a.shape[1]), row)
    by_class = lambda a: pl.BlockSpec((None, a.shape[1], tm // a.shape[1], a.shape[3]),
                                      lambda b, i: (b, 0, i, 0))
    grouped = [*o_groups, *s_groups]
    args = [x, g, *grouped, ob, wga, wgb, wa, wb, wo]
    in_specs = ([rows(x), full(g)] + [by_class(a) for a in grouped]
                + [rows(ob)] + [full(a) for a in (wga, wgb, wa, wb, wo)])
    scratch = [pltpu.VMEM((a.shape[3] // LANES, tm, LANES), F32)
               for a in grouped if a.shape[1] > 1]
    return pl.pallas_call(
        _mix_kernel,
        out_shape=jax.ShapeDtypeStruct((n, d), F32),
        grid=(batch, per_b),
        in_specs=in_specs,
        out_specs=pl.BlockSpec((tm, d), row),
        scratch_shapes=scratch,
        compiler_params=_params("parallel", "parallel"),
        name="mix_out",
    )(*args)


def _mem_kv_kernel(m_ref, g_ref, wk_ref, wv_ref, k_ref, v_ref):
    h = _rms(m_ref[...], g_ref[...]).astype(BF16)
    k_ref[...] = _dot(h, wk_ref[...]).astype(BF16)
    v_ref[...] = _dot(h, wv_ref[...]).astype(BF16)


def _mem_kv(mem, g, wk, wv):
    n, d = mem.shape
    tm = min(TOKEN_TILE, n)
    row = lambda i: (i, 0)
    full = _resident
    out = jax.ShapeDtypeStruct((n, wk.shape[1]), BF16)
    return pl.pallas_call(
        _mem_kv_kernel,
        out_shape=(out, out),
        grid=(n // tm,),
        in_specs=[pl.BlockSpec((tm, d), row), full(g), full(wk), full(wv)],
        out_specs=(pl.BlockSpec((tm, wk.shape[1]), row),) * 2,
        compiler_params=_params("parallel"),
        name="mem_kv",
    )(mem, g, wk, wv)


def _xattn_kernel(x_ref, g_ref, k_ref, v_ref, wq_ref, wo_ref, out_ref):
    x = x_ref[...]
    h = _rms(x, g_ref[...]).astype(BF16)
    dh = wq_ref.shape[1] // X_HEADS
    q = (_dot(h, wq_ref[...]) * (dh ** -0.5)).astype(BF16)
    outs = []
    for hd in range(X_HEADS):
        sl = slice(hd * dh, (hd + 1) * dh)
        s = _dot_nt(q[:, sl], k_ref[:, sl])
        m = jnp.max(s, axis=-1, keepdims=True)
        p = jnp.exp(s - m)
        den = jnp.sum(p, axis=-1, keepdims=True)
        outs.append(_dot(p.astype(BF16), v_ref[:, sl]) / den)
    o = jnp.concatenate(outs, axis=-1).astype(BF16)
    out_ref[...] = x + _dot(o, wo_ref[...])


def _xattn(x, g, k, v, wq, wo, batch):
    n, d = x.shape
    seq = n // batch
    tm = min(TOKEN_TILE, seq)
    n_mem = k.shape[0] // batch
    per_b = seq // tm
    row = lambda b, i: (b * per_b + i, 0)
    full = _resident
    kv = pl.BlockSpec((n_mem, k.shape[1]), lambda b, i: (b, 0))
    return pl.pallas_call(
        _xattn_kernel,
        out_shape=jax.ShapeDtypeStruct((n, d), F32),
        grid=(batch, per_b),
        in_specs=[pl.BlockSpec((tm, d), row), full(g), kv, kv, full(wq), full(wo)],
        out_specs=pl.BlockSpec((tm, d), row),
        compiler_params=_params("parallel", "parallel"),
        name="xattn",
    )(x, g, k, v, wq, wo)


def kernel(x, mem, positions, ffn1_norm, ffn1_w_gu, ffn1_w_down, mix_norm, w_in,
           hgrn_lower_bounds, hgrn_head_norm, w_att_branch, w_hgrn_branch, w_mix_out,
           xattn_norm, mem_norm, xattn_wq, xattn_wkv, xattn_wo,
           ffn2_norm, ffn2_w_gu, ffn2_w_down, final_norm):
    batch, seq, d = x.shape
    depth = w_in.shape[0]
    n = batch * seq
    d_ff = ffn1_w_down.shape[1]
    att_w = len(ATT_GROUPS) * ATT_GROUP_WIDTH
    hg_w = hgrn_head_norm.shape[1]
    x_w = xattn_wq.shape[2]

    xf = x.reshape(n, d)
    memf = mem.reshape(batch * mem.shape[1], d)
    cos2, sin2 = _rope_tables(positions.reshape(n, 1).astype(F32))

    sizes = (att_w,) * 3 + (hg_w,) * 4 + (d,) * 2
    offs = [sum(sizes[:i]) for i in range(len(sizes) + 1)]

    def w_in_part(l, idx):
        return w_in[l, :, offs[idx]:offs[idx + 1]].astype(BF16)

    row2 = lambda a, l: a[l].reshape(1, -1)

    for l in range(depth):
        xf = _ffn(xf, row2(ffn1_norm, l), ffn1_w_gu[l, :, :d_ff].astype(BF16),
                  ffn1_w_gu[l, :, d_ff:].astype(BF16), ffn1_w_down[l].astype(BF16))

        g_mix = row2(mix_norm, l)
        o_groups, s_groups = [], []
        for gi, (_, dil) in enumerate(ATT_GROUPS):
            gcols = slice(gi * ATT_GROUP_WIDTH, (gi + 1) * ATT_GROUP_WIDTH)
            q_a, k_a, v_a = _proj_attn(xf, g_mix, cos2, sin2, w_in_part(l, 0)[:, gcols],
                                       w_in_part(l, 1)[:, gcols], w_in_part(l, 2)[:, gcols],
                                       dil, batch)
            o_g, s_g = _attn_group(q_a, k_a, v_a)
            o_groups.append(o_g)
            s_groups.append(s_g)
        q_b, lf_b, i_b, og_b = _proj_hgrn(xf, g_mix, hgrn_lower_bounds,
                                          w_in_part(l, 3), w_in_part(l, 4),
                                          w_in_part(l, 5), w_in_part(l, 6), l)
        o_b = _hgrn(q_b, lf_b, i_b, og_b, row2(hgrn_head_norm, l), batch)
        xf = _mix(xf, g_mix, o_groups, s_groups, o_b,
                  w_in_part(l, 7), w_in_part(l, 8),
                  w_att_branch[l].astype(BF16), w_hgrn_branch[l].astype(BF16),
                  w_mix_out[l].astype(BF16))

        k_x, v_x = _mem_kv(memf, row2(mem_norm, l), xattn_wkv[l, :, :x_w].astype(BF16),
                           xattn_wkv[l, :, x_w:].astype(BF16))
        xf = _xattn(xf, row2(xattn_norm, l), k_x, v_x,
                    xattn_wq[l].astype(BF16), xattn_wo[l].astype(BF16), batch)

        last = l == depth - 1
        xf = _ffn(xf, row2(ffn2_norm, l), ffn2_w_gu[l, :, :d_ff].astype(BF16),
                  ffn2_w_gu[l, :, d_ff:].astype(BF16), ffn2_w_down[l].astype(BF16),
                  final_g=final_norm.reshape(1, -1) if last else None)

    return xf.reshape(batch, seq, d)
```

```python
import functools
import math

import jax
import jax.numpy as jnp
from jax import lax
from jax.experimental import pallas as pl
from jax.experimental.pallas import tpu as pltpu

F32 = jnp.float32
BF16 = jnp.bfloat16

ATT_GROUPS = ((128, 1), (512, 4), (2048, 16))
ATT_HEADS = 4
ATT_HEAD_DIM = 128
ATT_GROUP_WIDTH = ATT_HEADS * ATT_HEAD_DIM
HGRN_HEAD_DIM = 128
X_HEADS = 4
ROPE_THETA = 10000.0
EPS = 1e-6

LANES = 128
SUBLANES = 8
MXU_TILE = 256
VMEM_LIMIT_BYTES = 56 * 1024 * 1024

TOKEN_TILE = 512
FFN_CHUNK_TILES = 3
ATT_BLOCK = 128
ATT_CHAINS = 4
HGRN_CHUNK = 128
HGRN_SAFE_EXPONENT = 80.0
NEG_BIG = -1e30


def _params(*sem):
    return pltpu.CompilerParams(dimension_semantics=sem,
                                vmem_limit_bytes=VMEM_LIMIT_BYTES)


def _rms(x, g):
    ms = jnp.mean(x * x, axis=-1, keepdims=True)
    return x * lax.rsqrt(ms + EPS) * g


def _dot(a, b):
    return jnp.dot(a, b, preferred_element_type=F32)


def _dot_nt(a, b):
    return lax.dot_general(a, b, (((1,), (1,)), ((), ())), preferred_element_type=F32)


def _dot_tn(a, b):
    return lax.dot_general(a, b, (((0,), (0,)), ((), ())), preferred_element_type=F32)


def _resident(a):
    return pl.BlockSpec(a.shape, lambda *_: (0,) * a.ndim, pipeline_mode=pl.Buffered(1))


def _layer_block(a, layer, cols=None):
    _, rows, width = a.shape
    if cols is None:
        return pl.BlockSpec((None, rows, width), lambda *_: (layer, 0, 0),
                            pipeline_mode=pl.Buffered(1))
    off, size = cols
    return pl.BlockSpec((None, pl.Element(rows), pl.Element(size)), lambda *_: (layer, 0, off),
                        pipeline_mode=pl.Buffered(1))


def _rope_table_kernel(pos_ref, cos_ref, sin_ref):
    half = ATT_HEAD_DIM // 2
    lane = lax.broadcasted_iota(jnp.int32, (1, ATT_HEAD_DIM), 1)
    idx = jnp.where(lane >= half, lane - half, lane).astype(F32)
    inv_freq = jnp.exp(idx * (-2.0 * math.log(ROPE_THETA) / ATT_HEAD_DIM))
    ang = pos_ref[...] * inv_freq
    sign = jnp.where(lane >= half, 1.0, -1.0)
    cos_ref[...] = jnp.cos(ang)
    sin_ref[...] = jnp.sin(ang) * sign


def _rope_tables(pos_col):
    n = pos_col.shape[0]
    tm = min(n, 2048)
    out = jax.ShapeDtypeStruct((n, ATT_HEAD_DIM), F32)
    return pl.pallas_call(
        _rope_table_kernel,
        out_shape=(out, out),
        grid=(n // tm,),
        in_specs=[pl.BlockSpec((tm, 1), lambda i: (i, 0))],
        out_specs=(pl.BlockSpec((tm, ATT_HEAD_DIM), lambda i: (i, 0)),) * 2,
        compiler_params=_params("parallel"),
        name="rope_tables",
    )(pos_col)


def _ffn_chunks(d_ff):
    assert d_ff % MXU_TILE == 0
    tiles = d_ff // MXU_TILE
    n_chunks = -(-tiles // FFN_CHUNK_TILES)
    bounds = [MXU_TILE * (tiles * k // n_chunks) for k in range(n_chunks + 1)]
    return list(zip(bounds[:-1], bounds[1:]))


def _ffn_kernel(*refs, final):
    if final:
        x_ref, g_ref, wg_ref, wu_ref, wd_ref, fg_ref, o_ref = refs
    else:
        x_ref, g_ref, wg_ref, wu_ref, wd_ref, o_ref = refs
    x = x_ref[...]
    h = _rms(x, g_ref[...]).astype(BF16)
    acc = None
    for lo, hi in _ffn_chunks(wg_ref.shape[1]):
        gate = _dot(h, wg_ref[:, lo:hi])
        up = _dot(h, wu_ref[:, lo:hi])
        act = (gate * jax.nn.sigmoid(gate) * up).astype(BF16)
        part = _dot(act, wd_ref[lo:hi, :])
        acc = part if acc is None else acc + part
    y = x + 0.5 * acc
    if final:
        y = _rms(y, fg_ref[...])
    o_ref[...] = y


def _ffn(x, norm, w_gu, w_down, layer, final_g=None):
    n, d = x.shape
    d_ff = w_down.shape[1]
    tm = min(TOKEN_TILE, n)
    final = final_g is not None
    row = pl.BlockSpec((tm, d), lambda i: (i, 0))
    in_specs = [row, _layer_block(norm, layer), _layer_block(w_gu, layer, (0, d_ff)),
                _layer_block(w_gu, layer, (d_ff, d_ff)), _layer_block(w_down, layer)]
    args = [x, norm, w_gu, w_gu, w_down]
    if final:
        in_specs.append(_resident(final_g))
        args.append(final_g)
    return pl.pallas_call(
        functools.partial(_ffn_kernel, final=final),
        out_shape=jax.ShapeDtypeStruct((n, d), F32),
        grid=(n // tm,),
        in_specs=in_specs,
        out_specs=row,
        compiler_params=_params("parallel"),
        name="ffn_final" if final else "ffn",
    )(*args)


def _store_by_class(out_ref, scr_ref, val, dil):
    if dil == 1:
        out_ref[0] = val.astype(out_ref.dtype)
        return
    rows = val.shape[0] // dil
    for c in range(val.shape[1] // LANES):
        cols = slice(c * LANES, (c + 1) * LANES)
        scr_ref[c] = val[:, cols]
        for r in range(dil):
            out_ref[r, :, cols] = scr_ref[c, pl.ds(r, rows, stride=dil), :].astype(out_ref.dtype)


def _proj_attn_kernel(x_ref, g_ref, cos_ref, sin_ref, w_ref, *rest, dils):
    n_g = len(dils)
    outs, scr = rest[:3 * n_g], rest[3 * n_g:]
    gw = ATT_GROUP_WIDTH
    h = _rms(x_ref[...], g_ref[...]).astype(BF16)
    cos = cos_ref[...]
    sin = sin_ref[...]
    scale = ATT_HEAD_DIM ** -0.5
    for gi, dil in enumerate(dils):
        part = lambda p: w_ref[:, (p * n_g + gi) * gw:(p * n_g + gi + 1) * gw]
        zq = _dot(h, part(0))
        zk = _dot(h, part(1))
        rq, rk = [], []
        for hd in range(ATT_HEADS):
            sl = slice(hd * ATT_HEAD_DIM, (hd + 1) * ATT_HEAD_DIM)
            xq = zq[:, sl]
            xk = zk[:, sl]
            rq.append((xq * cos + pltpu.roll(xq, ATT_HEAD_DIM // 2, 1) * sin) * scale)
            rk.append(xk * cos + pltpu.roll(xk, ATT_HEAD_DIM // 2, 1) * sin)
        q_ref, k_ref, v_ref = outs[3 * gi:3 * gi + 3]
        _store_by_class(q_ref, scr[0] if scr else None, jnp.concatenate(rq, axis=-1), dil)
        _store_by_class(k_ref, scr[1] if scr else None, jnp.concatenate(rk, axis=-1), dil)
        _store_by_class(v_ref, scr[2] if scr else None, _dot(h, part(2)), dil)


def _proj_attn(x, norm, cos2, sin2, w_in, layer, batch):
    n, d = x.shape
    seq = n // batch
    tm = min(TOKEN_TILE, seq)
    per_b = seq // tm
    gw = ATT_GROUP_WIDTH
    dils = tuple(dil for _, dil in ATT_GROUPS)
    row = lambda b, i: (b * per_b + i, 0)
    out_shape, out_specs = [], []
    for dil in dils:
        out_shape += [jax.ShapeDtypeStruct((batch, dil, seq // dil, gw), BF16)] * 3
        out_specs += [pl.BlockSpec((None, dil, tm // dil, gw), lambda b, i: (b, 0, i, 0))] * 3
    scratch = [pltpu.VMEM((gw // LANES, tm, LANES), F32)] * 3 if max(dils) > 1 else []
    outs = pl.pallas_call(
        functools.partial(_proj_attn_kernel, dils=dils),
        out_shape=tuple(out_shape),
        grid=(batch, per_b),
        in_specs=[
            pl.BlockSpec((tm, d), row),
            _layer_block(norm, layer),
            pl.BlockSpec((tm, ATT_HEAD_DIM), row),
            pl.BlockSpec((tm, ATT_HEAD_DIM), row),
            _layer_block(w_in, layer, (0, 3 * len(dils) * gw)),
        ],
        out_specs=tuple(out_specs),
        scratch_shapes=scratch,
        compiler_params=_params("parallel", "parallel"),
        name="proj_attn",
    )(x, norm, cos2, sin2, w_in)
    return [outs[3 * gi:3 * gi + 3] for gi in range(len(dils))]


def _proj_hgrn_kernel(x_ref, g_ref, lbraw_ref, w_ref, q_ref, lf_ref, i_ref, og_ref, *, layer):
    h = _rms(x_ref[...], g_ref[...]).astype(BF16)
    raw = lbraw_ref[...]
    p = jnp.exp(raw - jnp.max(raw, axis=0, keepdims=True))
    p = p / jnp.sum(p, axis=0, keepdims=True)
    lb = jnp.sum(p[:layer + 1], axis=0, keepdims=True) - p[0:1]

    hw = q_ref.shape[1]
    sec = lambda k: w_ref[:, k * hw:(k + 1) * hw]
    q_ref[...] = (_dot(h, sec(0)) * (HGRN_HEAD_DIM ** -0.5)).astype(BF16)
    f = lb + (1.0 - lb) * jax.nn.sigmoid(_dot(h, sec(1)))
    lf_ref[...] = jnp.log(f)
    i_ref[...] = _dot(h, sec(2)).astype(BF16)
    og = _dot(h, sec(3))
    og_ref[...] = (og * jax.nn.sigmoid(og)).astype(BF16)


def _proj_hgrn(x, norm, lb_raw, w_in, col_off, layer):
    n, d = x.shape
    tm = min(TOKEN_TILE, n)
    width = lb_raw.shape[1]
    row = lambda i: (i, 0)
    bf = jax.ShapeDtypeStruct((n, width), BF16)
    return pl.pallas_call(
        functools.partial(_proj_hgrn_kernel, layer=layer),
        out_shape=(bf, jax.ShapeDtypeStruct((n, width), F32), bf, bf),
        grid=(n // tm,),
        in_specs=[pl.BlockSpec((tm, d), row), _layer_block(norm, layer), _resident(lb_raw),
                  _layer_block(w_in, layer, (col_off, 4 * width))],
        out_specs=(pl.BlockSpec((tm, width), row),) * 4,
        compiler_params=_params("parallel"),
        name="proj_hgrn",
    )(x, norm, lb_raw, w_in)


def _attn_kernel(q_ref, kp_ref, kc_ref, vp_ref, vc_ref, o_ref, st_ref, *, nres, nblk):
    first = pl.program_id(2) == 0
    row = lax.broadcasted_iota(jnp.int32, (ATT_BLOCK, 2 * ATT_BLOCK), 0)
    col = lax.broadcasted_iota(jnp.int32, (ATT_BLOCK, 2 * ATT_BLOCK), 1)
    band = jnp.logical_and(col >= row, col <= row + ATT_BLOCK)
    low = jnp.where(first, ATT_BLOCK, 0)
    band_first = jnp.logical_and(col >= jnp.maximum(row, low), col <= row + ATT_BLOCK)
    ones = jnp.ones((2 * ATT_BLOCK, ATT_HEAD_DIM), BF16)
    lane = lax.broadcasted_iota(jnp.int32, (ATT_BLOCK, LANES), 1)
    for res in range(nres):
        for blk in range(nblk):
            rows = slice(blk * ATT_BLOCK, (blk + 1) * ATT_BLOCK)
            stats = jnp.zeros((ATT_BLOCK, LANES), F32)
            for hd in range(ATT_HEADS):
                sl = slice(hd * ATT_HEAD_DIM, (hd + 1) * ATT_HEAD_DIM)
                if blk == 0:
                    k2 = jnp.concatenate([kp_ref[res, :, sl], kc_ref[res, 0:ATT_BLOCK, sl]], axis=0)
                    v2 = jnp.concatenate([vp_ref[res, :, sl], vc_ref[res, 0:ATT_BLOCK, sl]], axis=0)
                    mask = band_first
                else:
                    win = slice((blk - 1) * ATT_BLOCK, (blk + 1) * ATT_BLOCK)
                    k2 = kc_ref[res, win, sl]
                    v2 = vc_ref[res, win, sl]
                    mask = band
                s = jnp.where(mask, _dot_nt(q_ref[res, rows, sl], k2), NEG_BIG)
                m = jnp.max(s, axis=-1, keepdims=True)
                p = jnp.exp(s - m).astype(BF16)
                o_den = _dot(p, jnp.concatenate([v2, ones], axis=1))
                den = o_den[:, ATT_HEAD_DIM:]
                o_ref[res, rows, sl] = (o_den[:, :ATT_HEAD_DIM] / den).astype(BF16)
                stats = jnp.where(lane == hd, m + jnp.log(den), stats)
            st_ref[res, rows, :] = stats


def _attn_group(q, k, v):
    batch, dil, cls, gw = q.shape
    assert cls % ATT_BLOCK == 0
    nblk = min(ATT_CHAINS, cls // ATT_BLOCK)
    nres = max(1, min(dil, ATT_CHAINS // nblk))
    qb = nblk * ATT_BLOCK
    cur = lambda b, r, i: (b, r, i, 0)
    prev = lambda b, r, i: (b, r, jnp.maximum(i * nblk - 1, 0), 0)
    cur_spec = pl.BlockSpec((None, nres, qb, gw), cur)
    prev_spec = pl.BlockSpec((None, nres, ATT_BLOCK, gw), prev)
    return pl.pallas_call(
        functools.partial(_attn_kernel, nres=nres, nblk=nblk),
        out_shape=(jax.ShapeDtypeStruct((batch, dil, cls, gw), BF16),
                   jax.ShapeDtypeStruct((batch, dil, cls, LANES), F32)),
        grid=(batch, dil // nres, cls // qb),
        in_specs=[cur_spec, prev_spec, cur_spec, prev_spec, cur_spec],
        out_specs=(pl.BlockSpec((None, nres, qb, gw), cur),
                   pl.BlockSpec((None, nres, qb, LANES), cur)),
        compiler_params=_params("parallel", "parallel", "arbitrary"),
        name=f"attn_dil{dil}",
    )(q, k, k, v, v)


def _hgrn_scores_midref(a_ref, qin_ref, kout_ref, q, kk, b, heads):
    c = b.shape[0]
    b_mid = b[c // 2 - 1:c // 2, :]
    b_last = b[c - 1:c, :]
    q_s = q * jnp.exp(b - b_mid)
    k_s = kk * jnp.exp(b_mid - b)
    qin_ref[...] = (q_s * jnp.exp(b_mid)).astype(BF16)
    kout_ref[...] = (k_s * jnp.exp(b_last - b_mid)).astype(BF16)
    q_s = q_s.astype(BF16)
    k_s = k_s.astype(BF16)
    r_i = lax.broadcasted_iota(jnp.int32, (c, c), 0)
    c_i = lax.broadcasted_iota(jnp.int32, (c, c), 1)
    causal = c_i <= r_i
    for hd in range(heads):
        sl = slice(hd * HGRN_HEAD_DIM, (hd + 1) * HGRN_HEAD_DIM)
        a_ref[hd] = jnp.where(causal, _dot_nt(q_s[:, sl], k_s[:, sl]), 0.0).astype(BF16)


def _hgrn_scores_bounded(a_ref, qin_ref, kout_ref, q, kk, b, heads):
    c, w = b.shape
    qin_ref[...] = (q * jnp.exp(b)).astype(BF16)
    kout_ref[...] = (kk * jnp.exp(b[c - 1:c, :] - b)).astype(BF16)
    r_i = lax.broadcasted_iota(jnp.int32, (c, c), 0)
    c_i = lax.broadcasted_iota(jnp.int32, (c, c), 1)
    rows = lax.broadcasted_iota(jnp.int32, (c, w), 0)

    levels = []
    half = SUBLANES
    while half < c:
        span = 2 * half
        pieces = [jnp.broadcast_to(b[s + half - 1:s + half, :], (span, w))
                  for s in range(0, c, span)]
        bref = pieces[0] if len(pieces) == 1 else jnp.concatenate(pieces, axis=0)
        upper = (rows & (span - 1)) >= half
        e = jnp.exp(jnp.where(upper, b - bref, bref - b))
        q_up = jnp.where(upper, q * e, 0.0).astype(BF16)
        k_lo = jnp.where(upper, 0.0, kk * e).astype(BF16)
        same = (r_i & ~(span - 1)) == (c_i & ~(span - 1))
        levels.append((q_up, k_lo, same))
        half = span

    nb = c // SUBLANES
    sub = lax.broadcasted_iota(jnp.int32, (nb, SUBLANES, w), 1)
    q3 = q.reshape(nb, SUBLANES, w)
    k3 = kk.reshape(nb, SUBLANES, w)
    b3 = b.reshape(nb, SUBLANES, w)
    near = []
    for dist in range(SUBLANES):
        if dist == 0:
            prod = q3 * k3
        else:
            k_r = pltpu.roll(k3, dist, 1)
            b_r = pltpu.roll(b3, dist, 1)
            prod = jnp.where(sub >= dist, q3 * k_r * jnp.exp(b3 - b_r), 0.0)
        near.append(prod.reshape(c, w))

    for hd in range(heads):
        sl = slice(hd * HGRN_HEAD_DIM, (hd + 1) * HGRN_HEAD_DIM)
        a = jnp.zeros((c, c), F32)
        for q_up, k_lo, same in levels:
            a = a + jnp.where(same, _dot_nt(q_up[:, sl], k_lo[:, sl]), 0.0)
        for dist in range(SUBLANES):
            col_val = jnp.sum(near[dist][:, sl], axis=-1, keepdims=True)
            a = a + jnp.where(c_i == r_i - dist, col_val, 0.0)
        a_ref[hd] = a.astype(BF16)


def _hgrn_kernel(q_ref, lf_ref, i_ref, og_ref, hn_ref, o_ref,
                 st_ref, a_ref, qin_ref, kout_ref, *, chunk, heads):
    c = chunk

    @pl.when(pl.program_id(1) == 0)
    def _():
        st_ref[...] = jnp.zeros_like(st_ref)

    q = q_ref[...].astype(F32)
    lf = lf_ref[...]
    v_bf = i_ref[...]

    r_i = lax.broadcasted_iota(jnp.int32, (c, c), 0)
    c_i = lax.broadcasted_iota(jnp.int32, (c, c), 1)
    tri = jnp.where(c_i <= r_i, 1.0, 0.0).astype(BF16)
    lf_hi = lf.astype(BF16)
    lf_lo = (lf - lf_hi.astype(F32)).astype(BF16)
    b = _dot(tri, lf_hi) + _dot(tri, lf_lo)

    kk = 1.0 - jnp.exp(lf)
    b_last = b[c - 1:c, :]
    b_mid = b[c // 2 - 1:c // 2, :]
    st_decay = jnp.exp(b_last)

    reach = jnp.max(jnp.maximum(b[0:1, :] - b_mid, b_mid - b_last))
    midref_ok = reach < HGRN_SAFE_EXPONENT

    @pl.when(midref_ok)
    def _():
        _hgrn_scores_midref(a_ref, qin_ref, kout_ref, q, kk, b, heads)

    @pl.when(jnp.logical_not(midref_ok))
    def _():
        _hgrn_scores_bounded(a_ref, qin_ref, kout_ref, q, kk, b, heads)

    gain = hn_ref[...]
    for hd in range(heads):
        sl = slice(hd * HGRN_HEAD_DIM, (hd + 1) * HGRN_HEAD_DIM)
        v_h = v_bf[:, sl]
        st_h = st_ref[hd]
        o_h = _dot(a_ref[hd], v_h) + _dot_nt(qin_ref[:, sl], st_h.astype(BF16))
        st_ref[hd] = st_h * st_decay[:, sl] + _dot_tn(v_h, kout_ref[:, sl])
        o_n = _rms(o_h, gain[:, sl])
        o_ref[:, sl] = (o_n * og_ref[:, sl].astype(F32)).astype(BF16)


def _hgrn(q, lf, i, og, head_gain, layer, batch):
    n, w = q.shape
    seq = n // batch
    c = min(HGRN_CHUNK, seq)
    heads = w // HGRN_HEAD_DIM
    nc = seq // c
    blk = lambda b, t: (b * nc + t, 0)
    spec = pl.BlockSpec((c, w), blk)
    return pl.pallas_call(
        functools.partial(_hgrn_kernel, chunk=c, heads=heads),
        out_shape=jax.ShapeDtypeStruct((n, w), BF16),
        grid=(batch, nc),
        in_specs=[spec, spec, spec, spec, _layer_block(head_gain, layer)],
        out_specs=spec,
        scratch_shapes=[pltpu.VMEM((heads, HGRN_HEAD_DIM, HGRN_HEAD_DIM), F32),
                        pltpu.VMEM((heads, c, c), BF16),
                        pltpu.VMEM((c, w), BF16),
                        pltpu.VMEM((c, w), BF16)],
        compiler_params=_params("parallel", "arbitrary"),
        name="hgrn",
    )(q, lf, i, og, head_gain)


def _load_by_token(src_ref, scr_ref):
    dil, rows, width = src_ref.shape
    if dil == 1:
        return src_ref[0].astype(F32)
    parts = []
    for c in range(width // LANES):
        cols = slice(c * LANES, (c + 1) * LANES)
        for r in range(dil):
            scr_ref[c, pl.ds(r, rows, stride=dil), :] = src_ref[r, :, cols].astype(F32)
        parts.append(scr_ref[c])
    return parts[0] if len(parts) == 1 else jnp.concatenate(parts, axis=-1)


def _mix_kernel(x_ref, g_ref, o0_ref, o1_ref, o2_ref, s0_ref, s1_ref, s2_ref, ob_ref,
                wg_ref, wa_ref, wb_ref, wo_ref, out_ref, *scr):
    x = x_ref[...]
    d = x.shape[1]
    h = _rms(x, g_ref[...]).astype(BF16)
    gate_a = jax.nn.sigmoid(_dot(h, wg_ref[:, :d]))
    gate_b = jax.nn.sigmoid(_dot(h, wg_ref[:, d:]))

    scr = list(scr)
    take = lambda ref: _load_by_token(ref, scr.pop(0) if ref.shape[0] > 1 else None)
    o0, o1, o2 = take(o0_ref), take(o1_ref), take(o2_ref)
    s0, s1, s2 = take(s0_ref), take(s1_ref), take(s2_ref)
    mx = jnp.maximum(jnp.maximum(s0, s1), s2)
    e0, e1, e2 = jnp.exp(s0 - mx), jnp.exp(s1 - mx), jnp.exp(s2 - mx)
    inv = 1.0 / (e0 + e1 + e2)
    w0, w1, w2 = e0 * inv, e1 * inv, e2 * inv
    parts = []
    for hd in range(ATT_HEADS):
        sl = slice(hd * ATT_HEAD_DIM, (hd + 1) * ATT_HEAD_DIM)
        parts.append(o0[:, sl] * w0[:, hd:hd + 1] + o1[:, sl] * w1[:, hd:hd + 1]
                     + o2[:, sl] * w2[:, hd:hd + 1])
    att = jnp.concatenate(parts, axis=-1).astype(BF16)

    y_a = _dot(att, wa_ref[...])
    y_b = _dot(ob_ref[...], wb_ref[...])
    merged = (gate_a * y_a + gate_b * y_b).astype(BF16)
    out_ref[...] = x + _dot(merged, wo_ref[...])


def _mix(x, norm, o_groups, s_groups, ob, w_in, gate_off, wa, wb, wo, layer):
    n, d = x.shape
    batch = o_groups[0].shape[0]
    seq = n // batch
    tm = min(TOKEN_TILE, seq)
    per_b = seq // tm
    row = lambda b, i: (b * per_b + i, 0)
    rows = lambda a: pl.BlockSpec((tm, a.shape[1]), row)
    by_class = lambda a: pl.BlockSpec((None, a.shape[1], tm // a.shape[1], a.shape[3]),
                                      lambda b, i: (b, 0, i, 0))
    grouped = [*o_groups, *s_groups]
    args = [x, norm, *grouped, ob, w_in, wa, wb, wo]
    in_specs = ([rows(x), _layer_block(norm, layer)] + [by_class(a) for a in grouped]
                + [rows(ob), _layer_block(w_in, layer, (gate_off, 2 * d)),
                   _layer_block(wa, layer), _layer_block(wb, layer), _layer_block(wo, layer)])
    scratch = [pltpu.VMEM((a.shape[3] // LANES, tm, LANES), F32)
               for a in grouped if a.shape[1] > 1]
    return pl.pallas_call(
        _mix_kernel,
        out_shape=jax.ShapeDtypeStruct((n, d), F32),
        grid=(batch, per_b),
        in_specs=in_specs,
        out_specs=pl.BlockSpec((tm, d), row),
        scratch_shapes=scratch,
        compiler_params=_params("parallel", "parallel"),
        name="mix_out",
    )(*args)


def _mem_kv_kernel(m_ref, g_ref, wk_ref, wv_ref, k_ref, v_ref):
    h = _rms(m_ref[...], g_ref[...]).astype(BF16)
    k_ref[...] = _dot(h, wk_ref[...]).astype(BF16)
    v_ref[...] = _dot(h, wv_ref[...]).astype(BF16)


def _mem_kv(mem, norm, wkv, layer):
    n, d = mem.shape
    tm = min(TOKEN_TILE, n)
    width = wkv.shape[2] // 2
    row = lambda i: (i, 0)
    out = jax.ShapeDtypeStruct((n, width), BF16)
    return pl.pallas_call(
        _mem_kv_kernel,
        out_shape=(out, out),
        grid=(n // tm,),
        in_specs=[pl.BlockSpec((tm, d), row), _layer_block(norm, layer),
                  _layer_block(wkv, layer, (0, width)), _layer_block(wkv, layer, (width, width))],
        out_specs=(pl.BlockSpec((tm, width), row),) * 2,
        compiler_params=_params("parallel"),
        name="mem_kv",
    )(mem, norm, wkv, wkv)


def _xattn_kernel(x_ref, g_ref, k_ref, v_ref, wq_ref, wo_ref, out_ref):
    x = x_ref[...]
    h = _rms(x, g_ref[...]).astype(BF16)
    dh = wq_ref.shape[1] // X_HEADS
    q = (_dot(h, wq_ref[...]) * (dh ** -0.5)).astype(BF16)
    outs = []
    for hd in range(X_HEADS):
        sl = slice(hd * dh, (hd + 1) * dh)
        s = _dot_nt(q[:, sl], k_ref[:, sl])
        m = jnp.max(s, axis=-1, keepdims=True)
        p = jnp.exp(s - m)
        den = jnp.sum(p, axis=-1, keepdims=True)
        outs.append(_dot(p.astype(BF16), v_ref[:, sl]) / den)
    o = jnp.concatenate(outs, axis=-1).astype(BF16)
    out_ref[...] = x + _dot(o, wo_ref[...])


def _xattn(x, norm, k, v, wq, wo, layer, batch):
    n, d = x.shape
    seq = n // batch
    tm = min(TOKEN_TILE, seq)
    n_mem = k.shape[0] // batch
    per_b = seq // tm
    row = lambda b, i: (b * per_b + i, 0)
    kv = pl.BlockSpec((n_mem, k.shape[1]), lambda b, i: (b, 0))
    return pl.pallas_call(
        _xattn_kernel,
        out_shape=jax.ShapeDtypeStruct((n, d), F32),
        grid=(batch, per_b),
        in_specs=[pl.BlockSpec((tm, d), row), _layer_block(norm, layer), kv, kv,
                  _layer_block(wq, layer), _layer_block(wo, layer)],
        out_specs=pl.BlockSpec((tm, d), row),
        compiler_params=_params("parallel", "parallel"),
        name="xattn",
    )(x, norm, k, v, wq, wo)


def kernel(x, mem, positions, ffn1_norm, ffn1_w_gu, ffn1_w_down, mix_norm, w_in,
           hgrn_lower_bounds, hgrn_head_norm, w_att_branch, w_hgrn_branch, w_mix_out,
           xattn_norm, mem_norm, xattn_wq, xattn_wkv, xattn_wo,
           ffn2_norm, ffn2_w_gu, ffn2_w_down, final_norm):
    batch, seq, d = x.shape
    depth = w_in.shape[0]
    n = batch * seq
    att_w = len(ATT_GROUPS) * ATT_GROUP_WIDTH
    hg_w = hgrn_head_norm.shape[1]
    hgrn_off = 3 * att_w
    gate_off = hgrn_off + 4 * hg_w

    xf = x.reshape(n, d)
    memf = mem.reshape(batch * mem.shape[1], d)
    cos2, sin2 = _rope_tables(positions.reshape(n, 1).astype(F32))

    bf = lambda a: a.astype(BF16)
    gain = lambda a: a.reshape(a.shape[0], 1, a.shape[1])
    w_in, w_att_branch, w_hgrn_branch, w_mix_out = map(
        bf, (w_in, w_att_branch, w_hgrn_branch, w_mix_out))
    ffn1_w_gu, ffn1_w_down, ffn2_w_gu, ffn2_w_down = map(
        bf, (ffn1_w_gu, ffn1_w_down, ffn2_w_gu, ffn2_w_down))
    xattn_wq, xattn_wkv, xattn_wo = map(bf, (xattn_wq, xattn_wkv, xattn_wo))
    ffn1_norm, mix_norm, hgrn_head_norm, xattn_norm, mem_norm, ffn2_norm = map(
        gain, (ffn1_norm, mix_norm, hgrn_head_norm, xattn_norm, mem_norm, ffn2_norm))

    for l in range(depth):
        xf = _ffn(xf, ffn1_norm, ffn1_w_gu, ffn1_w_down, l)

        o_groups, s_groups = [], []
        for q_a, k_a, v_a in _proj_attn(xf, mix_norm, cos2, sin2, w_in, l, batch):
            o_g, s_g = _attn_group(q_a, k_a, v_a)
            o_groups.append(o_g)
            s_groups.append(s_g)
        q_b, lf_b, i_b, og_b = _proj_hgrn(xf, mix_norm, hgrn_lower_bounds, w_in, hgrn_off, l)
        o_b = _hgrn(q_b, lf_b, i_b, og_b, hgrn_head_norm, l, batch)
        xf = _mix(xf, mix_norm, o_groups, s_groups, o_b, w_in, gate_off,
                  w_att_branch, w_hgrn_branch, w_mix_out, l)

        k_x, v_x = _mem_kv(memf, mem_norm, xattn_wkv, l)
        xf = _xattn(xf, xattn_norm, k_x, v_x, xattn_wq, xattn_wo, l, batch)

        final_g = final_norm.reshape(1, -1) if l == depth - 1 else None
        xf = _ffn(xf, ffn2_norm, ffn2_w_gu, ffn2_w_down, l, final_g=final_g)

    return xf.reshape(batch, seq, d)
```

```python
import functools
import math

import jax
import jax.numpy as jnp
from jax import lax
from jax.experimental import pallas as pl
from jax.experimental.pallas import tpu as pltpu

F32 = jnp.float32
BF16 = jnp.bfloat16

ATT_GROUPS = ((128, 1), (512, 4), (2048, 16))
ATT_HEADS = 4
ATT_HEAD_DIM = 128
ATT_GROUP_WIDTH = ATT_HEADS * ATT_HEAD_DIM
HGRN_HEAD_DIM = 128
X_HEADS = 4
ROPE_THETA = 10000.0
EPS = 1e-6

LANES = 128
SUBLANES = 8
MXU_TILE = 256
VMEM_LIMIT_BYTES = 56 * 1024 * 1024

TOKEN_TILE = 512
FFN_CHUNK_TILES = 3
ATT_BLOCK = 128
ATT_CHAINS = 8
HGRN_CHUNK = 128
HGRN_SAFE_EXPONENT = 80.0
NEG_BIG = -1e30


def _params(*sem):
    return pltpu.CompilerParams(dimension_semantics=sem,
                                vmem_limit_bytes=VMEM_LIMIT_BYTES)


def _rms(x, g):
    ms = jnp.mean(x * x, axis=-1, keepdims=True)
    return x * lax.rsqrt(ms + EPS) * g


def _dot(a, b):
    return jnp.dot(a, b, preferred_element_type=F32)


def _dot_nt(a, b):
    return lax.dot_general(a, b, (((1,), (1,)), ((), ())), preferred_element_type=F32)


def _dot_tn(a, b):
    return lax.dot_general(a, b, (((0,), (0,)), ((), ())), preferred_element_type=F32)


def _resident(a):
    return pl.BlockSpec(a.shape, lambda *_: (0,) * a.ndim, pipeline_mode=pl.Buffered(1))


def _layer_block(a, layer, cols=None):
    _, rows, width = a.shape
    if cols is None:
        return pl.BlockSpec((None, rows, width), lambda *_: (layer, 0, 0),
                            pipeline_mode=pl.Buffered(1))
    off, size = cols
    return pl.BlockSpec((None, pl.Element(rows), pl.Element(size)), lambda *_: (layer, 0, off),
                        pipeline_mode=pl.Buffered(1))


def _rope_table_kernel(pos_ref, cos_ref, sin_ref):
    half = ATT_HEAD_DIM // 2
    lane = lax.broadcasted_iota(jnp.int32, (1, ATT_HEAD_DIM), 1)
    idx = jnp.where(lane >= half, lane - half, lane).astype(F32)
    inv_freq = jnp.exp(idx * (-2.0 * math.log(ROPE_THETA) / ATT_HEAD_DIM))
    ang = pos_ref[...] * inv_freq
    sign = jnp.where(lane >= half, 1.0, -1.0)
    cos_ref[...] = jnp.cos(ang)
    sin_ref[...] = jnp.sin(ang) * sign


def _rope_tables(pos_col):
    n = pos_col.shape[0]
    tm = min(n, 2048)
    out = jax.ShapeDtypeStruct((n, ATT_HEAD_DIM), F32)
    return pl.pallas_call(
        _rope_table_kernel,
        out_shape=(out, out),
        grid=(n // tm,),
        in_specs=[pl.BlockSpec((tm, 1), lambda i: (i, 0))],
        out_specs=(pl.BlockSpec((tm, ATT_HEAD_DIM), lambda i: (i, 0)),) * 2,
        compiler_params=_params("parallel"),
        name="rope_tables",
    )(pos_col)


def _ffn_chunks(d_ff):
    assert d_ff % MXU_TILE == 0
    tiles = d_ff // MXU_TILE
    n_chunks = -(-tiles // FFN_CHUNK_TILES)
    bounds = [MXU_TILE * (tiles * k // n_chunks) for k in range(n_chunks + 1)]
    return list(zip(bounds[:-1], bounds[1:]))


def _ffn_kernel(*refs, final):
    if final:
        x_ref, g_ref, wg_ref, wu_ref, wd_ref, fg_ref, o_ref = refs
    else:
        x_ref, g_ref, wg_ref, wu_ref, wd_ref, o_ref = refs
    x = x_ref[...]
    h = _rms(x, g_ref[...]).astype(BF16)
    acc = None
    for lo, hi in _ffn_chunks(wg_ref.shape[1]):
        gate = _dot(h, wg_ref[:, lo:hi])
        up = _dot(h, wu_ref[:, lo:hi])
        act = (gate * jax.nn.sigmoid(gate) * up).astype(BF16)
        part = _dot(act, wd_ref[lo:hi, :])
        acc = part if acc is None else acc + part
    y = x + 0.5 * acc
    if final:
        y = _rms(y, fg_ref[...])
    o_ref[...] = y


def _ffn(x, norm, w_gu, w_down, layer, final_g=None):
    n, d = x.shape
    d_ff = w_down.shape[1]
    tm = min(TOKEN_TILE, n)
    final = final_g is not None
    row = pl.BlockSpec((tm, d), lambda i: (i, 0))
    in_specs = [row, _layer_block(norm, layer), _layer_block(w_gu, layer, (0, d_ff)),
                _layer_block(w_gu, layer, (d_ff, d_ff)), _layer_block(w_down, layer)]
    args = [x, norm, w_gu, w_gu, w_down]
    if final:
        in_specs.append(_resident(final_g))
        args.append(final_g)
    return pl.pallas_call(
        functools.partial(_ffn_kernel, final=final),
        out_shape=jax.ShapeDtypeStruct((n, d), F32),
        grid=(n // tm,),
        in_specs=in_specs,
        out_specs=row,
        compiler_params=_params("parallel"),
        name="ffn_final" if final else "ffn",
    )(*args)


def _store_by_class(out_ref, scr_ref, val, dil):
    if dil == 1:
        out_ref[0] = val.astype(out_ref.dtype)
        return
    rows = val.shape[0] // dil
    for c in range(val.shape[1] // LANES):
        cols = slice(c * LANES, (c + 1) * LANES)
        scr_ref[c] = val[:, cols]
        for r in range(dil):
            out_ref[r, :, cols] = scr_ref[c, pl.ds(r, rows, stride=dil), :].astype(out_ref.dtype)


def _proj_attn_kernel(x_ref, g_ref, cos_ref, sin_ref, w_ref, *rest, dils):
    n_g = len(dils)
    outs, scr = rest[:3 * n_g], rest[3 * n_g:]
    gw = ATT_GROUP_WIDTH
    h = _rms(x_ref[...], g_ref[...]).astype(BF16)
    cos = cos_ref[...]
    sin = sin_ref[...]
    scale = ATT_HEAD_DIM ** -0.5 * math.log2(math.e)
    for gi, dil in enumerate(dils):
        part = lambda p: w_ref[:, (p * n_g + gi) * gw:(p * n_g + gi + 1) * gw]
        zq = _dot(h, part(0))
        zk = _dot(h, part(1))
        rq, rk = [], []
        for hd in range(ATT_HEADS):
            sl = slice(hd * ATT_HEAD_DIM, (hd + 1) * ATT_HEAD_DIM)
            xq = zq[:, sl]
            xk = zk[:, sl]
            rq.append((xq * cos + pltpu.roll(xq, ATT_HEAD_DIM // 2, 1) * sin) * scale)
            rk.append(xk * cos + pltpu.roll(xk, ATT_HEAD_DIM // 2, 1) * sin)
        q_ref, k_ref, v_ref = outs[3 * gi:3 * gi + 3]
        _store_by_class(q_ref, scr[0] if scr else None, jnp.concatenate(rq, axis=-1), dil)
        _store_by_class(k_ref, scr[1] if scr else None, jnp.concatenate(rk, axis=-1), dil)
        _store_by_class(v_ref, scr[2] if scr else None, _dot(h, part(2)), dil)


def _proj_attn(x, norm, cos2, sin2, w_in, layer, batch):
    n, d = x.shape
    seq = n // batch
    tm = min(TOKEN_TILE, seq)
    per_b = seq // tm
    gw = ATT_GROUP_WIDTH
    dils = tuple(dil for _, dil in ATT_GROUPS)
    row = lambda b, i: (b * per_b + i, 0)
    out_shape, out_specs = [], []
    for dil in dils:
        out_shape += [jax.ShapeDtypeStruct((batch, dil, seq // dil, gw), BF16)] * 3
        out_specs += [pl.BlockSpec((None, dil, tm // dil, gw), lambda b, i: (b, 0, i, 0))] * 3
    scratch = [pltpu.VMEM((gw // LANES, tm, LANES), F32)] * 3 if max(dils) > 1 else []
    outs = pl.pallas_call(
        functools.partial(_proj_attn_kernel, dils=dils),
        out_shape=tuple(out_shape),
        grid=(batch, per_b),
        in_specs=[
            pl.BlockSpec((tm, d), row),
            _layer_block(norm, layer),
            pl.BlockSpec((tm, ATT_HEAD_DIM), row),
            pl.BlockSpec((tm, ATT_HEAD_DIM), row),
            _layer_block(w_in, layer, (0, 3 * len(dils) * gw)),
        ],
        out_specs=tuple(out_specs),
        scratch_shapes=scratch,
        compiler_params=_params("parallel", "parallel"),
        name="proj_attn",
    )(x, norm, cos2, sin2, w_in)
    return [outs[3 * gi:3 * gi + 3] for gi in range(len(dils))]


def _proj_hgrn_kernel(x_ref, g_ref, lbraw_ref, w_ref, q_ref, lf_ref, i_ref, og_ref, *, layer):
    h = _rms(x_ref[...], g_ref[...]).astype(BF16)
    raw = lbraw_ref[...]
    p = jnp.exp(raw - jnp.max(raw, axis=0, keepdims=True))
    p = p / jnp.sum(p, axis=0, keepdims=True)
    lb = jnp.sum(p[:layer + 1], axis=0, keepdims=True) - p[0:1]

    hw = q_ref.shape[1]
    sec = lambda k: w_ref[:, k * hw:(k + 1) * hw]
    q_ref[...] = (_dot(h, sec(0)) * (HGRN_HEAD_DIM ** -0.5)).astype(BF16)
    f = lb + (1.0 - lb) * jax.nn.sigmoid(_dot(h, sec(1)))
    lf_ref[...] = jnp.log(f)
    i_ref[...] = _dot(h, sec(2)).astype(BF16)
    og = _dot(h, sec(3))
    og_ref[...] = (og * jax.nn.sigmoid(og)).astype(BF16)


def _proj_hgrn(x, norm, lb_raw, w_in, col_off, layer):
    n, d = x.shape
    tm = min(TOKEN_TILE, n)
    width = lb_raw.shape[1]
    row = lambda i: (i, 0)
    bf = jax.ShapeDtypeStruct((n, width), BF16)
    return pl.pallas_call(
        functools.partial(_proj_hgrn_kernel, layer=layer),
        out_shape=(bf, jax.ShapeDtypeStruct((n, width), F32), bf, bf),
        grid=(n // tm,),
        in_specs=[pl.BlockSpec((tm, d), row), _layer_block(norm, layer), _resident(lb_raw),
                  _layer_block(w_in, layer, (col_off, 4 * width))],
        out_specs=(pl.BlockSpec((tm, width), row),) * 4,
        compiler_params=_params("parallel"),
        name="proj_hgrn",
    )(x, norm, lb_raw, w_in)


def _attn_kernel(q_ref, kp_ref, kc_ref, vp_ref, vc_ref, o_ref, st_ref, *, nres, nblk):
    first = pl.program_id(2) == 0
    row = lax.broadcasted_iota(jnp.int32, (ATT_BLOCK, 2 * ATT_BLOCK), 0)
    col = lax.broadcasted_iota(jnp.int32, (ATT_BLOCK, 2 * ATT_BLOCK), 1)
    band = jnp.logical_and(col >= row, col <= row + ATT_BLOCK)
    low = jnp.where(first, ATT_BLOCK, 0)
    band_first = jnp.logical_and(col >= jnp.maximum(row, low), col <= row + ATT_BLOCK)
    ones = jnp.ones((2 * ATT_BLOCK, ATT_HEAD_DIM), BF16)
    lane = lax.broadcasted_iota(jnp.int32, (ATT_BLOCK, LANES), 1)
    for res in range(nres):
        for blk in range(nblk):
            rows = slice(blk * ATT_BLOCK, (blk + 1) * ATT_BLOCK)
            stats = jnp.ones((ATT_BLOCK, LANES), F32)
            for hd in range(ATT_HEADS):
                sl = slice(hd * ATT_HEAD_DIM, (hd + 1) * ATT_HEAD_DIM)
                if blk == 0:
                    k2 = jnp.concatenate([kp_ref[res, :, sl], kc_ref[res, 0:ATT_BLOCK, sl]], axis=0)
                    v2 = jnp.concatenate([vp_ref[res, :, sl], vc_ref[res, 0:ATT_BLOCK, sl]], axis=0)
                    mask = band_first
                else:
                    win = slice((blk - 1) * ATT_BLOCK, (blk + 1) * ATT_BLOCK)
                    k2 = kc_ref[res, win, sl]
                    v2 = vc_ref[res, win, sl]
                    mask = band
                s = jnp.where(mask, _dot_nt(q_ref[res, rows, sl], k2), NEG_BIG)
                m = jnp.max(s, axis=-1, keepdims=True)
                p = jnp.exp2(s - m).astype(BF16)
                o_den = _dot(p, jnp.concatenate([v2, ones], axis=1))
                den = o_den[:, ATT_HEAD_DIM:]
                o_ref[res, rows, sl] = o_den[:, :ATT_HEAD_DIM].astype(BF16)
                stats = jnp.where(lane == hd, m, jnp.where(lane == ATT_HEADS + hd, den, stats))
            st_ref[res, rows, :] = stats


def _attn_group(q, k, v):
    batch, dil, cls, gw = q.shape
    assert cls % ATT_BLOCK == 0
    nblk = min(ATT_CHAINS, cls // ATT_BLOCK)
    nres = max(1, min(dil, ATT_CHAINS // nblk))
    qb = nblk * ATT_BLOCK
    cur = lambda b, r, i: (b, r, i, 0)
    prev = lambda b, r, i: (b, r, jnp.maximum(i * nblk - 1, 0), 0)
    cur_spec = pl.BlockSpec((None, nres, qb, gw), cur)
    prev_spec = pl.BlockSpec((None, nres, ATT_BLOCK, gw), prev)
    return pl.pallas_call(
        functools.partial(_attn_kernel, nres=nres, nblk=nblk),
        out_shape=(jax.ShapeDtypeStruct((batch, dil, cls, gw), BF16),
                   jax.ShapeDtypeStruct((batch, dil, cls, LANES), F32)),
        grid=(batch, dil // nres, cls // qb),
        in_specs=[cur_spec, prev_spec, cur_spec, prev_spec, cur_spec],
        out_specs=(pl.BlockSpec((None, nres, qb, gw), cur),
                   pl.BlockSpec((None, nres, qb, LANES), cur)),
        compiler_params=_params("parallel", "parallel", "arbitrary"),
        name=f"attn_dil{dil}",
    )(q, k, k, v, v)


def _hgrn_decay_ahead(lf_ref, b_ref, reach_ref, slot):
    c = lf_ref.shape[0]
    lf = lf_ref[...]
    r_i = lax.broadcasted_iota(jnp.int32, (c, c), 0)
    c_i = lax.broadcasted_iota(jnp.int32, (c, c), 1)
    tri = jnp.where(c_i <= r_i, 1.0, 0.0).astype(BF16)
    lf_hi = lf.astype(BF16)
    lf_lo = (lf - lf_hi.astype(F32)).astype(BF16)
    b = _dot(tri, lf_hi) + _dot(tri, lf_lo)
    b_ref[slot] = b
    b_mid = b[c // 2 - 1:c // 2, :]
    reach_ref[slot] = jnp.max(jnp.maximum(b[0:1, :] - b_mid, b_mid - b[c - 1:c, :]))


def _hgrn_scores_midref(q, kk, b, heads):
    c = b.shape[0]
    b_mid = b[c // 2 - 1:c // 2, :]
    b_last = b[c - 1:c, :]
    q_s = q * jnp.exp(b - b_mid)
    k_s = kk * jnp.exp(b_mid - b)
    q_in = (q_s * jnp.exp(b_mid)).astype(BF16)
    k_out = (k_s * jnp.exp(b_last - b_mid)).astype(BF16)
    q_s = q_s.astype(BF16)
    k_s = k_s.astype(BF16)
    r_i = lax.broadcasted_iota(jnp.int32, (c, c), 0)
    c_i = lax.broadcasted_iota(jnp.int32, (c, c), 1)
    causal = c_i <= r_i
    scores = []
    for hd in range(heads):
        sl = slice(hd * HGRN_HEAD_DIM, (hd + 1) * HGRN_HEAD_DIM)
        scores.append(jnp.where(causal, _dot_nt(q_s[:, sl], k_s[:, sl]), 0.0).astype(BF16))
    return scores, q_in, k_out


def _hgrn_scores_bounded(q, kk, b, heads):
    c, w = b.shape
    q_in = (q * jnp.exp(b)).astype(BF16)
    k_out = (kk * jnp.exp(b[c - 1:c, :] - b)).astype(BF16)
    r_i = lax.broadcasted_iota(jnp.int32, (c, c), 0)
    c_i = lax.broadcasted_iota(jnp.int32, (c, c), 1)
    rows = lax.broadcasted_iota(jnp.int32, (c, w), 0)

    levels = []
    half = SUBLANES
    while half < c:
        span = 2 * half
        pieces = [jnp.broadcast_to(b[s + half - 1:s + half, :], (span, w))
                  for s in range(0, c, span)]
        bref = pieces[0] if len(pieces) == 1 else jnp.concatenate(pieces, axis=0)
        upper = (rows & (span - 1)) >= half
        e = jnp.exp(jnp.where(upper, b - bref, bref - b))
        q_up = jnp.where(upper, q * e, 0.0).astype(BF16)
        k_lo = jnp.where(upper, 0.0, kk * e).astype(BF16)
        same = (r_i & ~(span - 1)) == (c_i & ~(span - 1))
        levels.append((q_up, k_lo, same))
        half = span

    nb = c // SUBLANES
    sub = lax.broadcasted_iota(jnp.int32, (nb, SUBLANES, w), 1)
    q3 = q.reshape(nb, SUBLANES, w)
    k3 = kk.reshape(nb, SUBLANES, w)
    b3 = b.reshape(nb, SUBLANES, w)
    near = []
    for dist in range(SUBLANES):
        if dist == 0:
            prod = q3 * k3
        else:
            k_r = pltpu.roll(k3, dist, 1)
            b_r = pltpu.roll(b3, dist, 1)
            prod = jnp.where(sub >= dist, q3 * k_r * jnp.exp(b3 - b_r), 0.0)
        near.append(prod.reshape(c, w))

    scores = []
    for hd in range(heads):
        sl = slice(hd * HGRN_HEAD_DIM, (hd + 1) * HGRN_HEAD_DIM)
        a = jnp.zeros((c, c), F32)
        for q_up, k_lo, same in levels:
            a = a + jnp.where(same, _dot_nt(q_up[:, sl], k_lo[:, sl]), 0.0)
        for dist in range(SUBLANES):
            col_val = jnp.sum(near[dist][:, sl], axis=-1, keepdims=True)
            a = a + jnp.where(c_i == r_i - dist, col_val, 0.0)
        scores.append(a.astype(BF16))
    return scores, q_in, k_out


def _hgrn_chunk(q_ref, lf_ref, i_ref, og_ref, hn_ref, o_ref, st_ref, b, score_fn, heads):
    c = b.shape[0]
    q = q_ref[...].astype(F32)
    kk = 1.0 - jnp.exp(lf_ref[...])
    v_bf = i_ref[...]
    scores, q_in, k_out = score_fn(q, kk, b, heads)
    st_decay = jnp.exp(b[c - 1:c, :])
    gain = hn_ref[...]
    for hd in range(heads):
        sl = slice(hd * HGRN_HEAD_DIM, (hd + 1) * HGRN_HEAD_DIM)
        v_h = v_bf[:, sl]
        st_h = st_ref[hd]
        o_h = _dot(scores[hd], v_h) + _dot_nt(q_in[:, sl], st_h.astype(BF16))
        st_ref[hd] = st_h * st_decay[:, sl] + _dot_tn(v_h, k_out[:, sl])
        o_n = _rms(o_h, gain[:, sl])
        o_ref[:, sl] = (o_n * og_ref[:, sl].astype(F32)).astype(BF16)


def _hgrn_kernel(q_ref, lf_ref, lf_next_ref, i_ref, og_ref, hn_ref, o_ref,
                 st_ref, b_ref, reach_ref, *, heads):
    t = pl.program_id(1)
    slot = t % 2

    @pl.when(t == 0)
    def _():
        st_ref[...] = jnp.zeros_like(st_ref)
        _hgrn_decay_ahead(lf_ref, b_ref, reach_ref, 0)

    midref_ok = reach_ref[slot] < HGRN_SAFE_EXPONENT
    for take, score_fn in ((midref_ok, _hgrn_scores_midref),
                           (jnp.logical_not(midref_ok), _hgrn_scores_bounded)):
        @pl.when(take)
        def _():
            _hgrn_chunk(q_ref, lf_ref, i_ref, og_ref, hn_ref, o_ref, st_ref,
                        b_ref[slot], score_fn, heads)
            _hgrn_decay_ahead(lf_next_ref, b_ref, reach_ref, 1 - slot)


def _hgrn(q, lf, i, og, head_gain, layer, batch):
    n, w = q.shape
    seq = n // batch
    c = min(HGRN_CHUNK, seq)
    heads = w // HGRN_HEAD_DIM
    nc = seq // c
    blk = lambda b, t: (b * nc + t, 0)
    nxt = lambda b, t: (b * nc + jnp.minimum(t + 1, nc - 1), 0)
    spec = pl.BlockSpec((c, w), blk)
    return pl.pallas_call(
        functools.partial(_hgrn_kernel, heads=heads),
        out_shape=jax.ShapeDtypeStruct((n, w), BF16),
        grid=(batch, nc),
        in_specs=[spec, spec, pl.BlockSpec((c, w), nxt), spec, spec,
                  _layer_block(head_gain, layer)],
        out_specs=spec,
        scratch_shapes=[pltpu.VMEM((heads, HGRN_HEAD_DIM, HGRN_HEAD_DIM), F32),
                        pltpu.VMEM((2, c, w), F32),
                        pltpu.SMEM((2,), F32)],
        compiler_params=_params("parallel", "arbitrary"),
        name="hgrn",
    )(q, lf, lf, i, og, head_gain)


def _load_by_token(src_ref, scr_ref):
    dil, rows, width = src_ref.shape
    if dil == 1:
        return src_ref[0].astype(F32)
    parts = []
    for c in range(width // LANES):
        cols = slice(c * LANES, (c + 1) * LANES)
        for r in range(dil):
            scr_ref[c, pl.ds(r, rows, stride=dil), :] = src_ref[r, :, cols].astype(F32)
        parts.append(scr_ref[c])
    return parts[0] if len(parts) == 1 else jnp.concatenate(parts, axis=-1)


def _mix_kernel(x_ref, g_ref, o0_ref, o1_ref, o2_ref, s0_ref, s1_ref, s2_ref, ob_ref,
                wg_ref, wa_ref, wb_ref, wo_ref, out_ref, *scr):
    x = x_ref[...]
    d = x.shape[1]
    h = _rms(x, g_ref[...]).astype(BF16)
    gate_a = jax.nn.sigmoid(_dot(h, wg_ref[:, :d]))
    gate_b = jax.nn.sigmoid(_dot(h, wg_ref[:, d:]))

    scr = list(scr)
    take = lambda ref: _load_by_token(ref, scr.pop(0) if ref.shape[0] > 1 else None)
    o0, o1, o2 = take(o0_ref), take(o1_ref), take(o2_ref)
    s0, s1, s2 = take(s0_ref), take(s1_ref), take(s2_ref)
    head_lane = lax.broadcasted_iota(jnp.int32, s0.shape, 1) < ATT_HEADS
    lse = [s + jnp.log2(jnp.where(head_lane, pltpu.roll(s, LANES - ATT_HEADS, 1), 1.0))
           for s in (s0, s1, s2)]
    mx = jnp.maximum(jnp.maximum(lse[0], lse[1]), lse[2])
    inv = 1.0 / (jnp.exp2(lse[0] - mx) + jnp.exp2(lse[1] - mx) + jnp.exp2(lse[2] - mx))
    w0, w1, w2 = (jnp.exp2(s - mx) * inv for s in (s0, s1, s2))
    parts = []
    for hd in range(ATT_HEADS):
        sl = slice(hd * ATT_HEAD_DIM, (hd + 1) * ATT_HEAD_DIM)
        parts.append(o0[:, sl] * w0[:, hd:hd + 1] + o1[:, sl] * w1[:, hd:hd + 1]
                     + o2[:, sl] * w2[:, hd:hd + 1])
    att = jnp.concatenate(parts, axis=-1).astype(BF16)

    y_a = _dot(att, wa_ref[...])
    y_b = _dot(ob_ref[...], wb_ref[...])
    merged = (gate_a * y_a + gate_b * y_b).astype(BF16)
    out_ref[...] = x + _dot(merged, wo_ref[...])


def _mix(x, norm, o_groups, s_groups, ob, w_in, gate_off, wa, wb, wo, layer):
    n, d = x.shape
    batch = o_groups[0].shape[0]
    seq = n // batch
    tm = min(TOKEN_TILE, seq)
    per_b = seq // tm
    row = lambda b, i: (b * per_b + i, 0)
    rows = lambda a: pl.BlockSpec((tm, a.shape[1]), row)
    by_class = lambda a: pl.BlockSpec((None, a.shape[1], tm // a.shape[1], a.shape[3]),
                                      lambda b, i: (b, 0, i, 0))
    grouped = [*o_groups, *s_groups]
    args = [x, norm, *grouped, ob, w_in, wa, wb, wo]
    in_specs = ([rows(x), _layer_block(norm, layer)] + [by_class(a) for a in grouped]
                + [rows(ob), _layer_block(w_in, layer, (gate_off, 2 * d)),
                   _layer_block(wa, layer), _layer_block(wb, layer), _layer_block(wo, layer)])
    scratch = [pltpu.VMEM((a.shape[3] // LANES, tm, LANES), F32)
               for a in grouped if a.shape[1] > 1]
    return pl.pallas_call(
        _mix_kernel,
        out_shape=jax.ShapeDtypeStruct((n, d), F32),
        grid=(batch, per_b),
        in_specs=in_specs,
        out_specs=pl.BlockSpec((tm, d), row),
        scratch_shapes=scratch,
        compiler_params=_params("parallel", "parallel"),
        name="mix_out",
    )(*args)


def _mem_kv_kernel(m_ref, g_ref, wk_ref, wv_ref, k_ref, v_ref):
    h = _rms(m_ref[...], g_ref[...]).astype(BF16)
    k_ref[...] = _dot(h, wk_ref[...]).astype(BF16)
    v_ref[...] = _dot(h, wv_ref[...]).astype(BF16)


def _mem_kv(mem, norm, wkv, layer):
    n, d = mem.shape
    tm = min(TOKEN_TILE, n)
    width = wkv.shape[2] // 2
    row = lambda i: (i, 0)
    out = jax.ShapeDtypeStruct((n, width), BF16)
    return pl.pallas_call(
        _mem_kv_kernel,
        out_shape=(out, out),
        grid=(n // tm,),
        in_specs=[pl.BlockSpec((tm, d), row), _layer_block(norm, layer),
                  _layer_block(wkv, layer, (0, width)), _layer_block(wkv, layer, (width, width))],
        out_specs=(pl.BlockSpec((tm, width), row),) * 2,
        compiler_params=_params("parallel"),
        name="mem_kv",
    )(mem, norm, wkv, wkv)


def _xattn_kernel(x_ref, g_ref, k_ref, v_ref, wq_ref, wo_ref, out_ref):
    x = x_ref[...]
    h = _rms(x, g_ref[...]).astype(BF16)
    dh = wq_ref.shape[1] // X_HEADS
    q = (_dot(h, wq_ref[...]) * (dh ** -0.5)).astype(BF16)
    outs = []
    for hd in range(X_HEADS):
        sl = slice(hd * dh, (hd + 1) * dh)
        s = _dot_nt(q[:, sl], k_ref[:, sl])
        m = jnp.max(s, axis=-1, keepdims=True)
        p = jnp.exp(s - m)
        den = jnp.sum(p, axis=-1, keepdims=True)
        outs.append(_dot(p.astype(BF16), v_ref[:, sl]) / den)
    o = jnp.concatenate(outs, axis=-1).astype(BF16)
    out_ref[...] = x + _dot(o, wo_ref[...])


def _xattn(x, norm, k, v, wq, wo, layer, batch):
    n, d = x.shape
    seq = n // batch
    tm = min(TOKEN_TILE, seq)
    n_mem = k.shape[0] // batch
    per_b = seq // tm
    row = lambda b, i: (b * per_b + i, 0)
    kv = pl.BlockSpec((n_mem, k.shape[1]), lambda b, i: (b, 0))
    return pl.pallas_call(
        _xattn_kernel,
        out_shape=jax.ShapeDtypeStruct((n, d), F32),
        grid=(batch, per_b),
        in_specs=[pl.BlockSpec((tm, d), row), _layer_block(norm, layer), kv, kv,
                  _layer_block(wq, layer), _layer_block(wo, layer)],
        out_specs=pl.BlockSpec((tm, d), row),
        compiler_params=_params("parallel", "parallel"),
        name="xattn",
    )(x, norm, k, v, wq, wo)


def kernel(x, mem, positions, ffn1_norm, ffn1_w_gu, ffn1_w_down, mix_norm, w_in,
           hgrn_lower_bounds, hgrn_head_norm, w_att_branch, w_hgrn_branch, w_mix_out,
           xattn_norm, mem_norm, xattn_wq, xattn_wkv, xattn_wo,
           ffn2_norm, ffn2_w_gu, ffn2_w_down, final_norm):
    batch, seq, d = x.shape
    depth = w_in.shape[0]
    n = batch * seq
    att_w = len(ATT_GROUPS) * ATT_GROUP_WIDTH
    hg_w = hgrn_head_norm.shape[1]
    hgrn_off = 3 * att_w
    gate_off = hgrn_off + 4 * hg_w

    xf = x.reshape(n, d)
    memf = mem.reshape(batch * mem.shape[1], d)
    cos2, sin2 = _rope_tables(positions.reshape(n, 1).astype(F32))

    bf = lambda a: a.astype(BF16)
    gain = lambda a: a.reshape(a.shape[0], 1, a.shape[1])
    w_in, w_att_branch, w_hgrn_branch, w_mix_out = map(
        bf, (w_in, w_att_branch, w_hgrn_branch, w_mix_out))
    ffn1_w_gu, ffn1_w_down, ffn2_w_gu, ffn2_w_down = map(
        bf, (ffn1_w_gu, ffn1_w_down, ffn2_w_gu, ffn2_w_down))
    xattn_wq, xattn_wkv, xattn_wo = map(bf, (xattn_wq, xattn_wkv, xattn_wo))
    ffn1_norm, mix_norm, hgrn_head_norm, xattn_norm, mem_norm, ffn2_norm = map(
        gain, (ffn1_norm, mix_norm, hgrn_head_norm, xattn_norm, mem_norm, ffn2_norm))

    for l in range(depth):
        xf = _ffn(xf, ffn1_norm, ffn1_w_gu, ffn1_w_down, l)

        o_groups, s_groups = [], []
        for q_a, k_a, v_a in _proj_attn(xf, mix_norm, cos2, sin2, w_in, l, batch):
            o_g, s_g = _attn_group(q_a, k_a, v_a)
            o_groups.append(o_g)
            s_groups.append(s_g)
        q_b, lf_b, i_b, og_b = _proj_hgrn(xf, mix_norm, hgrn_lower_bounds, w_in, hgrn_off, l)
        o_b = _hgrn(q_b, lf_b, i_b, og_b, hgrn_head_norm, l, batch)
        xf = _mix(xf, mix_norm, o_groups, s_groups, o_b, w_in, gate_off,
                  w_att_branch, w_hgrn_branch, w_mix_out, l)

        k_x, v_x = _mem_kv(memf, mem_norm, xattn_wkv, l)
        xf = _xattn(xf, xattn_norm, k_x, v_x, xattn_wq, xattn_wo, l, batch)

        final_g = final_norm.reshape(1, -1) if l == depth - 1 else None
        xf = _ffn(xf, ffn2_norm, ffn2_w_gu, ffn2_w_down, l, final_g=final_g)

    return xf.reshape(batch, seq, d)
```

```python
import functools
import math

import jax
import jax.numpy as jnp
from jax import lax
from jax.experimental import pallas as pl
from jax.experimental.pallas import tpu as pltpu

F32 = jnp.float32
BF16 = jnp.bfloat16

ATT_GROUPS = ((128, 1), (512, 4), (2048, 16))
ATT_HEADS = 4
ATT_HEAD_DIM = 128
ATT_GROUP_WIDTH = ATT_HEADS * ATT_HEAD_DIM
HGRN_HEAD_DIM = 128
X_HEADS = 4
ROPE_THETA = 10000.0
EPS = 1e-6

LANES = 128
SUBLANES = 8
MXU_TILE = 256
VMEM_LIMIT_BYTES = 56 * 1024 * 1024

TOKEN_TILE = 1024
FFN_CHUNK_TILES = 3
ATT_BLOCK = 128
ATT_CHAINS = 8
HGRN_CHUNK = 128
HGRN_STEP_CHUNKS = 2
HGRN_SAFE_EXPONENT = 80.0
NEG_BIG = -1e30


def _params(*sem):
    return pltpu.CompilerParams(dimension_semantics=sem,
                                vmem_limit_bytes=VMEM_LIMIT_BYTES)


def _rms(x, g):
    ms = jnp.mean(x * x, axis=-1, keepdims=True)
    return x * lax.rsqrt(ms + EPS) * g


def _dot(a, b):
    return jnp.dot(a, b, preferred_element_type=F32)


def _dot_nt(a, b):
    return lax.dot_general(a, b, (((1,), (1,)), ((), ())), preferred_element_type=F32)


def _dot_tn(a, b):
    return lax.dot_general(a, b, (((0,), (0,)), ((), ())), preferred_element_type=F32)


def _resident(a):
    return pl.BlockSpec(a.shape, lambda *_: (0,) * a.ndim, pipeline_mode=pl.Buffered(1))


def _layer_block(a, layer, cols=None):
    _, rows, width = a.shape
    if cols is None:
        return pl.BlockSpec((None, rows, width), lambda *_: (layer, 0, 0),
                            pipeline_mode=pl.Buffered(1))
    off, size = cols
    return pl.BlockSpec((None, pl.Element(rows), pl.Element(size)), lambda *_: (layer, 0, off),
                        pipeline_mode=pl.Buffered(1))


def _rope_table_kernel(pos_ref, cos_ref, sin_ref):
    half = ATT_HEAD_DIM // 2
    lane = lax.broadcasted_iota(jnp.int32, (1, ATT_HEAD_DIM), 1)
    idx = jnp.where(lane >= half, lane - half, lane).astype(F32)
    inv_freq = jnp.exp(idx * (-2.0 * math.log(ROPE_THETA) / ATT_HEAD_DIM))
    ang = pos_ref[...] * inv_freq
    sign = jnp.where(lane >= half, 1.0, -1.0)
    cos_ref[...] = jnp.cos(ang)
    sin_ref[...] = jnp.sin(ang) * sign


def _rope_tables(pos_col):
    n = pos_col.shape[0]
    tm = min(n, 2048)
    out = jax.ShapeDtypeStruct((n, ATT_HEAD_DIM), F32)
    return pl.pallas_call(
        _rope_table_kernel,
        out_shape=(out, out),
        grid=(n // tm,),
        in_specs=[pl.BlockSpec((tm, 1), lambda i: (i, 0))],
        out_specs=(pl.BlockSpec((tm, ATT_HEAD_DIM), lambda i: (i, 0)),) * 2,
        compiler_params=_params("parallel"),
        name="rope_tables",
    )(pos_col)


def _ffn_chunks(d_ff):
    assert d_ff % MXU_TILE == 0
    tiles = d_ff // MXU_TILE
    n_chunks = -(-tiles // FFN_CHUNK_TILES)
    bounds = [MXU_TILE * (tiles * k // n_chunks) for k in range(n_chunks + 1)]
    return list(zip(bounds[:-1], bounds[1:]))


def _ffn_kernel(*refs, final):
    if final:
        x_ref, g_ref, wg_ref, wu_ref, wd_ref, fg_ref, o_ref = refs
    else:
        x_ref, g_ref, wg_ref, wu_ref, wd_ref, o_ref = refs
    x = x_ref[...]
    h = _rms(x, g_ref[...]).astype(BF16)
    acc = None
    for lo, hi in _ffn_chunks(wg_ref.shape[1]):
        gate = _dot(h, wg_ref[:, lo:hi])
        up = _dot(h, wu_ref[:, lo:hi])
        act = (gate * jax.nn.sigmoid(gate) * up).astype(BF16)
        part = _dot(act, wd_ref[lo:hi, :])
        acc = part if acc is None else acc + part
    y = x + 0.5 * acc
    if final:
        y = _rms(y, fg_ref[...])
    o_ref[...] = y


def _ffn(x, norm, w_gu, w_down, layer, final_g=None):
    n, d = x.shape
    d_ff = w_down.shape[1]
    tm = min(TOKEN_TILE, n)
    final = final_g is not None
    row = pl.BlockSpec((tm, d), lambda i: (i, 0))
    in_specs = [row, _layer_block(norm, layer), _layer_block(w_gu, layer, (0, d_ff)),
                _layer_block(w_gu, layer, (d_ff, d_ff)), _layer_block(w_down, layer)]
    args = [x, norm, w_gu, w_gu, w_down]
    if final:
        in_specs.append(_resident(final_g))
        args.append(final_g)
    return pl.pallas_call(
        functools.partial(_ffn_kernel, final=final),
        out_shape=jax.ShapeDtypeStruct((n, d), F32),
        grid=(n // tm,),
        in_specs=in_specs,
        out_specs=row,
        compiler_params=_params("parallel"),
        name="ffn_final" if final else "ffn",
    )(*args)


def _store_by_class(out_ref, scr_ref, val, dil):
    if dil == 1:
        out_ref[0] = val.astype(out_ref.dtype)
        return
    rows = val.shape[0] // dil
    for c in range(val.shape[1] // LANES):
        cols = slice(c * LANES, (c + 1) * LANES)
        scr_ref[c] = val[:, cols]
        for r in range(dil):
            out_ref[r, :, cols] = scr_ref[c, pl.ds(r, rows, stride=dil), :].astype(out_ref.dtype)


def _proj_attn_kernel(x_ref, g_ref, cos_ref, sin_ref, w_ref, *rest, dils):
    n_g = len(dils)
    outs, scr = rest[:3 * n_g], rest[3 * n_g:]
    gw = ATT_GROUP_WIDTH
    h = _rms(x_ref[...], g_ref[...]).astype(BF16)
    cos = cos_ref[...]
    sin = sin_ref[...]
    scale = ATT_HEAD_DIM ** -0.5 * math.log2(math.e)
    for gi, dil in enumerate(dils):
        part = lambda p: w_ref[:, (p * n_g + gi) * gw:(p * n_g + gi + 1) * gw]
        zq = _dot(h, part(0))
        zk = _dot(h, part(1))
        rq, rk = [], []
        for hd in range(ATT_HEADS):
            sl = slice(hd * ATT_HEAD_DIM, (hd + 1) * ATT_HEAD_DIM)
            xq = zq[:, sl]
            xk = zk[:, sl]
            rq.append((xq * cos + pltpu.roll(xq, ATT_HEAD_DIM // 2, 1) * sin) * scale)
            rk.append(xk * cos + pltpu.roll(xk, ATT_HEAD_DIM // 2, 1) * sin)
        q_ref, k_ref, v_ref = outs[3 * gi:3 * gi + 3]
        _store_by_class(q_ref, scr[0] if scr else None, jnp.concatenate(rq, axis=-1), dil)
        _store_by_class(k_ref, scr[1] if scr else None, jnp.concatenate(rk, axis=-1), dil)
        _store_by_class(v_ref, scr[2] if scr else None, _dot(h, part(2)), dil)


def _proj_attn(x, norm, cos2, sin2, w_in, layer, batch):
    n, d = x.shape
    seq = n // batch
    tm = min(TOKEN_TILE, seq)
    per_b = seq // tm
    gw = ATT_GROUP_WIDTH
    dils = tuple(dil for _, dil in ATT_GROUPS)
    row = lambda b, i: (b * per_b + i, 0)
    out_shape, out_specs = [], []
    for dil in dils:
        out_shape += [jax.ShapeDtypeStruct((batch, dil, seq // dil, gw), BF16)] * 3
        out_specs += [pl.BlockSpec((None, dil, tm // dil, gw), lambda b, i: (b, 0, i, 0))] * 3
    scratch = [pltpu.VMEM((gw // LANES, tm, LANES), F32)] * 3 if max(dils) > 1 else []
    outs = pl.pallas_call(
        functools.partial(_proj_attn_kernel, dils=dils),
        out_shape=tuple(out_shape),
        grid=(batch, per_b),
        in_specs=[
            pl.BlockSpec((tm, d), row),
            _layer_block(norm, layer),
            pl.BlockSpec((tm, ATT_HEAD_DIM), row),
            pl.BlockSpec((tm, ATT_HEAD_DIM), row),
            _layer_block(w_in, layer, (0, 3 * len(dils) * gw)),
        ],
        out_specs=tuple(out_specs),
        scratch_shapes=scratch,
        compiler_params=_params("parallel", "parallel"),
        name="proj_attn",
    )(x, norm, cos2, sin2, w_in)
    return [outs[3 * gi:3 * gi + 3] for gi in range(len(dils))]


def _proj_hgrn_kernel(x_ref, g_ref, lbraw_ref, w_ref, q_ref, lf_ref, i_ref, og_ref, *, layer):
    h = _rms(x_ref[...], g_ref[...]).astype(BF16)
    raw = lbraw_ref[...]
    p = jnp.exp(raw - jnp.max(raw, axis=0, keepdims=True))
    p = p / jnp.sum(p, axis=0, keepdims=True)
    lb = jnp.sum(p[:layer + 1], axis=0, keepdims=True) - p[0:1]

    hw = q_ref.shape[1]
    sec = lambda k: w_ref[:, k * hw:(k + 1) * hw]
    q_ref[...] = (_dot(h, sec(0)) * (HGRN_HEAD_DIM ** -0.5)).astype(BF16)
    f = lb + (1.0 - lb) * jax.nn.sigmoid(_dot(h, sec(1)))
    lf_ref[...] = jnp.log(f)
    i_ref[...] = _dot(h, sec(2)).astype(BF16)
    og = _dot(h, sec(3))
    og_ref[...] = (og * jax.nn.sigmoid(og)).astype(BF16)


def _proj_hgrn(x, norm, lb_raw, w_in, col_off, layer):
    n, d = x.shape
    tm = min(TOKEN_TILE, n)
    width = lb_raw.shape[1]
    row = lambda i: (i, 0)
    bf = jax.ShapeDtypeStruct((n, width), BF16)
    return pl.pallas_call(
        functools.partial(_proj_hgrn_kernel, layer=layer),
        out_shape=(bf, jax.ShapeDtypeStruct((n, width), F32), bf, bf),
        grid=(n // tm,),
        in_specs=[pl.BlockSpec((tm, d), row), _layer_block(norm, layer), _resident(lb_raw),
                  _layer_block(w_in, layer, (col_off, 4 * width))],
        out_specs=(pl.BlockSpec((tm, width), row),) * 4,
        compiler_params=_params("parallel"),
        name="proj_hgrn",
    )(x, norm, lb_raw, w_in)


def _attn_kernel(q_ref, kp_ref, kc_ref, vp_ref, vc_ref, o_ref, st_ref, *, nres, nblk):
    first = pl.program_id(2) == 0
    row = lax.broadcasted_iota(jnp.int32, (ATT_BLOCK, 2 * ATT_BLOCK), 0)
    col = lax.broadcasted_iota(jnp.int32, (ATT_BLOCK, 2 * ATT_BLOCK), 1)
    band = jnp.logical_and(col >= row, col <= row + ATT_BLOCK)
    low = jnp.where(first, ATT_BLOCK, 0)
    band_first = jnp.logical_and(col >= jnp.maximum(row, low), col <= row + ATT_BLOCK)
    ones = jnp.ones((2 * ATT_BLOCK, ATT_HEAD_DIM), BF16)
    lane = lax.broadcasted_iota(jnp.int32, (ATT_BLOCK, LANES), 1)
    for res in range(nres):
        for blk in range(nblk):
            rows = slice(blk * ATT_BLOCK, (blk + 1) * ATT_BLOCK)
            stats = jnp.ones((ATT_BLOCK, LANES), F32)
            for hd in range(ATT_HEADS):
                sl = slice(hd * ATT_HEAD_DIM, (hd + 1) * ATT_HEAD_DIM)
                if blk == 0:
                    k2 = jnp.concatenate([kp_ref[res, :, sl], kc_ref[res, 0:ATT_BLOCK, sl]], axis=0)
                    v2 = jnp.concatenate([vp_ref[res, :, sl], vc_ref[res, 0:ATT_BLOCK, sl]], axis=0)
                    mask = band_first
                else:
                    win = slice((blk - 1) * ATT_BLOCK, (blk + 1) * ATT_BLOCK)
                    k2 = kc_ref[res, win, sl]
                    v2 = vc_ref[res, win, sl]
                    mask = band
                s = jnp.where(mask, _dot_nt(q_ref[res, rows, sl], k2), NEG_BIG)
                m = jnp.max(s, axis=-1, keepdims=True)
                p = jnp.exp2(s - m).astype(BF16)
                o_den = _dot(p, jnp.concatenate([v2, ones], axis=1))
                den = o_den[:, ATT_HEAD_DIM:]
                o_ref[res, rows, sl] = o_den[:, :ATT_HEAD_DIM].astype(BF16)
                stats = jnp.where(lane == hd, m, jnp.where(lane == ATT_HEADS + hd, den, stats))
            st_ref[res, rows, :] = stats


def _attn_group(q, k, v):
    batch, dil, cls, gw = q.shape
    assert cls % ATT_BLOCK == 0
    nblk = min(ATT_CHAINS, cls // ATT_BLOCK)
    nres = max(1, min(dil, ATT_CHAINS // nblk))
    qb = nblk * ATT_BLOCK
    cur = lambda b, r, i: (b, r, i, 0)
    prev = lambda b, r, i: (b, r, jnp.maximum(i * nblk - 1, 0), 0)
    cur_spec = pl.BlockSpec((None, nres, qb, gw), cur)
    prev_spec = pl.BlockSpec((None, nres, ATT_BLOCK, gw), prev)
    return pl.pallas_call(
        functools.partial(_attn_kernel, nres=nres, nblk=nblk),
        out_shape=(jax.ShapeDtypeStruct((batch, dil, cls, gw), BF16),
                   jax.ShapeDtypeStruct((batch, dil, cls, LANES), F32)),
        grid=(batch, dil // nres, cls // qb),
        in_specs=[cur_spec, prev_spec, cur_spec, prev_spec, cur_spec],
        out_specs=(pl.BlockSpec((None, nres, qb, gw), cur),
                   pl.BlockSpec((None, nres, qb, LANES), cur)),
        compiler_params=_params("parallel", "parallel", "arbitrary"),
        name=f"attn_dil{dil}",
    )(q, k, k, v, v)


def _hgrn_decay_ahead(lf_ref, b_ref, kk_ref, reach_ref, slot, c):
    r_i = lax.broadcasted_iota(jnp.int32, (c, c), 0)
    c_i = lax.broadcasted_iota(jnp.int32, (c, c), 1)
    tri = jnp.where(c_i <= r_i, 1.0, 0.0).astype(BF16)
    reach = None
    for lo in range(0, lf_ref.shape[0], c):
        lf = lf_ref[lo:lo + c, :]
        lf_hi = lf.astype(BF16)
        lf_lo = (lf - lf_hi.astype(F32)).astype(BF16)
        b = _dot(tri, lf_hi) + _dot(tri, lf_lo)
        b_ref[slot, lo:lo + c, :] = b
        kk_ref[slot, lo:lo + c, :] = 1.0 - jnp.exp(lf)
        b_mid = b[c // 2 - 1:c // 2, :]
        r = jnp.max(jnp.maximum(b[0:1, :] - b_mid, b_mid - b[c - 1:c, :]))
        reach = r if reach is None else jnp.maximum(reach, r)
    reach_ref[slot] = reach


def _hgrn_scores_midref(q, kk, b, heads):
    c = b.shape[0]
    b_mid = b[c // 2 - 1:c // 2, :]
    b_last = b[c - 1:c, :]
    q_s = q * jnp.exp(b - b_mid)
    k_s = kk * jnp.exp(b_mid - b)
    q_in = (q_s * jnp.exp(b_mid)).astype(BF16)
    k_out = (k_s * jnp.exp(b_last - b_mid)).astype(BF16)
    q_s = q_s.astype(BF16)
    k_s = k_s.astype(BF16)
    r_i = lax.broadcasted_iota(jnp.int32, (c, c), 0)
    c_i = lax.broadcasted_iota(jnp.int32, (c, c), 1)
    causal = c_i <= r_i
    scores = []
    for hd in range(heads):
        sl = slice(hd * HGRN_HEAD_DIM, (hd + 1) * HGRN_HEAD_DIM)
        scores.append(jnp.where(causal, _dot_nt(q_s[:, sl], k_s[:, sl]), 0.0).astype(BF16))
    return scores, q_in, k_out


def _hgrn_scores_bounded(q, kk, b, heads):
    c, w = b.shape
    q_in = (q * jnp.exp(b)).astype(BF16)
    k_out = (kk * jnp.exp(b[c - 1:c, :] - b)).astype(BF16)
    r_i = lax.broadcasted_iota(jnp.int32, (c, c), 0)
    c_i = lax.broadcasted_iota(jnp.int32, (c, c), 1)
    rows = lax.broadcasted_iota(jnp.int32, (c, w), 0)

    levels = []
    half = SUBLANES
    while half < c:
        span = 2 * half
        pieces = [jnp.broadcast_to(b[s + half - 1:s + half, :], (span, w))
                  for s in range(0, c, span)]
        bref = pieces[0] if len(pieces) == 1 else jnp.concatenate(pieces, axis=0)
        upper = (rows & (span - 1)) >= half
        e = jnp.exp(jnp.where(upper, b - bref, bref - b))
        q_up = jnp.where(upper, q * e, 0.0).astype(BF16)
        k_lo = jnp.where(upper, 0.0, kk * e).astype(BF16)
        same = (r_i & ~(span - 1)) == (c_i & ~(span - 1))
        levels.append((q_up, k_lo, same))
        half = span

    nb = c // SUBLANES
    sub = lax.broadcasted_iota(jnp.int32, (nb, SUBLANES, w), 1)
    q3 = q.reshape(nb, SUBLANES, w)
    k3 = kk.reshape(nb, SUBLANES, w)
    b3 = b.reshape(nb, SUBLANES, w)
    near = []
    for dist in range(SUBLANES):
        if dist == 0:
            prod = q3 * k3
        else:
            k_r = pltpu.roll(k3, dist, 1)
            b_r = pltpu.roll(b3, dist, 1)
            prod = jnp.where(sub >= dist, q3 * k_r * jnp.exp(b3 - b_r), 0.0)
        near.append(prod.reshape(c, w))

    scores = []
    for hd in range(heads):
        sl = slice(hd * HGRN_HEAD_DIM, (hd + 1) * HGRN_HEAD_DIM)
        a = jnp.zeros((c, c), F32)
        for q_up, k_lo, same in levels:
            a = a + jnp.where(same, _dot_nt(q_up[:, sl], k_lo[:, sl]), 0.0)
        for dist in range(SUBLANES):
            col_val = jnp.sum(near[dist][:, sl], axis=-1, keepdims=True)
            a = a + jnp.where(c_i == r_i - dist, col_val, 0.0)
        scores.append(a.astype(BF16))
    return scores, q_in, k_out


def _hgrn_chunk(q_ref, i_ref, og_ref, hn_ref, o_ref, st_ref, b, kk, rows, score_fn, heads):
    c = b.shape[0]
    q = q_ref[rows, :].astype(F32)
    v_bf = i_ref[rows, :]
    scores, q_in, k_out = score_fn(q, kk, b, heads)
    st_decay = jnp.exp(b[c - 1:c, :])
    gain = hn_ref[...]
    for hd in range(heads):
        sl = slice(hd * HGRN_HEAD_DIM, (hd + 1) * HGRN_HEAD_DIM)
        v_h = v_bf[:, sl]
        st_h = st_ref[hd]
        o_h = _dot(scores[hd], v_h) + _dot_nt(q_in[:, sl], st_h.astype(BF16))
        st_ref[hd] = st_h * st_decay[:, sl] + _dot_tn(v_h, k_out[:, sl])
        o_n = _rms(o_h, gain[:, sl])
        o_ref[rows, sl] = (o_n * og_ref[rows, sl].astype(F32)).astype(BF16)


def _hgrn_kernel(q_ref, lf_ref, i_ref, og_ref, hn_ref, o_ref,
                 st_ref, b_ref, kk_ref, reach_ref, *, chunk, heads):
    t = pl.program_id(1)
    ahead = t % 2
    ready = 1 - ahead

    @pl.when(t == 0)
    def _():
        st_ref[...] = jnp.zeros_like(st_ref)
        reach_ref[ready] = 0.0
        _hgrn_decay_ahead(lf_ref, b_ref, kk_ref, reach_ref, ahead, chunk)

    midref_ok = reach_ref[ready] < HGRN_SAFE_EXPONENT
    for take, score_fn in ((midref_ok, _hgrn_scores_midref),
                           (jnp.logical_not(midref_ok), _hgrn_scores_bounded)):
        @pl.when(jnp.logical_and(t > 0, take))
        def _():
            for lo in range(0, q_ref.shape[0], chunk):
                rows = slice(lo, lo + chunk)
                _hgrn_chunk(q_ref, i_ref, og_ref, hn_ref, o_ref, st_ref,
                            b_ref[ready, rows, :], kk_ref[ready, rows, :], rows, score_fn, heads)
            _hgrn_decay_ahead(lf_ref, b_ref, kk_ref, reach_ref, ahead, chunk)


def _hgrn(q, lf, i, og, head_gain, layer, batch):
    n, w = q.shape
    seq = n // batch
    c = min(HGRN_CHUNK, seq)
    heads = w // HGRN_HEAD_DIM
    rows = min(HGRN_STEP_CHUNKS * c, seq)
    ns = seq // rows
    behind = lambda b, t: (b * ns + jnp.maximum(t - 1, 0), 0)
    upcoming = lambda b, t: (b * ns + jnp.minimum(t, ns - 1), 0)
    spec = pl.BlockSpec((rows, w), behind)
    return pl.pallas_call(
        functools.partial(_hgrn_kernel, chunk=c, heads=heads),
        out_shape=jax.ShapeDtypeStruct((n, w), BF16),
        grid=(batch, ns + 1),
        in_specs=[spec, pl.BlockSpec((rows, w), upcoming), spec, spec,
                  _layer_block(head_gain, layer)],
        out_specs=spec,
        scratch_shapes=[pltpu.VMEM((heads, HGRN_HEAD_DIM, HGRN_HEAD_DIM), F32),
                        pltpu.VMEM((2, rows, w), F32),
                        pltpu.VMEM((2, rows, w), F32),
                        pltpu.SMEM((2,), F32)],
        compiler_params=_params("parallel", "arbitrary"),
        name="hgrn",
    )(q, lf, i, og, head_gain)


def _load_by_token(src_ref, scr_ref):
    dil, rows, width = src_ref.shape
    if dil == 1:
        return src_ref[0].astype(F32)
    parts = []
    for c in range(width // LANES):
        cols = slice(c * LANES, (c + 1) * LANES)
        for r in range(dil):
            scr_ref[c, pl.ds(r, rows, stride=dil), :] = src_ref[r, :, cols].astype(F32)
        parts.append(scr_ref[c])
    return parts[0] if len(parts) == 1 else jnp.concatenate(parts, axis=-1)


def _mix_kernel(x_ref, g_ref, o0_ref, o1_ref, o2_ref, s0_ref, s1_ref, s2_ref, ob_ref,
                wg_ref, wa_ref, wb_ref, wo_ref, out_ref, *scr):
    x = x_ref[...]
    d = x.shape[1]
    h = _rms(x, g_ref[...]).astype(BF16)
    gate_a = jax.nn.sigmoid(_dot(h, wg_ref[:, :d]))
    gate_b = jax.nn.sigmoid(_dot(h, wg_ref[:, d:]))

    scr = list(scr)
    take = lambda ref: _load_by_token(ref, scr.pop(0) if ref.shape[0] > 1 else None)
    o0, o1, o2 = take(o0_ref), take(o1_ref), take(o2_ref)
    s0, s1, s2 = take(s0_ref), take(s1_ref), take(s2_ref)
    head_lane = lax.broadcasted_iota(jnp.int32, s0.shape, 1) < ATT_HEADS
    lse = [s + jnp.log2(jnp.where(head_lane, pltpu.roll(s, LANES - ATT_HEADS, 1), 1.0))
           for s in (s0, s1, s2)]
    mx = jnp.maximum(jnp.maximum(lse[0], lse[1]), lse[2])
    inv = 1.0 / (jnp.exp2(lse[0] - mx) + jnp.exp2(lse[1] - mx) + jnp.exp2(lse[2] - mx))
    w0, w1, w2 = (jnp.exp2(s - mx) * inv for s in (s0, s1, s2))
    parts = []
    for hd in range(ATT_HEADS):
        sl = slice(hd * ATT_HEAD_DIM, (hd + 1) * ATT_HEAD_DIM)
        parts.append(o0[:, sl] * w0[:, hd:hd + 1] + o1[:, sl] * w1[:, hd:hd + 1]
                     + o2[:, sl] * w2[:, hd:hd + 1])
    att = jnp.concatenate(parts, axis=-1).astype(BF16)

    y_a = _dot(att, wa_ref[...])
    y_b = _dot(ob_ref[...], wb_ref[...])
    merged = (gate_a * y_a + gate_b * y_b).astype(BF16)
    out_ref[...] = x + _dot(merged, wo_ref[...])


def _mix(x, norm, o_groups, s_groups, ob, w_in, gate_off, wa, wb, wo, layer):
    n, d = x.shape
    batch = o_groups[0].shape[0]
    seq = n // batch
    tm = min(TOKEN_TILE, seq)
    per_b = seq // tm
    row = lambda b, i: (b * per_b + i, 0)
    rows = lambda a: pl.BlockSpec((tm, a.shape[1]), row)
    by_class = lambda a: pl.BlockSpec((None, a.shape[1], tm // a.shape[1], a.shape[3]),
                                      lambda b, i: (b, 0, i, 0))
    grouped = [*o_groups, *s_groups]
    args = [x, norm, *grouped, ob, w_in, wa, wb, wo]
    in_specs = ([rows(x), _layer_block(norm, layer)] + [by_class(a) for a in grouped]
                + [rows(ob), _layer_block(w_in, layer, (gate_off, 2 * d)),
                   _layer_block(wa, layer), _layer_block(wb, layer), _layer_block(wo, layer)])
    scratch = [pltpu.VMEM((a.shape[3] // LANES, tm, LANES), F32)
               for a in grouped if a.shape[1] > 1]
    return pl.pallas_call(
        _mix_kernel,
        out_shape=jax.ShapeDtypeStruct((n, d), F32),
        grid=(batch, per_b),
        in_specs=in_specs,
        out_specs=pl.BlockSpec((tm, d), row),
        scratch_shapes=scratch,
        compiler_params=_params("parallel", "parallel"),
        name="mix_out",
    )(*args)


def _mem_kv_kernel(m_ref, g_ref, wk_ref, wv_ref, k_ref, v_ref):
    h = _rms(m_ref[...], g_ref[...]).astype(BF16)
    k_ref[...] = _dot(h, wk_ref[...]).astype(BF16)
    v_ref[...] = _dot(h, wv_ref[...]).astype(BF16)


def _mem_kv(mem, norm, wkv, layer):
    n, d = mem.shape
    tm = min(TOKEN_TILE, n)
    width = wkv.shape[2] // 2
    row = lambda i: (i, 0)
    out = jax.ShapeDtypeStruct((n, width), BF16)
    return pl.pallas_call(
        _mem_kv_kernel,
        out_shape=(out, out),
        grid=(n // tm,),
        in_specs=[pl.BlockSpec((tm, d), row), _layer_block(norm, layer),
                  _layer_block(wkv, layer, (0, width)), _layer_block(wkv, layer, (width, width))],
        out_specs=(pl.BlockSpec((tm, width), row),) * 2,
        compiler_params=_params("parallel"),
        name="mem_kv",
    )(mem, norm, wkv, wkv)


def _xattn_kernel(x_ref, g_ref, k_ref, v_ref, wq_ref, wo_ref, out_ref):
    x = x_ref[...]
    h = _rms(x, g_ref[...]).astype(BF16)
    dh = wq_ref.shape[1] // X_HEADS
    q = (_dot(h, wq_ref[...]) * (dh ** -0.5)).astype(BF16)
    outs = []
    for hd in range(X_HEADS):
        sl = slice(hd * dh, (hd + 1) * dh)
        s = _dot_nt(q[:, sl], k_ref[:, sl])
        m = jnp.max(s, axis=-1, keepdims=True)
        p = jnp.exp(s - m)
        den = jnp.sum(p, axis=-1, keepdims=True)
        outs.append(_dot(p.astype(BF16), v_ref[:, sl]) / den)
    o = jnp.concatenate(outs, axis=-1).astype(BF16)
    out_ref[...] = x + _dot(o, wo_ref[...])


def _xattn(x, norm, k, v, wq, wo, layer, batch):
    n, d = x.shape
    seq = n // batch
    tm = min(TOKEN_TILE, seq)
    n_mem = k.shape[0] // batch
    per_b = seq // tm
    row = lambda b, i: (b * per_b + i, 0)
    kv = pl.BlockSpec((n_mem, k.shape[1]), lambda b, i: (b, 0))
    return pl.pallas_call(
        _xattn_kernel,
        out_shape=jax.ShapeDtypeStruct((n, d), F32),
        grid=(batch, per_b),
        in_specs=[pl.BlockSpec((tm, d), row), _layer_block(norm, layer), kv, kv,
                  _layer_block(wq, layer), _layer_block(wo, layer)],
        out_specs=pl.BlockSpec((tm, d), row),
        compiler_params=_params("parallel", "parallel"),
        name="xattn",
    )(x, norm, k, v, wq, wo)


def kernel(x, mem, positions, ffn1_norm, ffn1_w_gu, ffn1_w_down, mix_norm, w_in,
           hgrn_lower_bounds, hgrn_head_norm, w_att_branch, w_hgrn_branch, w_mix_out,
           xattn_norm, mem_norm, xattn_wq, xattn_wkv, xattn_wo,
           ffn2_norm, ffn2_w_gu, ffn2_w_down, final_norm):
    batch, seq, d = x.shape
    depth = w_in.shape[0]
    n = batch * seq
    att_w = len(ATT_GROUPS) * ATT_GROUP_WIDTH
    hg_w = hgrn_head_norm.shape[1]
    hgrn_off = 3 * att_w
    gate_off = hgrn_off + 4 * hg_w

    xf = x.reshape(n, d)
    memf = mem.reshape(batch * mem.shape[1], d)
    cos2, sin2 = _rope_tables(positions.reshape(n, 1).astype(F32))

    bf = lambda a: a.astype(BF16)
    gain = lambda a: a.reshape(a.shape[0], 1, a.shape[1])
    w_in, w_att_branch, w_hgrn_branch, w_mix_out = map(
        bf, (w_in, w_att_branch, w_hgrn_branch, w_mix_out))
    ffn1_w_gu, ffn1_w_down, ffn2_w_gu, ffn2_w_down = map(
        bf, (ffn1_w_gu, ffn1_w_down, ffn2_w_gu, ffn2_w_down))
    xattn_wq, xattn_wkv, xattn_wo = map(bf, (xattn_wq, xattn_wkv, xattn_wo))
    ffn1_norm, mix_norm, hgrn_head_norm, xattn_norm, mem_norm, ffn2_norm = map(
        gain, (ffn1_norm, mix_norm, hgrn_head_norm, xattn_norm, mem_norm, ffn2_norm))

    for l in range(depth):
        xf = _ffn(xf, ffn1_norm, ffn1_w_gu, ffn1_w_down, l)

        o_groups, s_groups = [], []
        for q_a, k_a, v_a in _proj_attn(xf, mix_norm, cos2, sin2, w_in, l, batch):
            o_g, s_g = _attn_group(q_a, k_a, v_a)
            o_groups.append(o_g)
            s_groups.append(s_g)
        q_b, lf_b, i_b, og_b = _proj_hgrn(xf, mix_norm, hgrn_lower_bounds, w_in, hgrn_off, l)
        o_b = _hgrn(q_b, lf_b, i_b, og_b, hgrn_head_norm, l, batch)
        xf = _mix(xf, mix_norm, o_groups, s_groups, o_b, w_in, gate_off,
                  w_att_branch, w_hgrn_branch, w_mix_out, l)

        k_x, v_x = _mem_kv(memf, mem_norm, xattn_wkv, l)
        xf = _xattn(xf, xattn_norm, k_x, v_x, xattn_wq, xattn_wo, l, batch)

        final_g = final_norm.reshape(1, -1) if l == depth - 1 else None
        xf = _ffn(xf, ffn2_norm, ffn2_w_gu, ffn2_w_down, l, final_g=final_g)

    return xf.reshape(batch, seq, d)
```

```python
import functools
import math

import jax
import jax.numpy as jnp
from jax import lax
from jax.experimental import pallas as pl
from jax.experimental.pallas import tpu as pltpu

F32 = jnp.float32
BF16 = jnp.bfloat16

ATT_GROUPS = ((128, 1), (512, 4), (2048, 16))
ATT_HEADS = 4
ATT_HEAD_DIM = 128
ATT_GROUP_WIDTH = ATT_HEADS * ATT_HEAD_DIM
HGRN_HEAD_DIM = 128
X_HEADS = 4
ROPE_THETA = 10000.0
EPS = 1e-6

LANES = 128
SUBLANES = 8
MXU_TILE = 256
VMEM_LIMIT_BYTES = 56 * 1024 * 1024

TOKEN_TILE = 1024
FFN_CHUNK_TILES = 3
MXU_PERMUTE_MIN_STRIDE = 8
ATT_BLOCK = 128
ATT_CHAINS = 8
HGRN_CHUNK = 128
HGRN_STEP_CHUNKS = 2
HGRN_SAFE_EXPONENT = 80.0
NEG_BIG = -1e30


def _params(*sem):
    return pltpu.CompilerParams(dimension_semantics=sem,
                                vmem_limit_bytes=VMEM_LIMIT_BYTES)


def _rms(x, g):
    ms = jnp.mean(x * x, axis=-1, keepdims=True)
    return x * lax.rsqrt(ms + EPS) * g


def _dot(a, b):
    return jnp.dot(a, b, preferred_element_type=F32)


def _dot_nt(a, b):
    return lax.dot_general(a, b, (((1,), (1,)), ((), ())), preferred_element_type=F32)


def _dot_tn(a, b):
    return lax.dot_general(a, b, (((0,), (0,)), ((), ())), preferred_element_type=F32)


def _resident(a):
    return pl.BlockSpec(a.shape, lambda *_: (0,) * a.ndim, pipeline_mode=pl.Buffered(1))


def _layer_block(a, layer):
    _, rows, width = a.shape
    return pl.BlockSpec((None, rows, width), lambda *_: (layer, 0, 0),
                        pipeline_mode=pl.Buffered(1))


def _col_block(a, off, size):
    return pl.BlockSpec((pl.Element(a.shape[0]), pl.Element(size)), lambda *_: (0, off),
                        pipeline_mode=pl.Buffered(1))


def _pallas(body, *, name, grid, sem, in_specs, args, out_shape, out_specs, scratch=(),
            casts=(), step_of=None):
    single = not isinstance(out_shape, (tuple, list))
    out_shape = [out_shape] if single else list(out_shape)
    out_specs = [out_specs] if single else list(out_specs)
    n_in, n_out, n_cast = len(in_specs), len(out_shape), len(casts)
    steps = math.prod(grid)
    cast_in, cast_out, cast_shape = [], [], []
    for a, layer in casts:
        _, rows, cols = a.shape
        chunk = rows // steps
        assert chunk * steps == rows and chunk % (2 * SUBLANES) == 0
        cast_in.append(pl.BlockSpec((None, chunk, cols),
                                    lambda *g, layer=layer: (layer, step_of(*g), 0)))
        cast_out.append(pl.BlockSpec((chunk, cols), lambda *g: (step_of(*g), 0)))
        cast_shape.append(jax.ShapeDtypeStruct((rows, cols), BF16))

    def hosted(*refs):
        ins = refs[:n_in]
        srcs = refs[n_in:n_in + n_cast]
        outs = refs[n_in + n_cast:n_in + n_cast + n_out]
        dsts = refs[n_in + n_cast + n_out:n_in + 2 * n_cast + n_out]
        for src, dst in zip(srcs, dsts):
            dst[...] = src[...].astype(BF16)
        body(*ins, *outs, *refs[n_in + 2 * n_cast + n_out:])

    res = pl.pallas_call(
        hosted,
        out_shape=tuple(out_shape + cast_shape),
        grid=grid,
        in_specs=list(in_specs) + cast_in,
        out_specs=tuple(out_specs + cast_out),
        scratch_shapes=list(scratch),
        compiler_params=_params(*sem),
        name=name,
    )(*args, *[a for a, _ in casts])
    main = res[:n_out]
    return (main[0] if single else tuple(main)), list(res[n_out:])


def _rope_table_kernel(pos_ref, cos_ref, sin_ref):
    half = ATT_HEAD_DIM // 2
    lane = lax.broadcasted_iota(jnp.int32, (1, ATT_HEAD_DIM), 1)
    idx = jnp.where(lane >= half, lane - half, lane).astype(F32)
    inv_freq = jnp.exp(idx * (-2.0 * math.log(ROPE_THETA) / ATT_HEAD_DIM))
    ang = pos_ref[...] * inv_freq
    sign = jnp.where(lane >= half, 1.0, -1.0)
    cos_ref[...] = jnp.cos(ang)
    sin_ref[...] = jnp.sin(ang) * sign


def _rope_tables(pos_col):
    n = pos_col.shape[0]
    tm = min(n, 2048)
    out = jax.ShapeDtypeStruct((n, ATT_HEAD_DIM), F32)
    return pl.pallas_call(
        _rope_table_kernel,
        out_shape=(out, out),
        grid=(n // tm,),
        in_specs=[pl.BlockSpec((tm, 1), lambda i: (i, 0))],
        out_specs=(pl.BlockSpec((tm, ATT_HEAD_DIM), lambda i: (i, 0)),) * 2,
        compiler_params=_params("parallel"),
        name="rope_tables",
    )(pos_col)


def _ffn_chunks(d_ff):
    assert d_ff % MXU_TILE == 0
    tiles = d_ff // MXU_TILE
    n_chunks = -(-tiles // FFN_CHUNK_TILES)
    bounds = [MXU_TILE * (tiles * k // n_chunks) for k in range(n_chunks + 1)]
    return list(zip(bounds[:-1], bounds[1:]))


def _ffn_kernel(*refs, final):
    if final:
        x_ref, g_ref, wg_ref, wu_ref, wd_ref, fg_ref, o_ref = refs
    else:
        x_ref, g_ref, wg_ref, wu_ref, wd_ref, o_ref = refs
    x = x_ref[...]
    h = _rms(x, g_ref[...]).astype(BF16)
    acc = None
    for lo, hi in _ffn_chunks(wg_ref.shape[1]):
        gate = _dot(h, wg_ref[:, lo:hi])
        up = _dot(h, wu_ref[:, lo:hi])
        act = (gate * jax.nn.sigmoid(gate) * up).astype(BF16)
        part = _dot(act, wd_ref[lo:hi, :])
        acc = part if acc is None else acc + part
    y = x + 0.5 * acc
    if final:
        y = _rms(y, fg_ref[...])
    o_ref[...] = y


def _ffn(x, norm, w_gu, w_down, layer, final_g=None, casts=()):
    n, d = x.shape
    d_ff = w_down.shape[0]
    tm = min(TOKEN_TILE, n)
    final = final_g is not None
    row = pl.BlockSpec((tm, d), lambda i: (i, 0))
    in_specs = [row, _layer_block(norm, layer), _col_block(w_gu, 0, d_ff),
                _col_block(w_gu, d_ff, d_ff), _resident(w_down)]
    args = [x, norm, w_gu, w_gu, w_down]
    if final:
        in_specs.append(_resident(final_g))
        args.append(final_g)
    return _pallas(
        functools.partial(_ffn_kernel, final=final),
        name="ffn_final" if final else "ffn",
        grid=(n // tm,), sem=("parallel",),
        in_specs=in_specs, args=args,
        out_shape=jax.ShapeDtypeStruct((n, d), F32), out_specs=row,
        casts=casts, step_of=lambda i: i,
    )


def _class_permutation(dil):
    per = MXU_TILE // dil
    i = lax.broadcasted_iota(jnp.int32, (MXU_TILE, MXU_TILE), 0)
    j = lax.broadcasted_iota(jnp.int32, (MXU_TILE, MXU_TILE), 1)
    return jnp.where(j == (i % per) * dil + i // per, 1.0, 0.0).astype(BF16)


def _store_by_class(out_ref, scr_ref, val, dil):
    if dil == 1:
        out_ref[0] = val.astype(out_ref.dtype)
        return
    if dil < MXU_PERMUTE_MIN_STRIDE:
        rows = val.shape[0] // dil
        for c in range(val.shape[1] // LANES):
            cols = slice(c * LANES, (c + 1) * LANES)
            scr_ref[c] = val[:, cols]
            for r in range(dil):
                out_ref[r, :, cols] = (
                    scr_ref[c, pl.ds(r, rows, stride=dil), :].astype(out_ref.dtype))
        return
    val = val.astype(out_ref.dtype)
    perm = _class_permutation(dil)
    per = MXU_TILE // dil
    for s in range(val.shape[0] // MXU_TILE):
        y = _dot(perm, val[s * MXU_TILE:(s + 1) * MXU_TILE, :]).astype(out_ref.dtype)
        for r in range(dil):
            out_ref[r, s * per:(s + 1) * per, :] = y[r * per:(r + 1) * per, :]


def _proj_attn_kernel(x_ref, g_ref, cos_ref, sin_ref, w_ref, *rest, dils):
    n_g = len(dils)
    outs, scr = rest[:3 * n_g], rest[3 * n_g:]
    scr = scr if scr else (None,) * 3
    gw = ATT_GROUP_WIDTH
    h = _rms(x_ref[...], g_ref[...]).astype(BF16)
    cos = cos_ref[...]
    sin = sin_ref[...]
    scale = ATT_HEAD_DIM ** -0.5 * math.log2(math.e)
    for gi, dil in enumerate(dils):
        part = lambda p: w_ref[:, (p * n_g + gi) * gw:(p * n_g + gi + 1) * gw]
        zq = _dot(h, part(0))
        zk = _dot(h, part(1))
        rq, rk = [], []
        for hd in range(ATT_HEADS):
            sl = slice(hd * ATT_HEAD_DIM, (hd + 1) * ATT_HEAD_DIM)
            xq = zq[:, sl]
            xk = zk[:, sl]
            rq.append((xq * cos + pltpu.roll(xq, ATT_HEAD_DIM // 2, 1) * sin) * scale)
            rk.append(xk * cos + pltpu.roll(xk, ATT_HEAD_DIM // 2, 1) * sin)
        q_ref, k_ref, v_ref = outs[3 * gi:3 * gi + 3]
        _store_by_class(q_ref, scr[0], jnp.concatenate(rq, axis=-1), dil)
        _store_by_class(k_ref, scr[1], jnp.concatenate(rk, axis=-1), dil)
        _store_by_class(v_ref, scr[2], _dot(h, part(2)), dil)


def _proj_attn(x, norm, cos2, sin2, w_in, layer, batch, casts=()):
    n, d = x.shape
    seq = n // batch
    tm = min(TOKEN_TILE, seq)
    per_b = seq // tm
    gw = ATT_GROUP_WIDTH
    dils = tuple(dil for _, dil in ATT_GROUPS)
    row = lambda b, i: (b * per_b + i, 0)
    out_shape, out_specs = [], []
    for dil in dils:
        out_shape += [jax.ShapeDtypeStruct((batch, dil, seq // dil, gw), BF16)] * 3
        out_specs += [pl.BlockSpec((None, dil, tm // dil, gw), lambda b, i: (b, 0, i, 0))] * 3
    strided = any(1 < dil < MXU_PERMUTE_MIN_STRIDE for dil in dils)
    scratch = [pltpu.VMEM((gw // LANES, tm, LANES), F32)] * 3 if strided else []
    outs, cast_res = _pallas(
        functools.partial(_proj_attn_kernel, dils=dils),
        name="proj_attn",
        grid=(batch, per_b), sem=("parallel", "parallel"),
        in_specs=[
            pl.BlockSpec((tm, d), row),
            _layer_block(norm, layer),
            pl.BlockSpec((tm, ATT_HEAD_DIM), row),
            pl.BlockSpec((tm, ATT_HEAD_DIM), row),
            _col_block(w_in, 0, 3 * len(dils) * gw),
        ],
        args=[x, norm, cos2, sin2, w_in],
        out_shape=out_shape, out_specs=out_specs, scratch=scratch,
        casts=casts, step_of=lambda b, i: b * per_b + i,
    )
    return [outs[3 * gi:3 * gi + 3] for gi in range(len(dils))], cast_res


def _proj_hgrn_kernel(x_ref, g_ref, lbraw_ref, w_ref, q_ref, lf_ref, i_ref, og_ref, *, layer):
    h = _rms(x_ref[...], g_ref[...]).astype(BF16)
    raw = lbraw_ref[...]
    p = jnp.exp(raw - jnp.max(raw, axis=0, keepdims=True))
    p = p / jnp.sum(p, axis=0, keepdims=True)
    lb = jnp.sum(p[:layer + 1], axis=0, keepdims=True) - p[0:1]

    hw = q_ref.shape[1]
    sec = lambda k: w_ref[:, k * hw:(k + 1) * hw]
    q_ref[...] = (_dot(h, sec(0)) * (HGRN_HEAD_DIM ** -0.5)).astype(BF16)
    f = lb + (1.0 - lb) * jax.nn.sigmoid(_dot(h, sec(1)))
    lf_ref[...] = jnp.log(f)
    i_ref[...] = _dot(h, sec(2)).astype(BF16)
    og = _dot(h, sec(3))
    og_ref[...] = (og * jax.nn.sigmoid(og)).astype(BF16)


def _proj_hgrn(x, norm, lb_raw, w_in, col_off, layer, casts=()):
    n, d = x.shape
    tm = min(TOKEN_TILE, n)
    width = lb_raw.shape[1]
    row = lambda i: (i, 0)
    bf = jax.ShapeDtypeStruct((n, width), BF16)
    return _pallas(
        functools.partial(_proj_hgrn_kernel, layer=layer),
        name="proj_hgrn",
        grid=(n // tm,), sem=("parallel",),
        in_specs=[pl.BlockSpec((tm, d), row), _layer_block(norm, layer), _resident(lb_raw),
                  _col_block(w_in, col_off, 4 * width)],
        args=[x, norm, lb_raw, w_in],
        out_shape=(bf, jax.ShapeDtypeStruct((n, width), F32), bf, bf),
        out_specs=(pl.BlockSpec((tm, width), row),) * 4,
        casts=casts, step_of=lambda i: i,
    )


def _attn_kernel(q_ref, kp_ref, kc_ref, vp_ref, vc_ref, o_ref, st_ref, *, nres, nblk):
    first = pl.program_id(2) == 0
    row = lax.broadcasted_iota(jnp.int32, (ATT_BLOCK, 2 * ATT_BLOCK), 0)
    col = lax.broadcasted_iota(jnp.int32, (ATT_BLOCK, 2 * ATT_BLOCK), 1)
    band = jnp.logical_and(col >= row, col <= row + ATT_BLOCK)
    low = jnp.where(first, ATT_BLOCK, 0)
    band_first = jnp.logical_and(col >= jnp.maximum(row, low), col <= row + ATT_BLOCK)
    ones = jnp.ones((2 * ATT_BLOCK, ATT_HEAD_DIM), BF16)
    lane = lax.broadcasted_iota(jnp.int32, (ATT_BLOCK, LANES), 1)
    for res in range(nres):
        for blk in range(nblk):
            rows = slice(blk * ATT_BLOCK, (blk + 1) * ATT_BLOCK)
            stats = jnp.ones((ATT_BLOCK, LANES), F32)
            for hd in range(ATT_HEADS):
                sl = slice(hd * ATT_HEAD_DIM, (hd + 1) * ATT_HEAD_DIM)
                if blk == 0:
                    k2 = jnp.concatenate([kp_ref[res, :, sl], kc_ref[res, 0:ATT_BLOCK, sl]], axis=0)
                    v2 = jnp.concatenate([vp_ref[res, :, sl], vc_ref[res, 0:ATT_BLOCK, sl]], axis=0)
                    mask = band_first
                else:
                    win = slice((blk - 1) * ATT_BLOCK, (blk + 1) * ATT_BLOCK)
                    k2 = kc_ref[res, win, sl]
                    v2 = vc_ref[res, win, sl]
                    mask = band
                s = jnp.where(mask, _dot_nt(q_ref[res, rows, sl], k2), NEG_BIG)
                m = jnp.max(s, axis=-1, keepdims=True)
                p = jnp.exp2(s - m).astype(BF16)
                o_den = _dot(p, jnp.concatenate([v2, ones], axis=1))
                den = o_den[:, ATT_HEAD_DIM:]
                o_ref[res, rows, sl] = o_den[:, :ATT_HEAD_DIM].astype(BF16)
                stats = jnp.where(lane == hd, m, jnp.where(lane == ATT_HEADS + hd, den, stats))
            st_ref[res, rows, :] = stats


def _attn_group(q, k, v):
    batch, dil, cls, gw = q.shape
    assert cls % ATT_BLOCK == 0
    nblk = min(ATT_CHAINS, cls // ATT_BLOCK)
    nres = max(1, min(dil, ATT_CHAINS // nblk))
    qb = nblk * ATT_BLOCK
    cur = lambda b, r, i: (b, r, i, 0)
    prev = lambda b, r, i: (b, r, jnp.maximum(i * nblk - 1, 0), 0)
    cur_spec = pl.BlockSpec((None, nres, qb, gw), cur)
    prev_spec = pl.BlockSpec((None, nres, ATT_BLOCK, gw), prev)
    return pl.pallas_call(
        functools.partial(_attn_kernel, nres=nres, nblk=nblk),
        out_shape=(jax.ShapeDtypeStruct((batch, dil, cls, gw), BF16),
                   jax.ShapeDtypeStruct((batch, dil, cls, LANES), F32)),
        grid=(batch, dil // nres, cls // qb),
        in_specs=[cur_spec, prev_spec, cur_spec, prev_spec, cur_spec],
        out_specs=(pl.BlockSpec((None, nres, qb, gw), cur),
                   pl.BlockSpec((None, nres, qb, LANES), cur)),
        compiler_params=_params("parallel", "parallel", "arbitrary"),
        name=f"attn_dil{dil}",
    )(q, k, k, v, v)


def _hgrn_decay_ahead(lf_ref, b_ref, kk_ref, reach_ref, slot, c):
    r_i = lax.broadcasted_iota(jnp.int32, (c, c), 0)
    c_i = lax.broadcasted_iota(jnp.int32, (c, c), 1)
    tri = jnp.where(c_i <= r_i, 1.0, 0.0).astype(BF16)
    reach = None
    for lo in range(0, lf_ref.shape[0], c):
        lf = lf_ref[lo:lo + c, :]
        lf_hi = lf.astype(BF16)
        lf_lo = (lf - lf_hi.astype(F32)).astype(BF16)
        b = _dot(tri, lf_hi) + _dot(tri, lf_lo)
        b_ref[slot, lo:lo + c, :] = b
        kk_ref[slot, lo:lo + c, :] = 1.0 - jnp.exp(lf)
        b_mid = b[c // 2 - 1:c // 2, :]
        r = jnp.max(jnp.maximum(b[0:1, :] - b_mid, b_mid - b[c - 1:c, :]))
        reach = r if reach is None else jnp.maximum(reach, r)
    reach_ref[slot] = reach


def _hgrn_scores_midref(q, kk, b, heads):
    c = b.shape[0]
    b_mid = b[c // 2 - 1:c // 2, :]
    b_last = b[c - 1:c, :]
    q_s = q * jnp.exp(b - b_mid)
    k_s = kk * jnp.exp(b_mid - b)
    q_in = (q_s * jnp.exp(b_mid)).astype(BF16)
    k_out = (k_s * jnp.exp(b_last - b_mid)).astype(BF16)
    q_s = q_s.astype(BF16)
    k_s = k_s.astype(BF16)
    r_i = lax.broadcasted_iota(jnp.int32, (c, c), 0)
    c_i = lax.broadcasted_iota(jnp.int32, (c, c), 1)
    causal = c_i <= r_i
    scores = []
    for hd in range(heads):
        sl = slice(hd * HGRN_HEAD_DIM, (hd + 1) * HGRN_HEAD_DIM)
        scores.append(jnp.where(causal, _dot_nt(q_s[:, sl], k_s[:, sl]), 0.0).astype(BF16))
    return scores, q_in, k_out


def _hgrn_scores_bounded(q, kk, b, heads):
    c, w = b.shape
    q_in = (q * jnp.exp(b)).astype(BF16)
    k_out = (kk * jnp.exp(b[c - 1:c, :] - b)).astype(BF16)
    r_i = lax.broadcasted_iota(jnp.int32, (c, c), 0)
    c_i = lax.broadcasted_iota(jnp.int32, (c, c), 1)
    rows = lax.broadcasted_iota(jnp.int32, (c, w), 0)

    levels = []
    half = SUBLANES
    while half < c:
        span = 2 * half
        pieces = [jnp.broadcast_to(b[s + half - 1:s + half, :], (span, w))
                  for s in range(0, c, span)]
        bref = pieces[0] if len(pieces) == 1 else jnp.concatenate(pieces, axis=0)
        upper = (rows & (span - 1)) >= half
        e = jnp.exp(jnp.where(upper, b - bref, bref - b))
        q_up = jnp.where(upper, q * e, 0.0).astype(BF16)
        k_lo = jnp.where(upper, 0.0, kk * e).astype(BF16)
        same = (r_i & ~(span - 1)) == (c_i & ~(span - 1))
        levels.append((q_up, k_lo, same))
        half = span

    nb = c // SUBLANES
    sub = lax.broadcasted_iota(jnp.int32, (nb, SUBLANES, w), 1)
    q3 = q.reshape(nb, SUBLANES, w)
    k3 = kk.reshape(nb, SUBLANES, w)
    b3 = b.reshape(nb, SUBLANES, w)
    near = []
    for dist in range(SUBLANES):
        if dist == 0:
            prod = q3 * k3
        else:
            k_r = pltpu.roll(k3, dist, 1)
            b_r = pltpu.roll(b3, dist, 1)
            prod = jnp.where(sub >= dist, q3 * k_r * jnp.exp(b3 - b_r), 0.0)
        near.append(prod.reshape(c, w))

    scores = []
    for hd in range(heads):
        sl = slice(hd * HGRN_HEAD_DIM, (hd + 1) * HGRN_HEAD_DIM)
        a = jnp.zeros((c, c), F32)
        for q_up, k_lo, same in levels:
            a = a + jnp.where(same, _dot_nt(q_up[:, sl], k_lo[:, sl]), 0.0)
        for dist in range(SUBLANES):
            col_val = jnp.sum(near[dist][:, sl], axis=-1, keepdims=True)
            a = a + jnp.where(c_i == r_i - dist, col_val, 0.0)
        scores.append(a.astype(BF16))
    return scores, q_in, k_out


def _hgrn_chunk(q_ref, i_ref, og_ref, hn_ref, o_ref, st_ref, b, kk, rows, score_fn, heads):
    c = b.shape[0]
    q = q_ref[rows, :].astype(F32)
    v_bf = i_ref[rows, :]
    scores, q_in, k_out = score_fn(q, kk, b, heads)
    st_decay = jnp.exp(b[c - 1:c, :])
    gain = hn_ref[...]
    for hd in range(heads):
        sl = slice(hd * HGRN_HEAD_DIM, (hd + 1) * HGRN_HEAD_DIM)
        v_h = v_bf[:, sl]
        st_h = st_ref[hd]
        o_h = _dot(scores[hd], v_h) + _dot_nt(q_in[:, sl], st_h.astype(BF16))
        st_ref[hd] = st_h * st_decay[:, sl] + _dot_tn(v_h, k_out[:, sl])
        o_n = _rms(o_h, gain[:, sl])
        o_ref[rows, sl] = (o_n * og_ref[rows, sl].astype(F32)).astype(BF16)


def _hgrn_kernel(q_ref, lf_ref, i_ref, og_ref, hn_ref, o_ref,
                 st_ref, b_ref, kk_ref, reach_ref, *, chunk, heads):
    t = pl.program_id(1)
    ahead = t % 2
    ready = 1 - ahead

    @pl.when(t == 0)
    def _():
        st_ref[...] = jnp.zeros_like(st_ref)
        reach_ref[ready] = 0.0
        _hgrn_decay_ahead(lf_ref, b_ref, kk_ref, reach_ref, ahead, chunk)

    midref_ok = reach_ref[ready] < HGRN_SAFE_EXPONENT
    for take, score_fn in ((midref_ok, _hgrn_scores_midref),
                           (jnp.logical_not(midref_ok), _hgrn_scores_bounded)):
        @pl.when(jnp.logical_and(t > 0, take))
        def _():
            for lo in range(0, q_ref.shape[0], chunk):
                rows = slice(lo, lo + chunk)
                _hgrn_chunk(q_ref, i_ref, og_ref, hn_ref, o_ref, st_ref,
                            b_ref[ready, rows, :], kk_ref[ready, rows, :], rows, score_fn, heads)
            _hgrn_decay_ahead(lf_ref, b_ref, kk_ref, reach_ref, ahead, chunk)


def _hgrn(q, lf, i, og, head_gain, layer, batch):
    n, w = q.shape
    seq = n // batch
    c = min(HGRN_CHUNK, seq)
    heads = w // HGRN_HEAD_DIM
    rows = min(HGRN_STEP_CHUNKS * c, seq)
    ns = seq // rows
    behind = lambda b, t: (b * ns + jnp.maximum(t - 1, 0), 0)
    upcoming = lambda b, t: (b * ns + jnp.minimum(t, ns - 1), 0)
    spec = pl.BlockSpec((rows, w), behind)
    return pl.pallas_call(
        functools.partial(_hgrn_kernel, chunk=c, heads=heads),
        out_shape=jax.ShapeDtypeStruct((n, w), BF16),
        grid=(batch, ns + 1),
        in_specs=[spec, pl.BlockSpec((rows, w), upcoming), spec, spec,
                  _layer_block(head_gain, layer)],
        out_specs=spec,
        scratch_shapes=[pltpu.VMEM((heads, HGRN_HEAD_DIM, HGRN_HEAD_DIM), F32),
                        pltpu.VMEM((2, rows, w), F32),
                        pltpu.VMEM((2, rows, w), F32),
                        pltpu.SMEM((2,), F32)],
        compiler_params=_params("parallel", "arbitrary"),
        name="hgrn",
    )(q, lf, i, og, head_gain)


def _load_by_token(src_ref, scr_ref):
    dil, rows, width = src_ref.shape
    if dil == 1:
        return src_ref[0].astype(F32)
    parts = []
    for c in range(width // LANES):
        cols = slice(c * LANES, (c + 1) * LANES)
        for r in range(dil):
            scr_ref[c, pl.ds(r, rows, stride=dil), :] = src_ref[r, :, cols].astype(F32)
        parts.append(scr_ref[c])
    return parts[0] if len(parts) == 1 else jnp.concatenate(parts, axis=-1)


def _mix_kernel(x_ref, g_ref, o0_ref, o1_ref, o2_ref, s0_ref, s1_ref, s2_ref, ob_ref,
                wg_ref, wa_ref, wb_ref, wo_ref, out_ref, *scr):
    x = x_ref[...]
    d = x.shape[1]
    h = _rms(x, g_ref[...]).astype(BF16)
    gate_a = jax.nn.sigmoid(_dot(h, wg_ref[:, :d]))
    gate_b = jax.nn.sigmoid(_dot(h, wg_ref[:, d:]))

    scr = list(scr)
    take = lambda ref: _load_by_token(ref, scr.pop(0) if ref.shape[0] > 1 else None)
    o0, o1, o2 = take(o0_ref), take(o1_ref), take(o2_ref)
    s0, s1, s2 = take(s0_ref), take(s1_ref), take(s2_ref)
    head_lane = lax.broadcasted_iota(jnp.int32, s0.shape, 1) < ATT_HEADS
    lse = [s + jnp.log2(jnp.where(head_lane, pltpu.roll(s, LANES - ATT_HEADS, 1), 1.0))
           for s in (s0, s1, s2)]
    mx = jnp.maximum(jnp.maximum(lse[0], lse[1]), lse[2])
    inv = 1.0 / (jnp.exp2(lse[0] - mx) + jnp.exp2(lse[1] - mx) + jnp.exp2(lse[2] - mx))
    w0, w1, w2 = (jnp.exp2(s - mx) * inv for s in (s0, s1, s2))
    parts = []
    for hd in range(ATT_HEADS):
        sl = slice(hd * ATT_HEAD_DIM, (hd + 1) * ATT_HEAD_DIM)
        parts.append(o0[:, sl] * w0[:, hd:hd + 1] + o1[:, sl] * w1[:, hd:hd + 1]
                     + o2[:, sl] * w2[:, hd:hd + 1])
    att = jnp.concatenate(parts, axis=-1).astype(BF16)

    y_a = _dot(att, wa_ref[...])
    y_b = _dot(ob_ref[...], wb_ref[...])
    merged = (gate_a * y_a + gate_b * y_b).astype(BF16)
    out_ref[...] = x + _dot(merged, wo_ref[...])


def _mix(x, norm, o_groups, s_groups, ob, w_in, gate_off, wa, wb, wo, layer, casts=()):
    n, d = x.shape
    batch = o_groups[0].shape[0]
    seq = n // batch
    tm = min(TOKEN_TILE, seq)
    per_b = seq // tm
    row = lambda b, i: (b * per_b + i, 0)
    rows = lambda a: pl.BlockSpec((tm, a.shape[1]), row)
    by_class = lambda a: pl.BlockSpec((None, a.shape[1], tm // a.shape[1], a.shape[3]),
                                      lambda b, i: (b, 0, i, 0))
    grouped = [*o_groups, *s_groups]
    args = [x, norm, *grouped, ob, w_in, wa, wb, wo]
    in_specs = ([rows(x), _layer_block(norm, layer)] + [by_class(a) for a in grouped]
                + [rows(ob), _col_block(w_in, gate_off, 2 * d),
                   _resident(wa), _resident(wb), _resident(wo)])
    scratch = [pltpu.VMEM((a.shape[3] // LANES, tm, LANES), F32)
               for a in grouped if a.shape[1] > 1]
    return _pallas(
        _mix_kernel,
        name="mix_out",
        grid=(batch, per_b), sem=("parallel", "parallel"),
        in_specs=in_specs, args=args,
        out_shape=jax.ShapeDtypeStruct((n, d), F32), out_specs=pl.BlockSpec((tm, d), row),
        scratch=scratch,
        casts=casts, step_of=lambda b, i: b * per_b + i,
    )


def _mem_kv_kernel(m_ref, g_ref, wk_ref, wv_ref, k_ref, v_ref):
    h = _rms(m_ref[...], g_ref[...]).astype(BF16)
    k_ref[...] = _dot(h, wk_ref[...]).astype(BF16)
    v_ref[...] = _dot(h, wv_ref[...]).astype(BF16)


def _mem_kv(mem, norm, wkv, layer):
    n, d = mem.shape
    tm = min(TOKEN_TILE, n)
    width = wkv.shape[1] // 2
    row = lambda i: (i, 0)
    out = jax.ShapeDtypeStruct((n, width), BF16)
    return pl.pallas_call(
        _mem_kv_kernel,
        out_shape=(out, out),
        grid=(n // tm,),
        in_specs=[pl.BlockSpec((tm, d), row), _layer_block(norm, layer),
                  _col_block(wkv, 0, width), _col_block(wkv, width, width)],
        out_specs=(pl.BlockSpec((tm, width), row),) * 2,
        compiler_params=_params("parallel"),
        name="mem_kv",
    )(mem, norm, wkv, wkv)


def _xattn_kernel(x_ref, g_ref, k_ref, v_ref, wq_ref, wo_ref, out_ref):
    x = x_ref[...]
    h = _rms(x, g_ref[...]).astype(BF16)
    dh = wq_ref.shape[1] // X_HEADS
    q = (_dot(h, wq_ref[...]) * (dh ** -0.5)).astype(BF16)
    outs = []
    for hd in range(X_HEADS):
        sl = slice(hd * dh, (hd + 1) * dh)
        s = _dot_nt(q[:, sl], k_ref[:, sl])
        m = jnp.max(s, axis=-1, keepdims=True)
        p = jnp.exp(s - m)
        den = jnp.sum(p, axis=-1, keepdims=True)
        outs.append(_dot(p.astype(BF16), v_ref[:, sl]) / den)
    o = jnp.concatenate(outs, axis=-1).astype(BF16)
    out_ref[...] = x + _dot(o, wo_ref[...])


def _xattn(x, norm, k, v, wq, wo, layer, batch, casts=()):
    n, d = x.shape
    seq = n // batch
    tm = min(TOKEN_TILE, seq)
    n_mem = k.shape[0] // batch
    per_b = seq // tm
    row = lambda b, i: (b * per_b + i, 0)
    kv = pl.BlockSpec((n_mem, k.shape[1]), lambda b, i: (b, 0))
    return _pallas(
        _xattn_kernel,
        name="xattn",
        grid=(batch, per_b), sem=("parallel", "parallel"),
        in_specs=[pl.BlockSpec((tm, d), row), _layer_block(norm, layer), kv, kv,
                  _resident(wq), _resident(wo)],
        args=[x, norm, k, v, wq, wo],
        out_shape=jax.ShapeDtypeStruct((n, d), F32), out_specs=pl.BlockSpec((tm, d), row),
        casts=casts, step_of=lambda b, i: b * per_b + i,
    )


def kernel(x, mem, positions, ffn1_norm, ffn1_w_gu, ffn1_w_down, mix_norm, w_in,
           hgrn_lower_bounds, hgrn_head_norm, w_att_branch, w_hgrn_branch, w_mix_out,
           xattn_norm, mem_norm, xattn_wq, xattn_wkv, xattn_wo,
           ffn2_norm, ffn2_w_gu, ffn2_w_down, final_norm):
    batch, seq, d = x.shape
    depth = w_in.shape[0]
    n = batch * seq
    att_w = len(ATT_GROUPS) * ATT_GROUP_WIDTH
    hg_w = hgrn_head_norm.shape[1]
    hgrn_off = 3 * att_w
    gate_off = hgrn_off + 4 * hg_w

    xf = x.reshape(n, d)
    memf = mem.reshape(batch * mem.shape[1], d)
    cos2, sin2 = _rope_tables(positions.reshape(n, 1).astype(F32))

    gain = lambda a: a.reshape(a.shape[0], 1, a.shape[1])
    ffn1_norm, mix_norm, hgrn_head_norm, xattn_norm, mem_norm, ffn2_norm = map(
        gain, (ffn1_norm, mix_norm, hgrn_head_norm, xattn_norm, mem_norm, ffn2_norm))

    ffn1_w = (ffn1_w_gu[0].astype(BF16), ffn1_w_down[0].astype(BF16))

    for l in range(depth):
        xf, (w_in_l,) = _ffn(xf, ffn1_norm, *ffn1_w, l, casts=[(w_in, l)])

        groups, (w_att_l, w_hgrn_l, w_mix_l) = _proj_attn(
            xf, mix_norm, cos2, sin2, w_in_l, l, batch,
            casts=[(w_att_branch, l), (w_hgrn_branch, l), (w_mix_out, l)])
        o_groups, s_groups = [], []
        for q_a, k_a, v_a in groups:
            o_g, s_g = _attn_group(q_a, k_a, v_a)
            o_groups.append(o_g)
            s_groups.append(s_g)
        (q_b, lf_b, i_b, og_b), (wkv_l, wq_l, wo_l) = _proj_hgrn(
            xf, mix_norm, hgrn_lower_bounds, w_in_l, hgrn_off, l,
            casts=[(xattn_wkv, l), (xattn_wq, l), (xattn_wo, l)])
        o_b = _hgrn(q_b, lf_b, i_b, og_b, hgrn_head_norm, l, batch)
        xf, _ = _mix(xf, mix_norm, o_groups, s_groups, o_b, w_in_l, gate_off,
                     w_att_l, w_hgrn_l, w_mix_l, l)

        k_x, v_x = _mem_kv(memf, mem_norm, wkv_l, l)
        xf, (ffn2_gu_l, ffn2_down_l) = _xattn(xf, xattn_norm, k_x, v_x, wq_l, wo_l, l, batch,
                                              casts=[(ffn2_w_gu, l), (ffn2_w_down, l)])

        last = l == depth - 1
        xf, ffn1_w = _ffn(xf, ffn2_norm, ffn2_gu_l, ffn2_down_l, l,
                          final_g=final_norm.reshape(1, -1) if last else None,
                          casts=[] if last else [(ffn1_w_gu, l + 1), (ffn1_w_down, l + 1)])

    return xf.reshape(batch, seq, d)
```

```python
import functools
import math

import jax
import jax.numpy as jnp
from jax import lax
from jax.experimental import pallas as pl
from jax.experimental.pallas import tpu as pltpu

F32 = jnp.float32
BF16 = jnp.bfloat16

ATT_GROUPS = ((128, 1), (512, 4), (2048, 16))
ATT_HEADS = 4
ATT_HEAD_DIM = 128
ATT_GROUP_WIDTH = ATT_HEADS * ATT_HEAD_DIM
HGRN_HEAD_DIM = 128
X_HEADS = 4
ROPE_THETA = 10000.0
EPS = 1e-6

LANES = 128
SUBLANES = 8
MXU_TILE = 256
VMEM_LIMIT_BYTES = 56 * 1024 * 1024

TOKEN_TILE = 1024
FFN_CHUNK_TILES = 3
MXU_PERMUTE_MIN_STRIDE = 8
ATT_BLOCK = 128
ATT_CHAINS = 8
HGRN_CHUNK = 128
HGRN_STEP_CHUNKS = 2
HGRN_SAFE_EXPONENT = 115.0
NEG_BIG = -1e30


def _params(*sem):
    return pltpu.CompilerParams(dimension_semantics=sem,
                                vmem_limit_bytes=VMEM_LIMIT_BYTES)


def _rms(x, g):
    ms = jnp.mean(x * x, axis=-1, keepdims=True)
    return x * lax.rsqrt(ms + EPS) * g


def _dot(a, b):
    return jnp.dot(a, b, preferred_element_type=F32)


def _dot_nt(a, b):
    return lax.dot_general(a, b, (((1,), (1,)), ((), ())), preferred_element_type=F32)


def _dot_tn(a, b):
    return lax.dot_general(a, b, (((0,), (0,)), ((), ())), preferred_element_type=F32)


def _resident(a):
    return pl.BlockSpec(a.shape, lambda *_: (0,) * a.ndim, pipeline_mode=pl.Buffered(1))


def _layer_block(a, layer):
    _, rows, width = a.shape
    return pl.BlockSpec((None, rows, width), lambda *_: (layer, 0, 0),
                        pipeline_mode=pl.Buffered(1))


def _col_block(a, off, size):
    return pl.BlockSpec((pl.Element(a.shape[0]), pl.Element(size)), lambda *_: (0, off),
                        pipeline_mode=pl.Buffered(1))


def _pallas(body, *, name, grid, sem, in_specs, args, out_shape, out_specs, scratch=(),
            casts=(), step_of=None):
    single = not isinstance(out_shape, (tuple, list))
    out_shape = [out_shape] if single else list(out_shape)
    out_specs = [out_specs] if single else list(out_specs)
    n_in, n_out, n_cast = len(in_specs), len(out_shape), len(casts)
    steps = math.prod(grid)
    cast_in, cast_out, cast_shape = [], [], []
    for a, layer in casts:
        _, rows, cols = a.shape
        chunk = rows // steps
        assert chunk * steps == rows and chunk % (2 * SUBLANES) == 0
        cast_in.append(pl.BlockSpec((None, chunk, cols),
                                    lambda *g, layer=layer: (layer, step_of(*g), 0)))
        cast_out.append(pl.BlockSpec((chunk, cols), lambda *g: (step_of(*g), 0)))
        cast_shape.append(jax.ShapeDtypeStruct((rows, cols), BF16))

    def hosted(*refs):
        ins = refs[:n_in]
        srcs = refs[n_in:n_in + n_cast]
        outs = refs[n_in + n_cast:n_in + n_cast + n_out]
        dsts = refs[n_in + n_cast + n_out:n_in + 2 * n_cast + n_out]
        for src, dst in zip(srcs, dsts):
            dst[...] = src[...].astype(BF16)
        body(*ins, *outs, *refs[n_in + 2 * n_cast + n_out:])

    res = pl.pallas_call(
        hosted,
        out_shape=tuple(out_shape + cast_shape),
        grid=grid,
        in_specs=list(in_specs) + cast_in,
        out_specs=tuple(out_specs + cast_out),
        scratch_shapes=list(scratch),
        compiler_params=_params(*sem),
        name=name,
    )(*args, *[a for a, _ in casts])
    main = res[:n_out]
    return (main[0] if single else tuple(main)), list(res[n_out:])


def _rope_table_kernel(pos_a_ref, pos_b_ref, cos_ref, sin_ref):
    half = ATT_HEAD_DIM // 2
    rows = pos_a_ref.shape[0]
    lane = lax.broadcasted_iota(jnp.int32, (1, ATT_HEAD_DIM), 1)
    upper = lane >= half
    idx = jnp.where(upper, lane - half, lane).astype(F32)
    inv_freq = jnp.exp(idx * (-2.0 * math.log(ROPE_THETA) / ATT_HEAD_DIM))
    ang = jnp.where(upper, pos_b_ref[...], pos_a_ref[...]) * inv_freq
    sign = jnp.where(upper, 1.0, -1.0)
    for val, out_ref, mult in ((jnp.cos(ang), cos_ref, 1.0), (jnp.sin(ang), sin_ref, sign)):
        swapped = pltpu.roll(val, half, 1)
        out_ref[0:rows, :] = jnp.where(upper, swapped, val) * mult
        out_ref[rows:2 * rows, :] = jnp.where(upper, val, swapped) * mult


def _rope_tables(pos_col):
    n = pos_col.shape[0]
    tm = min(n, 2048)
    out = jax.ShapeDtypeStruct((n, ATT_HEAD_DIM), F32)
    return pl.pallas_call(
        _rope_table_kernel,
        out_shape=(out, out),
        grid=(n // tm,),
        in_specs=[pl.BlockSpec((tm // 2, 1), lambda i: (2 * i, 0)),
                  pl.BlockSpec((tm // 2, 1), lambda i: (2 * i + 1, 0))],
        out_specs=(pl.BlockSpec((tm, ATT_HEAD_DIM), lambda i: (i, 0)),) * 2,
        compiler_params=_params("parallel"),
        name="rope_tables",
    )(pos_col, pos_col)


def _ffn_chunks(d_ff):
    assert d_ff % MXU_TILE == 0
    tiles = d_ff // MXU_TILE
    n_chunks = -(-tiles // FFN_CHUNK_TILES)
    bounds = [MXU_TILE * (tiles * k // n_chunks) for k in range(n_chunks + 1)]
    return list(zip(bounds[:-1], bounds[1:]))


def _ffn_kernel(*refs, final):
    if final:
        x_ref, g_ref, wg_ref, wu_ref, wd_ref, fg_ref, o_ref = refs
    else:
        x_ref, g_ref, wg_ref, wu_ref, wd_ref, o_ref = refs
    x = x_ref[...]
    h = _rms(x, g_ref[...]).astype(BF16)
    acc = None
    for lo, hi in _ffn_chunks(wg_ref.shape[1]):
        gate = _dot(h, wg_ref[:, lo:hi])
        up = _dot(h, wu_ref[:, lo:hi])
        act = (gate * jax.nn.sigmoid(gate) * up).astype(BF16)
        part = _dot(act, wd_ref[lo:hi, :])
        acc = part if acc is None else acc + part
    y = x + 0.5 * acc
    if final:
        y = _rms(y, fg_ref[...])
    o_ref[...] = y


def _ffn(x, norm, w_gu, w_down, layer, final_g=None, casts=()):
    n, d = x.shape
    d_ff = w_down.shape[0]
    tm = min(TOKEN_TILE, n)
    final = final_g is not None
    row = pl.BlockSpec((tm, d), lambda i: (i, 0))
    in_specs = [row, _layer_block(norm, layer), _col_block(w_gu, 0, d_ff),
                _col_block(w_gu, d_ff, d_ff), _resident(w_down)]
    args = [x, norm, w_gu, w_gu, w_down]
    if final:
        in_specs.append(_resident(final_g))
        args.append(final_g)
    return _pallas(
        functools.partial(_ffn_kernel, final=final),
        name="ffn_final" if final else "ffn",
        grid=(n // tm,), sem=("parallel",),
        in_specs=in_specs, args=args,
        out_shape=jax.ShapeDtypeStruct((n, d), F32), out_specs=row,
        casts=casts, step_of=lambda i: i,
    )


def _class_permutation(dil):
    per = MXU_TILE // dil
    i = lax.broadcasted_iota(jnp.int32, (MXU_TILE, MXU_TILE), 0)
    j = lax.broadcasted_iota(jnp.int32, (MXU_TILE, MXU_TILE), 1)
    return jnp.where(j == (i % per) * dil + i // per, 1.0, 0.0).astype(BF16)


def _store_by_class(out_ref, scr_ref, val, dil):
    if dil == 1:
        out_ref[0] = val.astype(out_ref.dtype)
        return
    if dil < MXU_PERMUTE_MIN_STRIDE:
        rows = val.shape[0] // dil
        for c in range(val.shape[1] // LANES):
            cols = slice(c * LANES, (c + 1) * LANES)
            scr_ref[c] = val[:, cols]
            for r in range(dil):
                out_ref[r, :, cols] = (
                    scr_ref[c, pl.ds(r, rows, stride=dil), :].astype(out_ref.dtype))
        return
    val = val.astype(out_ref.dtype)
    perm = _class_permutation(dil)
    per = MXU_TILE // dil
    for s in range(val.shape[0] // MXU_TILE):
        y = _dot(perm, val[s * MXU_TILE:(s + 1) * MXU_TILE, :]).astype(out_ref.dtype)
        for r in range(dil):
            out_ref[r, s * per:(s + 1) * per, :] = y[r * per:(r + 1) * per, :]


def _proj_attn_kernel(x_ref, g_ref, cos_ref, sin_ref, w_ref, *rest, dils):
    n_g = len(dils)
    outs, scr = rest[:3 * n_g], rest[3 * n_g:]
    scr = scr if scr else (None,) * 3
    gw = ATT_GROUP_WIDTH
    h = _rms(x_ref[...], g_ref[...]).astype(BF16)
    cos = cos_ref[...]
    sin = sin_ref[...]
    scale = ATT_HEAD_DIM ** -0.5 * math.log2(math.e)
    for gi, dil in enumerate(dils):
        part = lambda p: w_ref[:, (p * n_g + gi) * gw:(p * n_g + gi + 1) * gw]
        zq = _dot(h, part(0))
        zk = _dot(h, part(1))
        rq, rk = [], []
        for hd in range(ATT_HEADS):
            sl = slice(hd * ATT_HEAD_DIM, (hd + 1) * ATT_HEAD_DIM)
            xq = zq[:, sl]
            xk = zk[:, sl]
            rq.append((xq * cos + pltpu.roll(xq, ATT_HEAD_DIM // 2, 1) * sin) * scale)
            rk.append(xk * cos + pltpu.roll(xk, ATT_HEAD_DIM // 2, 1) * sin)
        q_ref, k_ref, v_ref = outs[3 * gi:3 * gi + 3]
        _store_by_class(q_ref, scr[0], jnp.concatenate(rq, axis=-1), dil)
        _store_by_class(k_ref, scr[1], jnp.concatenate(rk, axis=-1), dil)
        _store_by_class(v_ref, scr[2], _dot(h, part(2)), dil)


def _proj_attn(x, norm, cos2, sin2, w_in, layer, batch, casts=()):
    n, d = x.shape
    seq = n // batch
    tm = min(TOKEN_TILE, seq)
    per_b = seq // tm
    gw = ATT_GROUP_WIDTH
    dils = tuple(dil for _, dil in ATT_GROUPS)
    row = lambda b, i: (b * per_b + i, 0)
    out_shape, out_specs = [], []
    for dil in dils:
        out_shape += [jax.ShapeDtypeStruct((batch, dil, seq // dil, gw), BF16)] * 3
        out_specs += [pl.BlockSpec((None, dil, tm // dil, gw), lambda b, i: (b, 0, i, 0))] * 3
    strided = any(1 < dil < MXU_PERMUTE_MIN_STRIDE for dil in dils)
    scratch = [pltpu.VMEM((gw // LANES, tm, LANES), F32)] * 3 if strided else []
    outs, cast_res = _pallas(
        functools.partial(_proj_attn_kernel, dils=dils),
        name="proj_attn",
        grid=(batch, per_b), sem=("parallel", "parallel"),
        in_specs=[
            pl.BlockSpec((tm, d), row),
            _layer_block(norm, layer),
            pl.BlockSpec((tm, ATT_HEAD_DIM), row),
            pl.BlockSpec((tm, ATT_HEAD_DIM), row),
            _col_block(w_in, 0, 3 * len(dils) * gw),
        ],
        args=[x, norm, cos2, sin2, w_in],
        out_shape=out_shape, out_specs=out_specs, scratch=scratch,
        casts=casts, step_of=lambda b, i: b * per_b + i,
    )
    return [outs[3 * gi:3 * gi + 3] for gi in range(len(dils))], cast_res


def _proj_hgrn_kernel(x_ref, g_ref, lbraw_ref, w_ref, q_ref, lf_ref, i_ref, og_ref, *, layer):
    h = _rms(x_ref[...], g_ref[...]).astype(BF16)
    raw = lbraw_ref[...]
    p = jnp.exp(raw - jnp.max(raw, axis=0, keepdims=True))
    p = p / jnp.sum(p, axis=0, keepdims=True)
    lb = jnp.sum(p[:layer + 1], axis=0, keepdims=True) - p[0:1]

    hw = q_ref.shape[1]
    sec = lambda k: w_ref[:, k * hw:(k + 1) * hw]
    q_ref[...] = (_dot(h, sec(0)) * (HGRN_HEAD_DIM ** -0.5)).astype(BF16)
    f = lb + (1.0 - lb) * jax.nn.sigmoid(_dot(h, sec(1)))
    lf_ref[...] = jnp.log2(f)
    i_ref[...] = _dot(h, sec(2)).astype(BF16)
    og = _dot(h, sec(3))
    og_ref[...] = (og * jax.nn.sigmoid(og)).astype(BF16)


def _proj_hgrn(x, norm, lb_raw, w_in, col_off, layer, casts=()):
    n, d = x.shape
    tm = min(TOKEN_TILE, n)
    width = lb_raw.shape[1]
    row = lambda i: (i, 0)
    bf = jax.ShapeDtypeStruct((n, width), BF16)
    return _pallas(
        functools.partial(_proj_hgrn_kernel, layer=layer),
        name="proj_hgrn",
        grid=(n // tm,), sem=("parallel",),
        in_specs=[pl.BlockSpec((tm, d), row), _layer_block(norm, layer), _resident(lb_raw),
                  _col_block(w_in, col_off, 4 * width)],
        args=[x, norm, lb_raw, w_in],
        out_shape=(bf, jax.ShapeDtypeStruct((n, width), F32), bf, bf),
        out_specs=(pl.BlockSpec((tm, width), row),) * 4,
        casts=casts, step_of=lambda i: i,
    )


def _attn_kernel(q_ref, kp_ref, kc_ref, vp_ref, vc_ref, o_ref, st_ref, *, nres, nblk):
    first = pl.program_id(2) == 0
    row = lax.broadcasted_iota(jnp.int32, (ATT_BLOCK, 2 * ATT_BLOCK), 0)
    col = lax.broadcasted_iota(jnp.int32, (ATT_BLOCK, 2 * ATT_BLOCK), 1)
    band = jnp.logical_and(col >= row, col <= row + ATT_BLOCK)
    low = jnp.where(first, ATT_BLOCK, 0)
    band_first = jnp.logical_and(col >= jnp.maximum(row, low), col <= row + ATT_BLOCK)
    ones = jnp.ones((2 * ATT_BLOCK, ATT_HEAD_DIM), BF16)
    lane = lax.broadcasted_iota(jnp.int32, (ATT_BLOCK, LANES), 1)
    for res in range(nres):
        for blk in range(nblk):
            rows = slice(blk * ATT_BLOCK, (blk + 1) * ATT_BLOCK)
            stats = jnp.ones((ATT_BLOCK, LANES), F32)
            for hd in range(ATT_HEADS):
                sl = slice(hd * ATT_HEAD_DIM, (hd + 1) * ATT_HEAD_DIM)
                if blk == 0:
                    k2 = jnp.concatenate([kp_ref[res, :, sl], kc_ref[res, 0:ATT_BLOCK, sl]], axis=0)
                    v2 = jnp.concatenate([vp_ref[res, :, sl], vc_ref[res, 0:ATT_BLOCK, sl]], axis=0)
                    mask = band_first
                else:
                    win = slice((blk - 1) * ATT_BLOCK, (blk + 1) * ATT_BLOCK)
                    k2 = kc_ref[res, win, sl]
                    v2 = vc_ref[res, win, sl]
                    mask = band
                s = jnp.where(mask, _dot_nt(q_ref[res, rows, sl], k2), NEG_BIG)
                m = jnp.max(s, axis=-1, keepdims=True)
                p = jnp.exp2(s - m).astype(BF16)
                o_den = _dot(p, jnp.concatenate([v2, ones], axis=1))
                den = o_den[:, ATT_HEAD_DIM:]
                o_ref[res, rows, sl] = o_den[:, :ATT_HEAD_DIM].astype(BF16)
                stats = jnp.where(lane == hd, m, jnp.where(lane == ATT_HEADS + hd, den, stats))
            st_ref[res, rows, :] = stats


def _attn_group(q, k, v):
    batch, dil, cls, gw = q.shape
    assert cls % ATT_BLOCK == 0
    nblk = min(ATT_CHAINS, cls // ATT_BLOCK)
    nres = max(1, min(dil, ATT_CHAINS // nblk))
    qb = nblk * ATT_BLOCK
    cur = lambda b, r, i: (b, r, i, 0)
    prev = lambda b, r, i: (b, r, jnp.maximum(i * nblk - 1, 0), 0)
    cur_spec = pl.BlockSpec((None, nres, qb, gw), cur)
    prev_spec = pl.BlockSpec((None, nres, ATT_BLOCK, gw), prev)
    return pl.pallas_call(
        functools.partial(_attn_kernel, nres=nres, nblk=nblk),
        out_shape=(jax.ShapeDtypeStruct((batch, dil, cls, gw), BF16),
                   jax.ShapeDtypeStruct((batch, dil, cls, LANES), F32)),
        grid=(batch, dil // nres, cls // qb),
        in_specs=[cur_spec, prev_spec, cur_spec, prev_spec, cur_spec],
        out_specs=(pl.BlockSpec((None, nres, qb, gw), cur),
                   pl.BlockSpec((None, nres, qb, LANES), cur)),
        compiler_params=_params("parallel", "parallel", "arbitrary"),
        name=f"attn_dil{dil}",
    )(q, k, k, v, v)


def _hgrn_decay_ahead(lf_ref, b_ref, kk_ref, reach_ref, slot, c):
    r_i = lax.broadcasted_iota(jnp.int32, (c, c), 0)
    c_i = lax.broadcasted_iota(jnp.int32, (c, c), 1)
    tri = jnp.where(c_i <= r_i, 1.0, 0.0).astype(BF16)
    reach = None
    for lo in range(0, lf_ref.shape[0], c):
        lf = lf_ref[lo:lo + c, :]
        lf_hi = lf.astype(BF16)
        lf_lo = (lf - lf_hi.astype(F32)).astype(BF16)
        b = _dot(tri, lf_hi) + _dot(tri, lf_lo)
        b_ref[slot, lo:lo + c, :] = b
        kk_ref[slot, lo:lo + c, :] = 1.0 - jnp.exp2(lf)
        b_mid = b[c // 2 - 1:c // 2, :]
        r = jnp.max(jnp.maximum(b[0:1, :] - b_mid, b_mid - b[c - 1:c, :]))
        reach = r if reach is None else jnp.maximum(reach, r)
    reach_ref[slot] = reach


def _hgrn_scores_midref(q, kk, b, heads):
    c = b.shape[0]
    b_mid = b[c // 2 - 1:c // 2, :]
    b_last = b[c - 1:c, :]
    q_s = q * jnp.exp2(b - b_mid).astype(BF16)
    k_s = (kk * jnp.exp2(b_mid - b)).astype(BF16)
    q_in = q_s * jnp.exp2(b_mid).astype(BF16)
    k_out = k_s * jnp.exp2(b_last - b_mid).astype(BF16)
    r_i = lax.broadcasted_iota(jnp.int32, (c, c), 0)
    c_i = lax.broadcasted_iota(jnp.int32, (c, c), 1)
    causal = c_i <= r_i
    scores = []
    for hd in range(heads):
        sl = slice(hd * HGRN_HEAD_DIM, (hd + 1) * HGRN_HEAD_DIM)
        scores.append(jnp.where(causal, _dot_nt(q_s[:, sl], k_s[:, sl]), 0.0).astype(BF16))
    return scores, q_in, k_out


def _hgrn_scores_bounded(q, kk, b, heads):
    c, w = b.shape
    q = q.astype(F32)
    q_in = (q * jnp.exp2(b)).astype(BF16)
    k_out = (kk * jnp.exp2(b[c - 1:c, :] - b)).astype(BF16)
    r_i = lax.broadcasted_iota(jnp.int32, (c, c), 0)
    c_i = lax.broadcasted_iota(jnp.int32, (c, c), 1)
    rows = lax.broadcasted_iota(jnp.int32, (c, w), 0)

    levels = []
    half = SUBLANES
    while half < c:
        span = 2 * half
        pieces = [jnp.broadcast_to(b[s + half - 1:s + half, :], (span, w))
                  for s in range(0, c, span)]
        bref = pieces[0] if len(pieces) == 1 else jnp.concatenate(pieces, axis=0)
        upper = (rows & (span - 1)) >= half
        e = jnp.exp2(jnp.where(upper, b - bref, bref - b))
        q_up = jnp.where(upper, q * e, 0.0).astype(BF16)
        k_lo = jnp.where(upper, 0.0, kk * e).astype(BF16)
        same = (r_i & ~(span - 1)) == (c_i & ~(span - 1))
        levels.append((q_up, k_lo, same))
        half = span

    nb = c // SUBLANES
    sub = lax.broadcasted_iota(jnp.int32, (nb, SUBLANES, w), 1)
    q3 = q.reshape(nb, SUBLANES, w)
    k3 = kk.reshape(nb, SUBLANES, w)
    b3 = b.reshape(nb, SUBLANES, w)
    near = []
    for dist in range(SUBLANES):
        if dist == 0:
            prod = q3 * k3
        else:
            k_r = pltpu.roll(k3, dist, 1)
            b_r = pltpu.roll(b3, dist, 1)
            prod = jnp.where(sub >= dist, q3 * k_r * jnp.exp2(b3 - b_r), 0.0)
        near.append(prod.reshape(c, w))

    scores = []
    for hd in range(heads):
        sl = slice(hd * HGRN_HEAD_DIM, (hd + 1) * HGRN_HEAD_DIM)
        a = jnp.zeros((c, c), F32)
        for q_up, k_lo, same in levels:
            a = a + jnp.where(same, _dot_nt(q_up[:, sl], k_lo[:, sl]), 0.0)
        for dist in range(SUBLANES):
            col_val = jnp.sum(near[dist][:, sl], axis=-1, keepdims=True)
            a = a + jnp.where(c_i == r_i - dist, col_val, 0.0)
        scores.append(a.astype(BF16))
    return scores, q_in, k_out


def _hgrn_chunk(q_ref, i_ref, og_ref, hn_ref, o_ref, st_ref, b, kk, rows, score_fn, heads):
    c = b.shape[0]
    v_bf = i_ref[rows, :]
    scores, q_in, k_out = score_fn(q_ref[rows, :], kk, b, heads)
    st_decay = jnp.exp2(b[c - 1:c, :])
    gain = hn_ref[...]
    for hd in range(heads):
        sl = slice(hd * HGRN_HEAD_DIM, (hd + 1) * HGRN_HEAD_DIM)
        v_h = v_bf[:, sl]
        st_h = st_ref[hd]
        o_h = _dot(scores[hd], v_h) + _dot_nt(q_in[:, sl], st_h.astype(BF16))
        st_ref[hd] = st_h * st_decay[:, sl] + _dot_tn(v_h, k_out[:, sl])
        o_n = _rms(o_h, gain[:, sl])
        o_ref[rows, sl] = (o_n * og_ref[rows, sl].astype(F32)).astype(BF16)


def _hgrn_kernel(q_ref, lf_ref, i_ref, og_ref, hn_ref, o_ref,
                 st_ref, b_ref, kk_ref, reach_ref, *, chunk, heads):
    t = pl.program_id(1)
    ahead = t % 2
    ready = 1 - ahead

    @pl.when(t == 0)
    def _():
        st_ref[...] = jnp.zeros_like(st_ref)
        reach_ref[ready] = 0.0
        _hgrn_decay_ahead(lf_ref, b_ref, kk_ref, reach_ref, ahead, chunk)

    midref_ok = reach_ref[ready] < HGRN_SAFE_EXPONENT
    for take, score_fn in ((midref_ok, _hgrn_scores_midref),
                           (jnp.logical_not(midref_ok), _hgrn_scores_bounded)):
        @pl.when(jnp.logical_and(t > 0, take))
        def _():
            for lo in range(0, q_ref.shape[0], chunk):
                rows = slice(lo, lo + chunk)
                _hgrn_chunk(q_ref, i_ref, og_ref, hn_ref, o_ref, st_ref,
                            b_ref[ready, rows, :], kk_ref[ready, rows, :], rows, score_fn, heads)
            _hgrn_decay_ahead(lf_ref, b_ref, kk_ref, reach_ref, ahead, chunk)


def _hgrn(q, lf, i, og, head_gain, layer, batch):
    n, w = q.shape
    seq = n // batch
    c = min(HGRN_CHUNK, seq)
    heads = w // HGRN_HEAD_DIM
    rows = min(HGRN_STEP_CHUNKS * c, seq)
    ns = seq // rows
    behind = lambda b, t: (b * ns + jnp.maximum(t - 1, 0), 0)
    upcoming = lambda b, t: (b * ns + jnp.minimum(t, ns - 1), 0)
    spec = pl.BlockSpec((rows, w), behind)
    return pl.pallas_call(
        functools.partial(_hgrn_kernel, chunk=c, heads=heads),
        out_shape=jax.ShapeDtypeStruct((n, w), BF16),
        grid=(batch, ns + 1),
        in_specs=[spec, pl.BlockSpec((rows, w), upcoming), spec, spec,
                  _layer_block(head_gain, layer)],
        out_specs=spec,
        scratch_shapes=[pltpu.VMEM((heads, HGRN_HEAD_DIM, HGRN_HEAD_DIM), F32),
                        pltpu.VMEM((2, rows, w), F32),
                        pltpu.VMEM((2, rows, w), F32),
                        pltpu.SMEM((2,), F32)],
        compiler_params=_params("parallel", "arbitrary"),
        name="hgrn",
    )(q, lf, i, og, head_gain)


def _load_by_token(src_ref, scr_ref):
    dil, rows, width = src_ref.shape
    if dil == 1:
        return src_ref[0].astype(F32)
    parts = []
    for c in range(width // LANES):
        cols = slice(c * LANES, (c + 1) * LANES)
        for r in range(dil):
            scr_ref[c, pl.ds(r, rows, stride=dil), :] = src_ref[r, :, cols].astype(F32)
        parts.append(scr_ref[c])
    return parts[0] if len(parts) == 1 else jnp.concatenate(parts, axis=-1)


def _mix_kernel(x_ref, g_ref, o0_ref, o1_ref, o2_ref, s0_ref, s1_ref, s2_ref, ob_ref,
                wg_ref, wa_ref, wb_ref, wo_ref, out_ref, *scr):
    x = x_ref[...]
    d = x.shape[1]
    h = _rms(x, g_ref[...]).astype(BF16)
    gate_a = jax.nn.sigmoid(_dot(h, wg_ref[:, :d]))
    gate_b = jax.nn.sigmoid(_dot(h, wg_ref[:, d:]))

    scr = list(scr)
    take = lambda ref: _load_by_token(ref, scr.pop(0) if ref.shape[0] > 1 else None)
    o0, o1, o2 = take(o0_ref), take(o1_ref), take(o2_ref)
    s0, s1, s2 = take(s0_ref), take(s1_ref), take(s2_ref)
    head_lane = lax.broadcasted_iota(jnp.int32, s0.shape, 1) < ATT_HEADS
    lse = [s + jnp.log2(jnp.where(head_lane, pltpu.roll(s, LANES - ATT_HEADS, 1), 1.0))
           for s in (s0, s1, s2)]
    mx = jnp.maximum(jnp.maximum(lse[0], lse[1]), lse[2])
    inv = 1.0 / (jnp.exp2(lse[0] - mx) + jnp.exp2(lse[1] - mx) + jnp.exp2(lse[2] - mx))
    w0, w1, w2 = (jnp.exp2(s - mx) * inv for s in (s0, s1, s2))
    parts = []
    for hd in range(ATT_HEADS):
        sl = slice(hd * ATT_HEAD_DIM, (hd + 1) * ATT_HEAD_DIM)
        parts.append(o0[:, sl] * w0[:, hd:hd + 1] + o1[:, sl] * w1[:, hd:hd + 1]
                     + o2[:, sl] * w2[:, hd:hd + 1])
    att = jnp.concatenate(parts, axis=-1).astype(BF16)

    y_a = _dot(att, wa_ref[...])
    y_b = _dot(ob_ref[...], wb_ref[...])
    merged = (gate_a * y_a + gate_b * y_b).astype(BF16)
    out_ref[...] = x + _dot(merged, wo_ref[...])


def _mix(x, norm, o_groups, s_groups, ob, w_in, gate_off, wa, wb, wo, layer, casts=()):
    n, d = x.shape
    batch = o_groups[0].shape[0]
    seq = n // batch
    tm = min(TOKEN_TILE, seq)
    per_b = seq // tm
    row = lambda b, i: (b * per_b + i, 0)
    rows = lambda a: pl.BlockSpec((tm, a.shape[1]), row)
    by_class = lambda a: pl.BlockSpec((None, a.shape[1], tm // a.shape[1], a.shape[3]),
                                      lambda b, i: (b, 0, i, 0))
    grouped = [*o_groups, *s_groups]
    args = [x, norm, *grouped, ob, w_in, wa, wb, wo]
    in_specs = ([rows(x), _layer_block(norm, layer)] + [by_class(a) for a in grouped]
                + [rows(ob), _col_block(w_in, gate_off, 2 * d),
                   _resident(wa), _resident(wb), _resident(wo)])
    scratch = [pltpu.VMEM((a.shape[3] // LANES, tm, LANES), F32)
               for a in grouped if a.shape[1] > 1]
    return _pallas(
        _mix_kernel,
        name="mix_out",
        grid=(batch, per_b), sem=("parallel", "parallel"),
        in_specs=in_specs, args=args,
        out_shape=jax.ShapeDtypeStruct((n, d), F32), out_specs=pl.BlockSpec((tm, d), row),
        scratch=scratch,
        casts=casts, step_of=lambda b, i: b * per_b + i,
    )


def _mem_kv_kernel(m_ref, g_ref, wk_ref, wv_ref, k_ref, v_ref):
    h = _rms(m_ref[...], g_ref[...]).astype(BF16)
    k_ref[...] = _dot(h, wk_ref[...]).astype(BF16)
    v_ref[...] = _dot(h, wv_ref[...]).astype(BF16)


def _mem_kv(mem, norm, wkv, layer):
    n, d = mem.shape
    tm = min(TOKEN_TILE, n)
    width = wkv.shape[1] // 2
    row = lambda i: (i, 0)
    out = jax.ShapeDtypeStruct((n, width), BF16)
    return pl.pallas_call(
        _mem_kv_kernel,
        out_shape=(out, out),
        grid=(n // tm,),
        in_specs=[pl.BlockSpec((tm, d), row), _layer_block(norm, layer),
                  _col_block(wkv, 0, width), _col_block(wkv, width, width)],
        out_specs=(pl.BlockSpec((tm, width), row),) * 2,
        compiler_params=_params("parallel"),
        name="mem_kv",
    )(mem, norm, wkv, wkv)


def _xattn_kernel(x_ref, g_ref, k_ref, v_ref, wq_ref, wo_ref, out_ref):
    x = x_ref[...]
    h = _rms(x, g_ref[...]).astype(BF16)
    dh = wq_ref.shape[1] // X_HEADS
    q = (_dot(h, wq_ref[...]) * (dh ** -0.5)).astype(BF16)
    outs = []
    for hd in range(X_HEADS):
        sl = slice(hd * dh, (hd + 1) * dh)
        s = _dot_nt(q[:, sl], k_ref[:, sl])
        m = jnp.max(s, axis=-1, keepdims=True)
        p = jnp.exp(s - m)
        den = jnp.sum(p, axis=-1, keepdims=True)
        outs.append(_dot(p.astype(BF16), v_ref[:, sl]) / den)
    o = jnp.concatenate(outs, axis=-1).astype(BF16)
    out_ref[...] = x + _dot(o, wo_ref[...])


def _xattn(x, norm, k, v, wq, wo, layer, batch, casts=()):
    n, d = x.shape
    seq = n // batch
    tm = min(TOKEN_TILE, seq)
    n_mem = k.shape[0] // batch
    per_b = seq // tm
    row = lambda b, i: (b * per_b + i, 0)
    kv = pl.BlockSpec((n_mem, k.shape[1]), lambda b, i: (b, 0))
    return _pallas(
        _xattn_kernel,
        name="xattn",
        grid=(batch, per_b), sem=("parallel", "parallel"),
        in_specs=[pl.BlockSpec((tm, d), row), _layer_block(norm, layer), kv, kv,
                  _resident(wq), _resident(wo)],
        args=[x, norm, k, v, wq, wo],
        out_shape=jax.ShapeDtypeStruct((n, d), F32), out_specs=pl.BlockSpec((tm, d), row),
        casts=casts, step_of=lambda b, i: b * per_b + i,
    )


def kernel(x, mem, positions, ffn1_norm, ffn1_w_gu, ffn1_w_down, mix_norm, w_in,
           hgrn_lower_bounds, hgrn_head_norm, w_att_branch, w_hgrn_branch, w_mix_out,
           xattn_norm, mem_norm, xattn_wq, xattn_wkv, xattn_wo,
           ffn2_norm, ffn2_w_gu, ffn2_w_down, final_norm):
    batch, seq, d = x.shape
    depth = w_in.shape[0]
    n = batch * seq
    att_w = len(ATT_GROUPS) * ATT_GROUP_WIDTH
    hg_w = hgrn_head_norm.shape[1]
    hgrn_off = 3 * att_w
    gate_off = hgrn_off + 4 * hg_w

    xf = x.reshape(n, d)
    memf = mem.reshape(batch * mem.shape[1], d)
    cos2, sin2 = _rope_tables(positions.reshape(n, 1).astype(F32))

    gain = lambda a: a.reshape(a.shape[0], 1, a.shape[1])
    ffn1_norm, mix_norm, hgrn_head_norm, xattn_norm, mem_norm, ffn2_norm = map(
        gain, (ffn1_norm, mix_norm, hgrn_head_norm, xattn_norm, mem_norm, ffn2_norm))

    ffn1_w = (ffn1_w_gu[0].astype(BF16), ffn1_w_down[0].astype(BF16))

    for l in range(depth):
        xf, (w_in_l,) = _ffn(xf, ffn1_norm, *ffn1_w, l, casts=[(w_in, l)])

        groups, (w_att_l, w_hgrn_l, w_mix_l) = _proj_attn(
            xf, mix_norm, cos2, sin2, w_in_l, l, batch,
            casts=[(w_att_branch, l), (w_hgrn_branch, l), (w_mix_out, l)])
        o_groups, s_groups = [], []
        for q_a, k_a, v_a in groups:
            o_g, s_g = _attn_group(q_a, k_a, v_a)
            o_groups.append(o_g)
            s_groups.append(s_g)
        (q_b, lf_b, i_b, og_b), (wkv_l, wq_l, wo_l) = _proj_hgrn(
            xf, mix_norm, hgrn_lower_bounds, w_in_l, hgrn_off, l,
            casts=[(xattn_wkv, l), (xattn_wq, l), (xattn_wo, l)])
        o_b = _hgrn(q_b, lf_b, i_b, og_b, hgrn_head_norm, l, batch)
        xf, _ = _mix(xf, mix_norm, o_groups, s_groups, o_b, w_in_l, gate_off,
                     w_att_l, w_hgrn_l, w_mix_l, l)

        k_x, v_x = _mem_kv(memf, mem_norm, wkv_l, l)
        xf, (ffn2_gu_l, ffn2_down_l) = _xattn(xf, xattn_norm, k_x, v_x, wq_l, wo_l, l, batch,
                                              casts=[(ffn2_w_gu, l), (ffn2_w_down, l)])

        last = l == depth - 1
        xf, ffn1_w = _ffn(xf, ffn2_norm, ffn2_gu_l, ffn2_down_l, l,
                          final_g=final_norm.reshape(1, -1) if last else None,
                          casts=[] if last else [(ffn1_w_gu, l + 1), (ffn1_w_down, l + 1)])

    return xf.reshape(batch, seq, d)
```

```python
import functools
import math

import jax
import jax.numpy as jnp
from jax import lax
from jax.experimental import pallas as pl
from jax.experimental.pallas import tpu as pltpu

F32 = jnp.float32
BF16 = jnp.bfloat16

ATT_GROUPS = ((128, 1), (512, 4), (2048, 16))
ATT_HEADS = 4
ATT_HEAD_DIM = 128
ATT_GROUP_WIDTH = ATT_HEADS * ATT_HEAD_DIM
HGRN_HEAD_DIM = 128
X_HEADS = 4
ROPE_THETA = 10000.0
EPS = 1e-6

LANES = 128
SUBLANES = 8
MXU_TILE = 256
VMEM_LIMIT_BYTES = 56 * 1024 * 1024

TOKEN_TILE = 1024
FFN_CHUNK_TILES = 3
MXU_PERMUTE_MIN_STRIDE = 8
ATT_BLOCK = 128
ATT_CHAINS = 8
HGRN_CHUNK = 128
HGRN_STEP_CHUNKS = 2
HGRN_SAFE_EXPONENT = 115.0
NEG_BIG = -1e30


def _params(*sem):
    return pltpu.CompilerParams(dimension_semantics=sem,
                                vmem_limit_bytes=VMEM_LIMIT_BYTES)


def _rms(x, g):
    ms = jnp.mean(x * x, axis=-1, keepdims=True)
    return x * lax.rsqrt(ms + EPS) * g


def _dot(a, b):
    return jnp.dot(a, b, preferred_element_type=F32)


def _dot_nt(a, b):
    return lax.dot_general(a, b, (((1,), (1,)), ((), ())), preferred_element_type=F32)


def _dot_tn(a, b):
    return lax.dot_general(a, b, (((0,), (0,)), ((), ())), preferred_element_type=F32)


def _resident(a):
    return pl.BlockSpec(a.shape, lambda *_: (0,) * a.ndim, pipeline_mode=pl.Buffered(1))


def _layer_block(a, layer):
    _, rows, width = a.shape
    return pl.BlockSpec((None, rows, width), lambda *_: (layer, 0, 0),
                        pipeline_mode=pl.Buffered(1))


def _col_block(a, off, size):
    return pl.BlockSpec((pl.Element(a.shape[0]), pl.Element(size)), lambda *_: (0, off),
                        pipeline_mode=pl.Buffered(1))


def _pallas(body, *, name, grid, sem, in_specs, args, out_shape, out_specs, scratch=(),
            casts=(), step_of=None):
    single = not isinstance(out_shape, (tuple, list))
    out_shape = [out_shape] if single else list(out_shape)
    out_specs = [out_specs] if single else list(out_specs)
    n_in, n_out, n_cast = len(in_specs), len(out_shape), len(casts)
    steps = math.prod(grid)
    cast_in, cast_out, cast_shape = [], [], []
    for a, layer in casts:
        _, rows, cols = a.shape
        chunk = rows // steps
        assert chunk * steps == rows and chunk % (2 * SUBLANES) == 0
        cast_in.append(pl.BlockSpec((None, chunk, cols),
                                    lambda *g, layer=layer: (layer, step_of(*g), 0)))
        cast_out.append(pl.BlockSpec((chunk, cols), lambda *g: (step_of(*g), 0)))
        cast_shape.append(jax.ShapeDtypeStruct((rows, cols), BF16))

    def hosted(*refs):
        ins = refs[:n_in]
        srcs = refs[n_in:n_in + n_cast]
        outs = refs[n_in + n_cast:n_in + n_cast + n_out]
        dsts = refs[n_in + n_cast + n_out:n_in + 2 * n_cast + n_out]
        for src, dst in zip(srcs, dsts):
            dst[...] = src[...].astype(BF16)
        body(*ins, *outs, *refs[n_in + 2 * n_cast + n_out:])

    res = pl.pallas_call(
        hosted,
        out_shape=tuple(out_shape + cast_shape),
        grid=grid,
        in_specs=list(in_specs) + cast_in,
        out_specs=tuple(out_specs + cast_out),
        scratch_shapes=list(scratch),
        compiler_params=_params(*sem),
        name=name,
    )(*args, *[a for a, _ in casts])
    main = res[:n_out]
    return (main[0] if single else tuple(main)), list(res[n_out:])


def _rope_table_kernel(pos_a_ref, pos_b_ref, cos_ref, sin_ref):
    half = ATT_HEAD_DIM // 2
    rows = pos_a_ref.shape[0]
    lane = lax.broadcasted_iota(jnp.int32, (1, ATT_HEAD_DIM), 1)
    upper = lane >= half
    idx = jnp.where(upper, lane - half, lane).astype(F32)
    inv_freq = jnp.exp(idx * (-2.0 * math.log(ROPE_THETA) / ATT_HEAD_DIM))
    ang = jnp.where(upper, pos_b_ref[...], pos_a_ref[...]) * inv_freq
    sign = jnp.where(upper, 1.0, -1.0)
    for val, out_ref, mult in ((jnp.cos(ang), cos_ref, 1.0), (jnp.sin(ang), sin_ref, sign)):
        swapped = pltpu.roll(val, half, 1)
        out_ref[0:rows, :] = jnp.where(upper, swapped, val) * mult
        out_ref[rows:2 * rows, :] = jnp.where(upper, val, swapped) * mult


def _rope_tables(pos_col):
    n = pos_col.shape[0]
    tm = min(n, 2048)
    out = jax.ShapeDtypeStruct((n, ATT_HEAD_DIM), F32)
    return pl.pallas_call(
        _rope_table_kernel,
        out_shape=(out, out),
        grid=(n // tm,),
        in_specs=[pl.BlockSpec((tm // 2, 1), lambda i: (2 * i, 0)),
                  pl.BlockSpec((tm // 2, 1), lambda i: (2 * i + 1, 0))],
        out_specs=(pl.BlockSpec((tm, ATT_HEAD_DIM), lambda i: (i, 0)),) * 2,
        compiler_params=_params("parallel"),
        name="rope_tables",
    )(pos_col, pos_col)


def _ffn_chunks(d_ff):
    assert d_ff % MXU_TILE == 0
    tiles = d_ff // MXU_TILE
    n_chunks = -(-tiles // FFN_CHUNK_TILES)
    bounds = [MXU_TILE * (tiles * k // n_chunks) for k in range(n_chunks + 1)]
    return list(zip(bounds[:-1], bounds[1:]))


def _ffn_kernel(*refs, final):
    if final:
        x_ref, g_ref, wg_ref, wu_ref, wd_ref, fg_ref, o_ref = refs
    else:
        x_ref, g_ref, wg_ref, wu_ref, wd_ref, o_ref = refs
    x = x_ref[...]
    h = _rms(x, g_ref[...]).astype(BF16)
    acc = None
    for lo, hi in _ffn_chunks(wg_ref.shape[1]):
        gate = _dot(h, wg_ref[:, lo:hi])
        up = _dot(h, wu_ref[:, lo:hi])
        act = (gate * jax.nn.sigmoid(gate) * up).astype(BF16)
        part = _dot(act, wd_ref[lo:hi, :])
        acc = part if acc is None else acc + part
    y = x + 0.5 * acc
    if final:
        y = _rms(y, fg_ref[...])
    o_ref[...] = y


def _ffn(x, norm, w_gu, w_down, layer, final_g=None, casts=()):
    n, d = x.shape
    d_ff = w_down.shape[0]
    tm = min(TOKEN_TILE, n)
    final = final_g is not None
    row = pl.BlockSpec((tm, d), lambda i: (i, 0))
    in_specs = [row, _layer_block(norm, layer), _col_block(w_gu, 0, d_ff),
                _col_block(w_gu, d_ff, d_ff), _resident(w_down)]
    args = [x, norm, w_gu, w_gu, w_down]
    if final:
        in_specs.append(_resident(final_g))
        args.append(final_g)
    return _pallas(
        functools.partial(_ffn_kernel, final=final),
        name="ffn_final" if final else "ffn",
        grid=(n // tm,), sem=("parallel",),
        in_specs=in_specs, args=args,
        out_shape=jax.ShapeDtypeStruct((n, d), F32), out_specs=row,
        casts=casts, step_of=lambda i: i,
    )


def _class_permutation(dil):
    per = MXU_TILE // dil
    i = lax.broadcasted_iota(jnp.int32, (MXU_TILE, MXU_TILE), 0)
    j = lax.broadcasted_iota(jnp.int32, (MXU_TILE, MXU_TILE), 1)
    return jnp.where(j == (i % per) * dil + i // per, 1.0, 0.0).astype(BF16)


def _store_by_class(out_ref, scr_ref, val, dil):
    if dil == 1:
        out_ref[0] = val.astype(out_ref.dtype)
        return
    if dil < MXU_PERMUTE_MIN_STRIDE:
        rows = val.shape[0] // dil
        for c in range(val.shape[1] // LANES):
            cols = slice(c * LANES, (c + 1) * LANES)
            scr_ref[c] = val[:, cols]
            for r in range(dil):
                out_ref[r, :, cols] = (
                    scr_ref[c, pl.ds(r, rows, stride=dil), :].astype(out_ref.dtype))
        return
    val = val.astype(out_ref.dtype)
    perm = _class_permutation(dil)
    per = MXU_TILE // dil
    for s in range(val.shape[0] // MXU_TILE):
        y = _dot(perm, val[s * MXU_TILE:(s + 1) * MXU_TILE, :]).astype(out_ref.dtype)
        for r in range(dil):
            out_ref[r, s * per:(s + 1) * per, :] = y[r * per:(r + 1) * per, :]


def _proj_attn_kernel(x_ref, g_ref, cos_ref, sin_ref, w_ref, *rest, dils, fused, strided):
    n_g = len(dils)
    gw = ATT_GROUP_WIDTH
    n_out = sum(2 if f else 3 for f in fused)
    outs, scr = list(rest[:n_out]), list(rest[n_out:])
    stage = [scr.pop(0) for _ in range(3)] if strided else [None] * 3
    first = pl.program_id(1) == 0
    h = _rms(x_ref[...], g_ref[...]).astype(BF16)
    cos = cos_ref[...]
    sin = sin_ref[...]
    scale = ATT_HEAD_DIM ** -0.5 * math.log2(math.e)
    for gi, dil in enumerate(dils):
        part = lambda p: w_ref[:, (p * n_g + gi) * gw:(p * n_g + gi + 1) * gw]
        zq = _dot(h, part(0))
        zk = _dot(h, part(1))
        rq, rk = [], []
        for hd in range(ATT_HEADS):
            sl = slice(hd * ATT_HEAD_DIM, (hd + 1) * ATT_HEAD_DIM)
            xq = zq[:, sl]
            xk = zk[:, sl]
            rq.append((xq * cos + pltpu.roll(xq, ATT_HEAD_DIM // 2, 1) * sin) * scale)
            rk.append(xk * cos + pltpu.roll(xk, ATT_HEAD_DIM // 2, 1) * sin)
        if fused[gi]:
            o_ref, st_ref = outs.pop(0), outs.pop(0)
            q_ref, k_ref, v_ref, kp_ref, vp_ref = (scr.pop(0) for _ in range(5))
        else:
            q_ref, k_ref, v_ref = outs.pop(0), outs.pop(0), outs.pop(0)
        _store_by_class(q_ref, stage[0], jnp.concatenate(rq, axis=-1), dil)
        _store_by_class(k_ref, stage[1], jnp.concatenate(rk, axis=-1), dil)
        _store_by_class(v_ref, stage[2], _dot(h, part(2)), dil)
        if fused[gi]:
            @pl.when(first)
            def _():
                kp_ref[...] = jnp.zeros_like(kp_ref)
                vp_ref[...] = jnp.zeros_like(vp_ref)

            rows = q_ref.shape[1]
            _attn_window(q_ref, kp_ref, k_ref, vp_ref, v_ref, o_ref, st_ref,
                         first, dil, rows // ATT_BLOCK)
            kp_ref[...] = k_ref[:, rows - ATT_BLOCK:, :]
            vp_ref[...] = v_ref[:, rows - ATT_BLOCK:, :]


def _proj_attn(x, norm, cos2, sin2, w_in, layer, batch, casts=()):
    n, d = x.shape
    seq = n // batch
    tm = min(TOKEN_TILE, seq)
    per_b = seq // tm
    gw = ATT_GROUP_WIDTH
    dils = tuple(dil for _, dil in ATT_GROUPS)
    fused = tuple((tm // dil) % ATT_BLOCK == 0 for dil in dils)
    strided = any(1 < dil < MXU_PERMUTE_MIN_STRIDE for dil in dils)
    row = lambda b, i: (b * per_b + i, 0)
    by_class = lambda dil, width: pl.BlockSpec((None, dil, tm // dil, width),
                                               lambda b, i: (b, 0, i, 0))
    out_shape, out_specs = [], []
    scratch = [pltpu.VMEM((gw // LANES, tm, LANES), F32)] * 3 if strided else []
    for dil, f in zip(dils, fused):
        act = jax.ShapeDtypeStruct((batch, dil, seq // dil, gw), BF16)
        if f:
            out_shape += [act, jax.ShapeDtypeStruct((batch, dil, seq // dil, LANES), F32)]
            out_specs += [by_class(dil, gw), by_class(dil, LANES)]
            scratch += [pltpu.VMEM((dil, tm // dil, gw), BF16)] * 3
            scratch += [pltpu.VMEM((dil, ATT_BLOCK, gw), BF16)] * 2
        else:
            out_shape += [act] * 3
            out_specs += [by_class(dil, gw)] * 3
    outs, cast_res = _pallas(
        functools.partial(_proj_attn_kernel, dils=dils, fused=fused, strided=strided),
        name="proj_attn",
        grid=(batch, per_b), sem=("parallel", "arbitrary"),
        in_specs=[
            pl.BlockSpec((tm, d), row),
            _layer_block(norm, layer),
            pl.BlockSpec((tm, ATT_HEAD_DIM), row),
            pl.BlockSpec((tm, ATT_HEAD_DIM), row),
            _col_block(w_in, 0, 3 * len(dils) * gw),
        ],
        args=[x, norm, cos2, sin2, w_in],
        out_shape=out_shape, out_specs=out_specs, scratch=scratch,
        casts=casts, step_of=lambda b, i: b * per_b + i,
    )
    outs, groups = list(outs), []
    for f in fused:
        groups.append(("attn", outs.pop(0), outs.pop(0)) if f
                      else ("qkv", outs.pop(0), outs.pop(0), outs.pop(0)))
    return groups, cast_res


def _proj_hgrn_kernel(x_ref, g_ref, lbraw_ref, w_ref, q_ref, lf_ref, i_ref, og_ref, *, layer):
    h = _rms(x_ref[...], g_ref[...]).astype(BF16)
    raw = lbraw_ref[...]
    p = jnp.exp(raw - jnp.max(raw, axis=0, keepdims=True))
    p = p / jnp.sum(p, axis=0, keepdims=True)
    lb = jnp.sum(p[:layer + 1], axis=0, keepdims=True) - p[0:1]

    hw = q_ref.shape[1]
    sec = lambda k: w_ref[:, k * hw:(k + 1) * hw]
    q_ref[...] = (_dot(h, sec(0)) * (HGRN_HEAD_DIM ** -0.5)).astype(BF16)
    f = lb + (1.0 - lb) * jax.nn.sigmoid(_dot(h, sec(1)))
    lf_ref[...] = jnp.log2(f)
    i_ref[...] = _dot(h, sec(2)).astype(BF16)
    og = _dot(h, sec(3))
    og_ref[...] = (og * jax.nn.sigmoid(og)).astype(BF16)


def _proj_hgrn(x, norm, lb_raw, w_in, col_off, layer, casts=()):
    n, d = x.shape
    tm = min(TOKEN_TILE, n)
    width = lb_raw.shape[1]
    row = lambda i: (i, 0)
    bf = jax.ShapeDtypeStruct((n, width), BF16)
    return _pallas(
        functools.partial(_proj_hgrn_kernel, layer=layer),
        name="proj_hgrn",
        grid=(n // tm,), sem=("parallel",),
        in_specs=[pl.BlockSpec((tm, d), row), _layer_block(norm, layer), _resident(lb_raw),
                  _col_block(w_in, col_off, 4 * width)],
        args=[x, norm, lb_raw, w_in],
        out_shape=(bf, jax.ShapeDtypeStruct((n, width), F32), bf, bf),
        out_specs=(pl.BlockSpec((tm, width), row),) * 4,
        casts=casts, step_of=lambda i: i,
    )


def _attn_kernel(q_ref, kp_ref, kc_ref, vp_ref, vc_ref, o_ref, st_ref, *, nres, nblk):
    _attn_window(q_ref, kp_ref, kc_ref, vp_ref, vc_ref, o_ref, st_ref,
                 pl.program_id(2) == 0, nres, nblk)


def _attn_window(q_ref, kp_ref, kc_ref, vp_ref, vc_ref, o_ref, st_ref, first, nres, nblk):
    row = lax.broadcasted_iota(jnp.int32, (ATT_BLOCK, 2 * ATT_BLOCK), 0)
    col = lax.broadcasted_iota(jnp.int32, (ATT_BLOCK, 2 * ATT_BLOCK), 1)
    band = jnp.logical_and(col >= row, col <= row + ATT_BLOCK)
    low = jnp.where(first, ATT_BLOCK, 0)
    band_first = jnp.logical_and(col >= jnp.maximum(row, low), col <= row + ATT_BLOCK)
    ones = jnp.ones((2 * ATT_BLOCK, ATT_HEAD_DIM), BF16)
    lane = lax.broadcasted_iota(jnp.int32, (ATT_BLOCK, LANES), 1)
    for res in range(nres):
        for blk in range(nblk):
            rows = slice(blk * ATT_BLOCK, (blk + 1) * ATT_BLOCK)
            stats = jnp.ones((ATT_BLOCK, LANES), F32)
            for hd in range(ATT_HEADS):
                sl = slice(hd * ATT_HEAD_DIM, (hd + 1) * ATT_HEAD_DIM)
                if blk == 0:
                    k2 = jnp.concatenate([kp_ref[res, :, sl], kc_ref[res, 0:ATT_BLOCK, sl]], axis=0)
                    v2 = jnp.concatenate([vp_ref[res, :, sl], vc_ref[res, 0:ATT_BLOCK, sl]], axis=0)
                    mask = band_first
                else:
                    win = slice((blk - 1) * ATT_BLOCK, (blk + 1) * ATT_BLOCK)
                    k2 = kc_ref[res, win, sl]
                    v2 = vc_ref[res, win, sl]
                    mask = band
                s = jnp.where(mask, _dot_nt(q_ref[res, rows, sl], k2), NEG_BIG)
                m = jnp.max(s, axis=-1, keepdims=True)
                p = jnp.exp2(s - m).astype(BF16)
                o_den = _dot(p, jnp.concatenate([v2, ones], axis=1))
                den = o_den[:, ATT_HEAD_DIM:]
                o_ref[res, rows, sl] = o_den[:, :ATT_HEAD_DIM].astype(BF16)
                stats = jnp.where(lane == hd, m, jnp.where(lane == ATT_HEADS + hd, den, stats))
            st_ref[res, rows, :] = stats


def _attn_group(q, k, v):
    batch, dil, cls, gw = q.shape
    assert cls % ATT_BLOCK == 0
    nblk = min(ATT_CHAINS, cls // ATT_BLOCK)
    nres = max(1, min(dil, ATT_CHAINS // nblk))
    qb = nblk * ATT_BLOCK
    cur = lambda b, r, i: (b, r, i, 0)
    prev = lambda b, r, i: (b, r, jnp.maximum(i * nblk - 1, 0), 0)
    cur_spec = pl.BlockSpec((None, nres, qb, gw), cur)
    prev_spec = pl.BlockSpec((None, nres, ATT_BLOCK, gw), prev)
    return pl.pallas_call(
        functools.partial(_attn_kernel, nres=nres, nblk=nblk),
        out_shape=(jax.ShapeDtypeStruct((batch, dil, cls, gw), BF16),
                   jax.ShapeDtypeStruct((batch, dil, cls, LANES), F32)),
        grid=(batch, dil // nres, cls // qb),
        in_specs=[cur_spec, prev_spec, cur_spec, prev_spec, cur_spec],
        out_specs=(pl.BlockSpec((None, nres, qb, gw), cur),
                   pl.BlockSpec((None, nres, qb, LANES), cur)),
        compiler_params=_params("parallel", "parallel", "arbitrary"),
        name=f"attn_dil{dil}",
    )(q, k, k, v, v)


def _hgrn_decay_ahead(lf_ref, b_ref, kk_ref, reach_ref, slot, c):
    r_i = lax.broadcasted_iota(jnp.int32, (c, c), 0)
    c_i = lax.broadcasted_iota(jnp.int32, (c, c), 1)
    tri = jnp.where(c_i <= r_i, 1.0, 0.0).astype(BF16)
    reach = None
    for lo in range(0, lf_ref.shape[0], c):
        lf = lf_ref[lo:lo + c, :]
        lf_hi = lf.astype(BF16)
        lf_lo = (lf - lf_hi.astype(F32)).astype(BF16)
        b = _dot(tri, lf_hi) + _dot(tri, lf_lo)
        b_ref[slot, lo:lo + c, :] = b
        kk_ref[slot, lo:lo + c, :] = 1.0 - jnp.exp2(lf)
        b_mid = b[c // 2 - 1:c // 2, :]
        r = jnp.max(jnp.maximum(b[0:1, :] - b_mid, b_mid - b[c - 1:c, :]))
        reach = r if reach is None else jnp.maximum(reach, r)
    reach_ref[slot] = reach


def _hgrn_scores_midref(q, kk, b, heads):
    c = b.shape[0]
    b_mid = b[c // 2 - 1:c // 2, :]
    b_last = b[c - 1:c, :]
    q_s = q * jnp.exp2(b - b_mid).astype(BF16)
    k_s = (kk * jnp.exp2(b_mid - b)).astype(BF16)
    q_in = q_s * jnp.exp2(b_mid).astype(BF16)
    k_out = k_s * jnp.exp2(b_last - b_mid).astype(BF16)
    r_i = lax.broadcasted_iota(jnp.int32, (c, c), 0)
    c_i = lax.broadcasted_iota(jnp.int32, (c, c), 1)
    causal = c_i <= r_i
    scores = []
    for hd in range(heads):
        sl = slice(hd * HGRN_HEAD_DIM, (hd + 1) * HGRN_HEAD_DIM)
        scores.append(jnp.where(causal, _dot_nt(q_s[:, sl], k_s[:, sl]), 0.0).astype(BF16))
    return scores, q_in, k_out


def _hgrn_scores_bounded(q, kk, b, heads):
    c, w = b.shape
    q = q.astype(F32)
    q_in = (q * jnp.exp2(b)).astype(BF16)
    k_out = (kk * jnp.exp2(b[c - 1:c, :] - b)).astype(BF16)
    r_i = lax.broadcasted_iota(jnp.int32, (c, c), 0)
    c_i = lax.broadcasted_iota(jnp.int32, (c, c), 1)
    rows = lax.broadcasted_iota(jnp.int32, (c, w), 0)

    levels = []
    half = SUBLANES
    while half < c:
        span = 2 * half
        pieces = [jnp.broadcast_to(b[s + half - 1:s + half, :], (span, w))
                  for s in range(0, c, span)]
        bref = pieces[0] if len(pieces) == 1 else jnp.concatenate(pieces, axis=0)
        upper = (rows & (span - 1)) >= half
        e = jnp.exp2(jnp.where(upper, b - bref, bref - b))
        q_up = jnp.where(upper, q * e, 0.0).astype(BF16)
        k_lo = jnp.where(upper, 0.0, kk * e).astype(BF16)
        same = (r_i & ~(span - 1)) == (c_i & ~(span - 1))
        levels.append((q_up, k_lo, same))
        half = span

    nb = c // SUBLANES
    sub = lax.broadcasted_iota(jnp.int32, (nb, SUBLANES, w), 1)
    q3 = q.reshape(nb, SUBLANES, w)
    k3 = kk.reshape(nb, SUBLANES, w)
    b3 = b.reshape(nb, SUBLANES, w)
    near = []
    for dist in range(SUBLANES):
        if dist == 0:
            prod = q3 * k3
        else:
            k_r = pltpu.roll(k3, dist, 1)
            b_r = pltpu.roll(b3, dist, 1)
            prod = jnp.where(sub >= dist, q3 * k_r * jnp.exp2(b3 - b_r), 0.0)
        near.append(prod.reshape(c, w))

    scores = []
    for hd in range(heads):
        sl = slice(hd * HGRN_HEAD_DIM, (hd + 1) * HGRN_HEAD_DIM)
        a = jnp.zeros((c, c), F32)
        for q_up, k_lo, same in levels:
            a = a + jnp.where(same, _dot_nt(q_up[:, sl], k_lo[:, sl]), 0.0)
        for dist in range(SUBLANES):
            col_val = jnp.sum(near[dist][:, sl], axis=-1, keepdims=True)
            a = a + jnp.where(c_i == r_i - dist, col_val, 0.0)
        scores.append(a.astype(BF16))
    return scores, q_in, k_out


def _hgrn_chunk(q_ref, i_ref, og_ref, hn_ref, o_ref, st_ref, b, kk, rows, score_fn, heads):
    c = b.shape[0]
    v_bf = i_ref[rows, :]
    scores, q_in, k_out = score_fn(q_ref[rows, :], kk, b, heads)
    st_decay = jnp.exp2(b[c - 1:c, :])
    gain = hn_ref[...]
    for hd in range(heads):
        sl = slice(hd * HGRN_HEAD_DIM, (hd + 1) * HGRN_HEAD_DIM)
        v_h = v_bf[:, sl]
        st_h = st_ref[hd]
        o_h = _dot(scores[hd], v_h) + _dot_nt(q_in[:, sl], st_h.astype(BF16))
        st_ref[hd] = st_h * st_decay[:, sl] + _dot_tn(v_h, k_out[:, sl])
        o_n = _rms(o_h, gain[:, sl])
        o_ref[rows, sl] = (o_n * og_ref[rows, sl].astype(F32)).astype(BF16)


def _hgrn_kernel(q_ref, lf_ref, i_ref, og_ref, hn_ref, o_ref,
                 st_ref, b_ref, kk_ref, reach_ref, *, chunk, heads):
    t = pl.program_id(1)
    ahead = t % 2
    ready = 1 - ahead

    @pl.when(t == 0)
    def _():
        st_ref[...] = jnp.zeros_like(st_ref)
        reach_ref[ready] = 0.0
        _hgrn_decay_ahead(lf_ref, b_ref, kk_ref, reach_ref, ahead, chunk)

    midref_ok = reach_ref[ready] < HGRN_SAFE_EXPONENT
    for take, score_fn in ((midref_ok, _hgrn_scores_midref),
                           (jnp.logical_not(midref_ok), _hgrn_scores_bounded)):
        @pl.when(jnp.logical_and(t > 0, take))
        def _():
            for lo in range(0, q_ref.shape[0], chunk):
                rows = slice(lo, lo + chunk)
                _hgrn_chunk(q_ref, i_ref, og_ref, hn_ref, o_ref, st_ref,
                            b_ref[ready, rows, :], kk_ref[ready, rows, :], rows, score_fn, heads)
            _hgrn_decay_ahead(lf_ref, b_ref, kk_ref, reach_ref, ahead, chunk)


def _hgrn(q, lf, i, og, head_gain, layer, batch):
    n, w = q.shape
    seq = n // batch
    c = min(HGRN_CHUNK, seq)
    heads = w // HGRN_HEAD_DIM
    rows = min(HGRN_STEP_CHUNKS * c, seq)
    ns = seq // rows
    behind = lambda b, t: (b * ns + jnp.maximum(t - 1, 0), 0)
    upcoming = lambda b, t: (b * ns + jnp.minimum(t, ns - 1), 0)
    spec = pl.BlockSpec((rows, w), behind)
    return pl.pallas_call(
        functools.partial(_hgrn_kernel, chunk=c, heads=heads),
        out_shape=jax.ShapeDtypeStruct((n, w), BF16),
        grid=(batch, ns + 1),
        in_specs=[spec, pl.BlockSpec((rows, w), upcoming), spec, spec,
                  _layer_block(head_gain, layer)],
        out_specs=spec,
        scratch_shapes=[pltpu.VMEM((heads, HGRN_HEAD_DIM, HGRN_HEAD_DIM), F32),
                        pltpu.VMEM((2, rows, w), F32),
                        pltpu.VMEM((2, rows, w), F32),
                        pltpu.SMEM((2,), F32)],
        compiler_params=_params("parallel", "arbitrary"),
        name="hgrn",
    )(q, lf, i, og, head_gain)


def _load_by_token(src_ref, scr_ref):
    dil, rows, width = src_ref.shape
    if dil == 1:
        return src_ref[0].astype(F32)
    parts = []
    for c in range(width // LANES):
        cols = slice(c * LANES, (c + 1) * LANES)
        for r in range(dil):
            scr_ref[c, pl.ds(r, rows, stride=dil), :] = src_ref[r, :, cols].astype(F32)
        parts.append(scr_ref[c])
    return parts[0] if len(parts) == 1 else jnp.concatenate(parts, axis=-1)


def _mix_kernel(x_ref, g_ref, o0_ref, o1_ref, o2_ref, s0_ref, s1_ref, s2_ref, ob_ref,
                wg_ref, wa_ref, wb_ref, wo_ref, out_ref, *scr):
    x = x_ref[...]
    d = x.shape[1]
    h = _rms(x, g_ref[...]).astype(BF16)
    gate_a = jax.nn.sigmoid(_dot(h, wg_ref[:, :d]))
    gate_b = jax.nn.sigmoid(_dot(h, wg_ref[:, d:]))

    scr = list(scr)
    take = lambda ref: _load_by_token(ref, scr.pop(0) if ref.shape[0] > 1 else None)
    o0, o1, o2 = take(o0_ref), take(o1_ref), take(o2_ref)
    s0, s1, s2 = take(s0_ref), take(s1_ref), take(s2_ref)
    head_lane = lax.broadcasted_iota(jnp.int32, s0.shape, 1) < ATT_HEADS
    lse = [s + jnp.log2(jnp.where(head_lane, pltpu.roll(s, LANES - ATT_HEADS, 1), 1.0))
           for s in (s0, s1, s2)]
    mx = jnp.maximum(jnp.maximum(lse[0], lse[1]), lse[2])
    inv = 1.0 / (jnp.exp2(lse[0] - mx) + jnp.exp2(lse[1] - mx) + jnp.exp2(lse[2] - mx))
    w0, w1, w2 = (jnp.exp2(s - mx) * inv for s in (s0, s1, s2))
    parts = []
    for hd in range(ATT_HEADS):
        sl = slice(hd * ATT_HEAD_DIM, (hd + 1) * ATT_HEAD_DIM)
        parts.append(o0[:, sl] * w0[:, hd:hd + 1] + o1[:, sl] * w1[:, hd:hd + 1]
                     + o2[:, sl] * w2[:, hd:hd + 1])
    att = jnp.concatenate(parts, axis=-1).astype(BF16)

    y_a = _dot(att, wa_ref[...])
    y_b = _dot(ob_ref[...], wb_ref[...])
    merged = (gate_a * y_a + gate_b * y_b).astype(BF16)
    out_ref[...] = x + _dot(merged, wo_ref[...])


def _mix(x, norm, o_groups, s_groups, ob, w_in, gate_off, wa, wb, wo, layer, casts=()):
    n, d = x.shape
    batch = o_groups[0].shape[0]
    seq = n // batch
    tm = min(TOKEN_TILE, seq)
    per_b = seq // tm
    row = lambda b, i: (b * per_b + i, 0)
    rows = lambda a: pl.BlockSpec((tm, a.shape[1]), row)
    by_class = lambda a: pl.BlockSpec((None, a.shape[1], tm // a.shape[1], a.shape[3]),
                                      lambda b, i: (b, 0, i, 0))
    grouped = [*o_groups, *s_groups]
    args = [x, norm, *grouped, ob, w_in, wa, wb, wo]
    in_specs = ([rows(x), _layer_block(norm, layer)] + [by_class(a) for a in grouped]
                + [rows(ob), _col_block(w_in, gate_off, 2 * d),
                   _resident(wa), _resident(wb), _resident(wo)])
    scratch = [pltpu.VMEM((a.shape[3] // LANES, tm, LANES), F32)
               for a in grouped if a.shape[1] > 1]
    return _pallas(
        _mix_kernel,
        name="mix_out",
        grid=(batch, per_b), sem=("parallel", "parallel"),
        in_specs=in_specs, args=args,
        out_shape=jax.ShapeDtypeStruct((n, d), F32), out_specs=pl.BlockSpec((tm, d), row),
        scratch=scratch,
        casts=casts, step_of=lambda b, i: b * per_b + i,
    )


def _mem_kv_kernel(m_ref, g_ref, wk_ref, wv_ref, k_ref, v_ref):
    h = _rms(m_ref[...], g_ref[...]).astype(BF16)
    k_ref[...] = _dot(h, wk_ref[...]).astype(BF16)
    v_ref[...] = _dot(h, wv_ref[...]).astype(BF16)


def _mem_kv(mem, norm, wkv, layer):
    n, d = mem.shape
    tm = min(TOKEN_TILE, n)
    width = wkv.shape[1] // 2
    row = lambda i: (i, 0)
    out = jax.ShapeDtypeStruct((n, width), BF16)
    return pl.pallas_call(
        _mem_kv_kernel,
        out_shape=(out, out),
        grid=(n // tm,),
        in_specs=[pl.BlockSpec((tm, d), row), _layer_block(norm, layer),
                  _col_block(wkv, 0, width), _col_block(wkv, width, width)],
        out_specs=(pl.BlockSpec((tm, width), row),) * 2,
        compiler_params=_params("parallel"),
        name="mem_kv",
    )(mem, norm, wkv, wkv)


def _xattn_kernel(x_ref, g_ref, k_ref, v_ref, wq_ref, wo_ref, out_ref):
    x = x_ref[...]
    h = _rms(x, g_ref[...]).astype(BF16)
    dh = wq_ref.shape[1] // X_HEADS
    q = (_dot(h, wq_ref[...]) * (dh ** -0.5)).astype(BF16)
    outs = []
    for hd in range(X_HEADS):
        sl = slice(hd * dh, (hd + 1) * dh)
        s = _dot_nt(q[:, sl], k_ref[:, sl])
        m = jnp.max(s, axis=-1, keepdims=True)
        p = jnp.exp(s - m)
        den = jnp.sum(p, axis=-1, keepdims=True)
        outs.append(_dot(p.astype(BF16), v_ref[:, sl]) / den)
    o = jnp.concatenate(outs, axis=-1).astype(BF16)
    out_ref[...] = x + _dot(o, wo_ref[...])


def _xattn(x, norm, k, v, wq, wo, layer, batch, casts=()):
    n, d = x.shape
    seq = n // batch
    tm = min(TOKEN_TILE, seq)
    n_mem = k.shape[0] // batch
    per_b = seq // tm
    row = lambda b, i: (b * per_b + i, 0)
    kv = pl.BlockSpec((n_mem, k.shape[1]), lambda b, i: (b, 0))
    return _pallas(
        _xattn_kernel,
        name="xattn",
        grid=(batch, per_b), sem=("parallel", "parallel"),
        in_specs=[pl.BlockSpec((tm, d), row), _layer_block(norm, layer), kv, kv,
                  _resident(wq), _resident(wo)],
        args=[x, norm, k, v, wq, wo],
        out_shape=jax.ShapeDtypeStruct((n, d), F32), out_specs=pl.BlockSpec((tm, d), row),
        casts=casts, step_of=lambda b, i: b * per_b + i,
    )


def kernel(x, mem, positions, ffn1_norm, ffn1_w_gu, ffn1_w_down, mix_norm, w_in,
           hgrn_lower_bounds, hgrn_head_norm, w_att_branch, w_hgrn_branch, w_mix_out,
           xattn_norm, mem_norm, xattn_wq, xattn_wkv, xattn_wo,
           ffn2_norm, ffn2_w_gu, ffn2_w_down, final_norm):
    batch, seq, d = x.shape
    depth = w_in.shape[0]
    n = batch * seq
    att_w = len(ATT_GROUPS) * ATT_GROUP_WIDTH
    hg_w = hgrn_head_norm.shape[1]
    hgrn_off = 3 * att_w
    gate_off = hgrn_off + 4 * hg_w

    xf = x.reshape(n, d)
    memf = mem.reshape(batch * mem.shape[1], d)
    cos2, sin2 = _rope_tables(positions.reshape(n, 1).astype(F32))

    gain = lambda a: a.reshape(a.shape[0], 1, a.shape[1])
    ffn1_norm, mix_norm, hgrn_head_norm, xattn_norm, mem_norm, ffn2_norm = map(
        gain, (ffn1_norm, mix_norm, hgrn_head_norm, xattn_norm, mem_norm, ffn2_norm))

    ffn1_w = (ffn1_w_gu[0].astype(BF16), ffn1_w_down[0].astype(BF16))

    for l in range(depth):
        xf, (w_in_l,) = _ffn(xf, ffn1_norm, *ffn1_w, l, casts=[(w_in, l)])

        groups, (w_att_l, w_hgrn_l, w_mix_l) = _proj_attn(
            xf, mix_norm, cos2, sin2, w_in_l, l, batch,
            casts=[(w_att_branch, l), (w_hgrn_branch, l), (w_mix_out, l)])
        o_groups, s_groups = [], []
        for kind, *arrs in groups:
            o_g, s_g = arrs if kind == "attn" else _attn_group(*arrs)
            o_groups.append(o_g)
            s_groups.append(s_g)
        (q_b, lf_b, i_b, og_b), (wkv_l, wq_l, wo_l) = _proj_hgrn(
            xf, mix_norm, hgrn_lower_bounds, w_in_l, hgrn_off, l,
            casts=[(xattn_wkv, l), (xattn_wq, l), (xattn_wo, l)])
        o_b = _hgrn(q_b, lf_b, i_b, og_b, hgrn_head_norm, l, batch)
        xf, _ = _mix(xf, mix_norm, o_groups, s_groups, o_b, w_in_l, gate_off,
                     w_att_l, w_hgrn_l, w_mix_l, l)

        k_x, v_x = _mem_kv(memf, mem_norm, wkv_l, l)
        xf, (ffn2_gu_l, ffn2_down_l) = _xattn(xf, xattn_norm, k_x, v_x, wq_l, wo_l, l, batch,
                                              casts=[(ffn2_w_gu, l), (ffn2_w_down, l)])

        last = l == depth - 1
        xf, ffn1_w = _ffn(xf, ffn2_norm, ffn2_gu_l, ffn2_down_l, l,
                          final_g=final_norm.reshape(1, -1) if last else None,
                          casts=[] if last else [(ffn1_w_gu, l + 1), (ffn1_w_down, l + 1)])

    return xf.reshape(batch, seq, d)
```

```python
import functools
import math

import jax
import jax.numpy as jnp
from jax import lax
from jax.experimental import pallas as pl
from jax.experimental.pallas import tpu as pltpu

F32 = jnp.float32
BF16 = jnp.bfloat16

ATT_GROUPS = ((128, 1), (512, 4), (2048, 16))
ATT_HEADS = 4
ATT_HEAD_DIM = 128
ATT_GROUP_WIDTH = ATT_HEADS * ATT_HEAD_DIM
HGRN_HEAD_DIM = 128
X_HEADS = 4
ROPE_THETA = 10000.0
EPS = 1e-6

LANES = 128
SUBLANES = 8
MXU_TILE = 256
VMEM_LIMIT_BYTES = 56 * 1024 * 1024

TOKEN_TILE = 1024
FFN_CHUNK_TILES = 3
MXU_PERMUTE_MIN_STRIDE = 8
ATT_BLOCK = 128
ATT_CHAINS = 8
HGRN_CHUNK = 128
HGRN_SAFE_EXPONENT = 115.0
NEG_BIG = -1e30


def _params(*sem):
    return pltpu.CompilerParams(dimension_semantics=sem,
                                vmem_limit_bytes=VMEM_LIMIT_BYTES)


def _rms(x, g):
    ms = jnp.mean(x * x, axis=-1, keepdims=True)
    return x * lax.rsqrt(ms + EPS) * g


def _dot(a, b):
    return jnp.dot(a, b, preferred_element_type=F32)


def _dot_nt(a, b):
    return lax.dot_general(a, b, (((1,), (1,)), ((), ())), preferred_element_type=F32)


def _dot_tn(a, b):
    return lax.dot_general(a, b, (((0,), (0,)), ((), ())), preferred_element_type=F32)


def _resident(a):
    return pl.BlockSpec(a.shape, lambda *_: (0,) * a.ndim, pipeline_mode=pl.Buffered(1))


def _layer_block(a, layer):
    _, rows, width = a.shape
    return pl.BlockSpec((None, rows, width), lambda *_: (layer, 0, 0),
                        pipeline_mode=pl.Buffered(1))


def _col_block(a, off, size):
    return pl.BlockSpec((pl.Element(a.shape[0]), pl.Element(size)), lambda *_: (0, off),
                        pipeline_mode=pl.Buffered(1))


def _pallas(body, *, name, grid, sem, in_specs, args, out_shape, out_specs, scratch=(),
            casts=(), step_of=None):
    single = not isinstance(out_shape, (tuple, list))
    out_shape = [out_shape] if single else list(out_shape)
    out_specs = [out_specs] if single else list(out_specs)
    n_in, n_out, n_cast = len(in_specs), len(out_shape), len(casts)
    steps = math.prod(grid)
    cast_in, cast_out, cast_shape = [], [], []
    for a, layer in casts:
        _, rows, cols = a.shape
        chunk = rows // steps
        assert chunk * steps == rows and chunk % (2 * SUBLANES) == 0
        cast_in.append(pl.BlockSpec((None, chunk, cols),
                                    lambda *g, layer=layer: (layer, step_of(*g), 0)))
        cast_out.append(pl.BlockSpec((chunk, cols), lambda *g: (step_of(*g), 0)))
        cast_shape.append(jax.ShapeDtypeStruct((rows, cols), BF16))

    def hosted(*refs):
        ins = refs[:n_in]
        srcs = refs[n_in:n_in + n_cast]
        outs = refs[n_in + n_cast:n_in + n_cast + n_out]
        dsts = refs[n_in + n_cast + n_out:n_in + 2 * n_cast + n_out]
        for src, dst in zip(srcs, dsts):
            dst[...] = src[...].astype(BF16)
        body(*ins, *outs, *refs[n_in + 2 * n_cast + n_out:])

    res = pl.pallas_call(
        hosted,
        out_shape=tuple(out_shape + cast_shape),
        grid=grid,
        in_specs=list(in_specs) + cast_in,
        out_specs=tuple(out_specs + cast_out),
        scratch_shapes=list(scratch),
        compiler_params=_params(*sem),
        name=name,
    )(*args, *[a for a, _ in casts])
    main = res[:n_out]
    return (main[0] if single else tuple(main)), list(res[n_out:])


def _rope_table_kernel(pos_a_ref, pos_b_ref, cos_ref, sin_ref):
    half = ATT_HEAD_DIM // 2
    rows = pos_a_ref.shape[0]
    lane = lax.broadcasted_iota(jnp.int32, (1, ATT_HEAD_DIM), 1)
    upper = lane >= half
    idx = jnp.where(upper, lane - half, lane).astype(F32)
    inv_freq = jnp.exp(idx * (-2.0 * math.log(ROPE_THETA) / ATT_HEAD_DIM))
    ang = jnp.where(upper, pos_b_ref[...], pos_a_ref[...]) * inv_freq
    sign = jnp.where(upper, 1.0, -1.0)
    for val, out_ref, mult in ((jnp.cos(ang), cos_ref, 1.0), (jnp.sin(ang), sin_ref, sign)):
        swapped = pltpu.roll(val, half, 1)
        out_ref[0:rows, :] = jnp.where(upper, swapped, val) * mult
        out_ref[rows:2 * rows, :] = jnp.where(upper, val, swapped) * mult


def _rope_tables(pos_col):
    n = pos_col.shape[0]
    tm = min(n, 2048)
    out = jax.ShapeDtypeStruct((n, ATT_HEAD_DIM), F32)
    return pl.pallas_call(
        _rope_table_kernel,
        out_shape=(out, out),
        grid=(n // tm,),
        in_specs=[pl.BlockSpec((tm // 2, 1), lambda i: (2 * i, 0)),
                  pl.BlockSpec((tm // 2, 1), lambda i: (2 * i + 1, 0))],
        out_specs=(pl.BlockSpec((tm, ATT_HEAD_DIM), lambda i: (i, 0)),) * 2,
        compiler_params=_params("parallel"),
        name="rope_tables",
    )(pos_col, pos_col)


def _ffn_chunks(d_ff):
    assert d_ff % MXU_TILE == 0
    tiles = d_ff // MXU_TILE
    n_chunks = -(-tiles // FFN_CHUNK_TILES)
    bounds = [MXU_TILE * (tiles * k // n_chunks) for k in range(n_chunks + 1)]
    return list(zip(bounds[:-1], bounds[1:]))


def _ffn_kernel(*refs, final):
    if final:
        x_ref, g_ref, wg_ref, wu_ref, wd_ref, fg_ref, o_ref = refs
    else:
        x_ref, g_ref, wg_ref, wu_ref, wd_ref, o_ref = refs
    x = x_ref[...]
    h = _rms(x, g_ref[...]).astype(BF16)
    acc = None
    for lo, hi in _ffn_chunks(wg_ref.shape[1]):
        gate = _dot(h, wg_ref[:, lo:hi])
        up = _dot(h, wu_ref[:, lo:hi])
        act = (gate * jax.nn.sigmoid(gate) * up).astype(BF16)
        part = _dot(act, wd_ref[lo:hi, :])
        acc = part if acc is None else acc + part
    y = x + 0.5 * acc
    if final:
        y = _rms(y, fg_ref[...])
    o_ref[...] = y


def _ffn(x, norm, w_gu, w_down, layer, final_g=None, casts=()):
    n, d = x.shape
    d_ff = w_down.shape[0]
    tm = min(TOKEN_TILE, n)
    final = final_g is not None
    row = pl.BlockSpec((tm, d), lambda i: (i, 0))
    in_specs = [row, _layer_block(norm, layer), _col_block(w_gu, 0, d_ff),
                _col_block(w_gu, d_ff, d_ff), _resident(w_down)]
    args = [x, norm, w_gu, w_gu, w_down]
    if final:
        in_specs.append(_resident(final_g))
        args.append(final_g)
    return _pallas(
        functools.partial(_ffn_kernel, final=final),
        name="ffn_final" if final else "ffn",
        grid=(n // tm,), sem=("parallel",),
        in_specs=in_specs, args=args,
        out_shape=jax.ShapeDtypeStruct((n, d), F32), out_specs=row,
        casts=casts, step_of=lambda i: i,
    )


def _class_permutation(dil):
    per = MXU_TILE // dil
    i = lax.broadcasted_iota(jnp.int32, (MXU_TILE, MXU_TILE), 0)
    j = lax.broadcasted_iota(jnp.int32, (MXU_TILE, MXU_TILE), 1)
    return jnp.where(j == (i % per) * dil + i // per, 1.0, 0.0).astype(BF16)


def _store_by_class(out_ref, scr_ref, val, dil):
    if dil == 1:
        out_ref[0] = val.astype(out_ref.dtype)
        return
    if dil < MXU_PERMUTE_MIN_STRIDE:
        rows = val.shape[0] // dil
        for c in range(val.shape[1] // LANES):
            cols = slice(c * LANES, (c + 1) * LANES)
            scr_ref[c] = val[:, cols]
            for r in range(dil):
                out_ref[r, :, cols] = (
                    scr_ref[c, pl.ds(r, rows, stride=dil), :].astype(out_ref.dtype))
        return
    val = val.astype(out_ref.dtype)
    perm = _class_permutation(dil)
    per = MXU_TILE // dil
    for s in range(val.shape[0] // MXU_TILE):
        y = _dot(perm, val[s * MXU_TILE:(s + 1) * MXU_TILE, :]).astype(out_ref.dtype)
        for r in range(dil):
            out_ref[r, s * per:(s + 1) * per, :] = y[r * per:(r + 1) * per, :]


def _proj_attn_kernel(x_ref, g_ref, cos_ref, sin_ref, w_ref, *rest, dils, fused, strided):
    n_g = len(dils)
    gw = ATT_GROUP_WIDTH
    n_out = sum(2 if f else 3 for f in fused)
    outs, scr = list(rest[:n_out]), list(rest[n_out:])
    stage = [scr.pop(0) for _ in range(3)] if strided else [None] * 3
    first = pl.program_id(1) == 0
    h = _rms(x_ref[...], g_ref[...]).astype(BF16)
    cos = cos_ref[...]
    sin = sin_ref[...]
    scale = ATT_HEAD_DIM ** -0.5 * math.log2(math.e)
    for gi, dil in enumerate(dils):
        part = lambda p: w_ref[:, (p * n_g + gi) * gw:(p * n_g + gi + 1) * gw]
        zq = _dot(h, part(0))
        zk = _dot(h, part(1))
        rq, rk = [], []
        for hd in range(ATT_HEADS):
            sl = slice(hd * ATT_HEAD_DIM, (hd + 1) * ATT_HEAD_DIM)
            xq = zq[:, sl]
            xk = zk[:, sl]
            rq.append((xq * cos + pltpu.roll(xq, ATT_HEAD_DIM // 2, 1) * sin) * scale)
            rk.append(xk * cos + pltpu.roll(xk, ATT_HEAD_DIM // 2, 1) * sin)
        if fused[gi]:
            o_ref, st_ref = outs.pop(0), outs.pop(0)
            q_ref, k_ref, v_ref, kp_ref, vp_ref = (scr.pop(0) for _ in range(5))
        else:
            q_ref, k_ref, v_ref = outs.pop(0), outs.pop(0), outs.pop(0)
        _store_by_class(q_ref, stage[0], jnp.concatenate(rq, axis=-1), dil)
        _store_by_class(k_ref, stage[1], jnp.concatenate(rk, axis=-1), dil)
        _store_by_class(v_ref, stage[2], _dot(h, part(2)), dil)
        if fused[gi]:
            @pl.when(first)
            def _():
                kp_ref[...] = jnp.zeros_like(kp_ref)
                vp_ref[...] = jnp.zeros_like(vp_ref)

            rows = q_ref.shape[1]
            _attn_window(q_ref, kp_ref, k_ref, vp_ref, v_ref, o_ref, st_ref,
                         first, dil, rows // ATT_BLOCK)
            kp_ref[...] = k_ref[:, rows - ATT_BLOCK:, :]
            vp_ref[...] = v_ref[:, rows - ATT_BLOCK:, :]


def _proj_attn(x, norm, cos2, sin2, w_in, layer, batch, casts=()):
    n, d = x.shape
    seq = n // batch
    tm = min(TOKEN_TILE, seq)
    per_b = seq // tm
    gw = ATT_GROUP_WIDTH
    dils = tuple(dil for _, dil in ATT_GROUPS)
    fused = tuple((tm // dil) % ATT_BLOCK == 0 for dil in dils)
    strided = any(1 < dil < MXU_PERMUTE_MIN_STRIDE for dil in dils)
    row = lambda b, i: (b * per_b + i, 0)
    by_class = lambda dil, width: pl.BlockSpec((None, dil, tm // dil, width),
                                               lambda b, i: (b, 0, i, 0))
    out_shape, out_specs = [], []
    scratch = [pltpu.VMEM((gw // LANES, tm, LANES), F32)] * 3 if strided else []
    for dil, f in zip(dils, fused):
        act = jax.ShapeDtypeStruct((batch, dil, seq // dil, gw), BF16)
        if f:
            out_shape += [act, jax.ShapeDtypeStruct((batch, dil, seq // dil, LANES), F32)]
            out_specs += [by_class(dil, gw), by_class(dil, LANES)]
            scratch += [pltpu.VMEM((dil, tm // dil, gw), BF16)] * 3
            scratch += [pltpu.VMEM((dil, ATT_BLOCK, gw), BF16)] * 2
        else:
            out_shape += [act] * 3
            out_specs += [by_class(dil, gw)] * 3
    outs, cast_res = _pallas(
        functools.partial(_proj_attn_kernel, dils=dils, fused=fused, strided=strided),
        name="proj_attn",
        grid=(batch, per_b), sem=("parallel", "arbitrary"),
        in_specs=[
            pl.BlockSpec((tm, d), row),
            _layer_block(norm, layer),
            pl.BlockSpec((tm, ATT_HEAD_DIM), row),
            pl.BlockSpec((tm, ATT_HEAD_DIM), row),
            _col_block(w_in, 0, 3 * len(dils) * gw),
        ],
        args=[x, norm, cos2, sin2, w_in],
        out_shape=out_shape, out_specs=out_specs, scratch=scratch,
        casts=casts, step_of=lambda b, i: b * per_b + i,
    )
    outs, groups = list(outs), []
    for f in fused:
        groups.append(("attn", outs.pop(0), outs.pop(0)) if f
                      else ("qkv", outs.pop(0), outs.pop(0), outs.pop(0)))
    return groups, cast_res


def _hgrn_mixer_kernel(x_ref, g_ref, lbraw_ref, w_ref, hn_ref, o_ref,
                       q_ref, i_ref, og_ref, kk_ref, b_ref, st_ref, reach_ref,
                       *, layer, tiles_per_seq, chunk, heads):
    h = _rms(x_ref[...], g_ref[...]).astype(BF16)
    raw = lbraw_ref[...]
    p = jnp.exp(raw - jnp.max(raw, axis=0, keepdims=True))
    p = p / jnp.sum(p, axis=0, keepdims=True)
    lb = jnp.sum(p[:layer + 1], axis=0, keepdims=True) - p[0:1]

    hw = q_ref.shape[1]
    sec = lambda k: w_ref[:, k * hw:(k + 1) * hw]
    q_ref[...] = (_dot(h, sec(0)) * (HGRN_HEAD_DIM ** -0.5)).astype(BF16)
    f = lb + (1.0 - lb) * jax.nn.sigmoid(_dot(h, sec(1)))
    kk_ref[...] = 1.0 - f
    i_ref[...] = _dot(h, sec(2)).astype(BF16)
    og = _dot(h, sec(3))
    og_ref[...] = (og * jax.nn.sigmoid(og)).astype(BF16)
    _hgrn_decay(jnp.log2(f), b_ref, reach_ref, chunk)

    @pl.when(pl.program_id(0) % tiles_per_seq == 0)
    def _():
        st_ref[...] = jnp.zeros_like(st_ref)

    def one_chunk(ci, carry):
        rows = pl.ds(pl.multiple_of(ci * chunk, chunk), chunk)
        midref_ok = reach_ref[ci] < HGRN_SAFE_EXPONENT
        for take, score_fn in ((midref_ok, _hgrn_scores_midref),
                               (jnp.logical_not(midref_ok), _hgrn_scores_bounded)):
            @pl.when(take)
            def _():
                _hgrn_chunk(q_ref, i_ref, og_ref, hn_ref, o_ref, st_ref,
                            b_ref[rows, :], kk_ref[rows, :], rows, score_fn, heads)
        return carry

    lax.fori_loop(0, x_ref.shape[0] // chunk, one_chunk, 0)


def _hgrn_mixer(x, norm, lb_raw, w_in, col_off, head_gain, layer, batch, casts=()):
    n, d = x.shape
    seq = n // batch
    tm = min(TOKEN_TILE, seq)
    width = lb_raw.shape[1]
    chunk = min(HGRN_CHUNK, tm)
    heads = width // HGRN_HEAD_DIM
    row = lambda i: (i, 0)
    tile = lambda dtype: pltpu.VMEM((tm, width), dtype)
    return _pallas(
        functools.partial(_hgrn_mixer_kernel, layer=layer, tiles_per_seq=seq // tm,
                          chunk=chunk, heads=heads),
        name="hgrn_mixer",
        grid=(n // tm,), sem=("arbitrary",),
        in_specs=[pl.BlockSpec((tm, d), row), _layer_block(norm, layer), _resident(lb_raw),
                  _col_block(w_in, col_off, 4 * width), _layer_block(head_gain, layer)],
        args=[x, norm, lb_raw, w_in, head_gain],
        out_shape=jax.ShapeDtypeStruct((n, width), BF16),
        out_specs=pl.BlockSpec((tm, width), row),
        scratch=[tile(BF16), tile(BF16), tile(BF16), tile(F32), tile(F32),
                 pltpu.VMEM((heads, HGRN_HEAD_DIM, HGRN_HEAD_DIM), F32),
                 pltpu.SMEM((tm // chunk,), F32)],
        casts=casts, step_of=lambda i: i,
    )


def _attn_kernel(q_ref, kp_ref, kc_ref, vp_ref, vc_ref, o_ref, st_ref, *, nres, nblk):
    _attn_window(q_ref, kp_ref, kc_ref, vp_ref, vc_ref, o_ref, st_ref,
                 pl.program_id(2) == 0, nres, nblk)


def _attn_window(q_ref, kp_ref, kc_ref, vp_ref, vc_ref, o_ref, st_ref, first, nres, nblk):
    row = lax.broadcasted_iota(jnp.int32, (ATT_BLOCK, 2 * ATT_BLOCK), 0)
    col = lax.broadcasted_iota(jnp.int32, (ATT_BLOCK, 2 * ATT_BLOCK), 1)
    band = jnp.logical_and(col >= row, col <= row + ATT_BLOCK)
    low = jnp.where(first, ATT_BLOCK, 0)
    band_first = jnp.logical_and(col >= jnp.maximum(row, low), col <= row + ATT_BLOCK)
    ones = jnp.ones((2 * ATT_BLOCK, ATT_HEAD_DIM), BF16)
    lane = lax.broadcasted_iota(jnp.int32, (ATT_BLOCK, LANES), 1)
    for res in range(nres):
        for blk in range(nblk):
            rows = slice(blk * ATT_BLOCK, (blk + 1) * ATT_BLOCK)
            stats = jnp.ones((ATT_BLOCK, LANES), F32)
            for hd in range(ATT_HEADS):
                sl = slice(hd * ATT_HEAD_DIM, (hd + 1) * ATT_HEAD_DIM)
                if blk == 0:
                    k2 = jnp.concatenate([kp_ref[res, :, sl], kc_ref[res, 0:ATT_BLOCK, sl]], axis=0)
                    v2 = jnp.concatenate([vp_ref[res, :, sl], vc_ref[res, 0:ATT_BLOCK, sl]], axis=0)
                    mask = band_first
                else:
                    win = slice((blk - 1) * ATT_BLOCK, (blk + 1) * ATT_BLOCK)
                    k2 = kc_ref[res, win, sl]
                    v2 = vc_ref[res, win, sl]
                    mask = band
                s = jnp.where(mask, _dot_nt(q_ref[res, rows, sl], k2), NEG_BIG)
                m = jnp.max(s, axis=-1, keepdims=True)
                p = jnp.exp2(s - m).astype(BF16)
                o_den = _dot(p, jnp.concatenate([v2, ones], axis=1))
                den = o_den[:, ATT_HEAD_DIM:]
                o_ref[res, rows, sl] = o_den[:, :ATT_HEAD_DIM].astype(BF16)
                stats = jnp.where(lane == hd, m, jnp.where(lane == ATT_HEADS + hd, den, stats))
            st_ref[res, rows, :] = stats


def _attn_group(q, k, v):
    batch, dil, cls, gw = q.shape
    assert cls % ATT_BLOCK == 0
    nblk = min(ATT_CHAINS, cls // ATT_BLOCK)
    nres = max(1, min(dil, ATT_CHAINS // nblk))
    qb = nblk * ATT_BLOCK
    cur = lambda b, r, i: (b, r, i, 0)
    prev = lambda b, r, i: (b, r, jnp.maximum(i * nblk - 1, 0), 0)
    cur_spec = pl.BlockSpec((None, nres, qb, gw), cur)
    prev_spec = pl.BlockSpec((None, nres, ATT_BLOCK, gw), prev)
    return pl.pallas_call(
        functools.partial(_attn_kernel, nres=nres, nblk=nblk),
        out_shape=(jax.ShapeDtypeStruct((batch, dil, cls, gw), BF16),
                   jax.ShapeDtypeStruct((batch, dil, cls, LANES), F32)),
        grid=(batch, dil // nres, cls // qb),
        in_specs=[cur_spec, prev_spec, cur_spec, prev_spec, cur_spec],
        out_specs=(pl.BlockSpec((None, nres, qb, gw), cur),
                   pl.BlockSpec((None, nres, qb, LANES), cur)),
        compiler_params=_params("parallel", "parallel", "arbitrary"),
        name=f"attn_dil{dil}",
    )(q, k, k, v, v)


def _hgrn_decay(lf_all, b_ref, reach_ref, c):
    r_i = lax.broadcasted_iota(jnp.int32, (c, c), 0)
    c_i = lax.broadcasted_iota(jnp.int32, (c, c), 1)
    tri = jnp.where(c_i <= r_i, 1.0, 0.0).astype(BF16)
    for k in range(lf_all.shape[0] // c):
        lf = lf_all[k * c:(k + 1) * c, :]
        lf_hi = lf.astype(BF16)
        lf_lo = (lf - lf_hi.astype(F32)).astype(BF16)
        b = _dot(tri, lf_hi) + _dot(tri, lf_lo)
        b_ref[k * c:(k + 1) * c, :] = b
        b_mid = b[c // 2 - 1:c // 2, :]
        reach_ref[k] = jnp.max(jnp.maximum(b[0:1, :] - b_mid, b_mid - b[c - 1:c, :]))


def _hgrn_scores_midref(q, kk, b, heads):
    c = b.shape[0]
    b_mid = b[c // 2 - 1:c // 2, :]
    b_last = b[c - 1:c, :]
    q_s = q * jnp.exp2(b - b_mid).astype(BF16)
    k_s = (kk * jnp.exp2(b_mid - b)).astype(BF16)
    q_in = q_s * jnp.exp2(b_mid).astype(BF16)
    k_out = k_s * jnp.exp2(b_last - b_mid).astype(BF16)
    r_i = lax.broadcasted_iota(jnp.int32, (c, c), 0)
    c_i = lax.broadcasted_iota(jnp.int32, (c, c), 1)
    causal = c_i <= r_i
    scores = []
    for hd in range(heads):
        sl = slice(hd * HGRN_HEAD_DIM, (hd + 1) * HGRN_HEAD_DIM)
        scores.append(jnp.where(causal, _dot_nt(q_s[:, sl], k_s[:, sl]), 0.0).astype(BF16))
    return scores, q_in, k_out


def _hgrn_scores_bounded(q, kk, b, heads):
    c, w = b.shape
    q = q.astype(F32)
    q_in = (q * jnp.exp2(b)).astype(BF16)
    k_out = (kk * jnp.exp2(b[c - 1:c, :] - b)).astype(BF16)
    r_i = lax.broadcasted_iota(jnp.int32, (c, c), 0)
    c_i = lax.broadcasted_iota(jnp.int32, (c, c), 1)
    rows = lax.broadcasted_iota(jnp.int32, (c, w), 0)

    levels = []
    half = SUBLANES
    while half < c:
        span = 2 * half
        pieces = [jnp.broadcast_to(b[s + half - 1:s + half, :], (span, w))
                  for s in range(0, c, span)]
        bref = pieces[0] if len(pieces) == 1 else jnp.concatenate(pieces, axis=0)
        upper = (rows & (span - 1)) >= half
        e = jnp.exp2(jnp.where(upper, b - bref, bref - b))
        q_up = jnp.where(upper, q * e, 0.0).astype(BF16)
        k_lo = jnp.where(upper, 0.0, kk * e).astype(BF16)
        same = (r_i & ~(span - 1)) == (c_i & ~(span - 1))
        levels.append((q_up, k_lo, same))
        half = span

    nb = c // SUBLANES
    sub = lax.broadcasted_iota(jnp.int32, (nb, SUBLANES, w), 1)
    q3 = q.reshape(nb, SUBLANES, w)
    k3 = kk.reshape(nb, SUBLANES, w)
    b3 = b.reshape(nb, SUBLANES, w)
    near = []
    for dist in range(SUBLANES):
        if dist == 0:
            prod = q3 * k3
        else:
            k_r = pltpu.roll(k3, dist, 1)
            b_r = pltpu.roll(b3, dist, 1)
            prod = jnp.where(sub >= dist, q3 * k_r * jnp.exp2(b3 - b_r), 0.0)
        near.append(prod.reshape(c, w))

    scores = []
    for hd in range(heads):
        sl = slice(hd * HGRN_HEAD_DIM, (hd + 1) * HGRN_HEAD_DIM)
        a = jnp.zeros((c, c), F32)
        for q_up, k_lo, same in levels:
            a = a + jnp.where(same, _dot_nt(q_up[:, sl], k_lo[:, sl]), 0.0)
        for dist in range(SUBLANES):
            col_val = jnp.sum(near[dist][:, sl], axis=-1, keepdims=True)
            a = a + jnp.where(c_i == r_i - dist, col_val, 0.0)
        scores.append(a.astype(BF16))
    return scores, q_in, k_out


def _hgrn_chunk(q_ref, i_ref, og_ref, hn_ref, o_ref, st_ref, b, kk, rows, score_fn, heads):
    c = b.shape[0]
    v_bf = i_ref[rows, :]
    scores, q_in, k_out = score_fn(q_ref[rows, :], kk, b, heads)
    st_decay = jnp.exp2(b[c - 1:c, :])
    gain = hn_ref[...]
    for hd in range(heads):
        sl = slice(hd * HGRN_HEAD_DIM, (hd + 1) * HGRN_HEAD_DIM)
        v_h = v_bf[:, sl]
        st_h = st_ref[hd]
        o_h = _dot(scores[hd], v_h) + _dot_nt(q_in[:, sl], st_h.astype(BF16))
        st_ref[hd] = st_h * st_decay[:, sl] + _dot_tn(v_h, k_out[:, sl])
        o_n = _rms(o_h, gain[:, sl])
        o_ref[rows, sl] = (o_n * og_ref[rows, sl].astype(F32)).astype(BF16)


def _load_by_token(src_ref, scr_ref):
    dil, rows, width = src_ref.shape
    if dil == 1:
        return src_ref[0].astype(F32)
    parts = []
    for c in range(width // LANES):
        cols = slice(c * LANES, (c + 1) * LANES)
        for r in range(dil):
            scr_ref[c, pl.ds(r, rows, stride=dil), :] = src_ref[r, :, cols].astype(F32)
        parts.append(scr_ref[c])
    return parts[0] if len(parts) == 1 else jnp.concatenate(parts, axis=-1)


def _mix_kernel(x_ref, g_ref, o0_ref, o1_ref, o2_ref, s0_ref, s1_ref, s2_ref, ob_ref,
                wg_ref, wa_ref, wb_ref, wo_ref, out_ref, *scr):
    x = x_ref[...]
    d = x.shape[1]
    h = _rms(x, g_ref[...]).astype(BF16)
    gate_a = jax.nn.sigmoid(_dot(h, wg_ref[:, :d]))
    gate_b = jax.nn.sigmoid(_dot(h, wg_ref[:, d:]))

    scr = list(scr)
    take = lambda ref: _load_by_token(ref, scr.pop(0) if ref.shape[0] > 1 else None)
    o0, o1, o2 = take(o0_ref), take(o1_ref), take(o2_ref)
    s0, s1, s2 = take(s0_ref), take(s1_ref), take(s2_ref)
    head_lane = lax.broadcasted_iota(jnp.int32, s0.shape, 1) < ATT_HEADS
    lse = [s + jnp.log2(jnp.where(head_lane, pltpu.roll(s, LANES - ATT_HEADS, 1), 1.0))
           for s in (s0, s1, s2)]
    mx = jnp.maximum(jnp.maximum(lse[0], lse[1]), lse[2])
    inv = 1.0 / (jnp.exp2(lse[0] - mx) + jnp.exp2(lse[1] - mx) + jnp.exp2(lse[2] - mx))
    w0, w1, w2 = (jnp.exp2(s - mx) * inv for s in (s0, s1, s2))
    parts = []
    for hd in range(ATT_HEADS):
        sl = slice(hd * ATT_HEAD_DIM, (hd + 1) * ATT_HEAD_DIM)
        parts.append(o0[:, sl] * w0[:, hd:hd + 1] + o1[:, sl] * w1[:, hd:hd + 1]
                     + o2[:, sl] * w2[:, hd:hd + 1])
    att = jnp.concatenate(parts, axis=-1).astype(BF16)

    y_a = _dot(att, wa_ref[...])
    y_b = _dot(ob_ref[...], wb_ref[...])
    merged = (gate_a * y_a + gate_b * y_b).astype(BF16)
    out_ref[...] = x + _dot(merged, wo_ref[...])


def _mix(x, norm, o_groups, s_groups, ob, w_in, gate_off, wa, wb, wo, layer, casts=()):
    n, d = x.shape
    batch = o_groups[0].shape[0]
    seq = n // batch
    tm = min(TOKEN_TILE, seq)
    per_b = seq // tm
    row = lambda b, i: (b * per_b + i, 0)
    rows = lambda a: pl.BlockSpec((tm, a.shape[1]), row)
    by_class = lambda a: pl.BlockSpec((None, a.shape[1], tm // a.shape[1], a.shape[3]),
                                      lambda b, i: (b, 0, i, 0))
    grouped = [*o_groups, *s_groups]
    args = [x, norm, *grouped, ob, w_in, wa, wb, wo]
    in_specs = ([rows(x), _layer_block(norm, layer)] + [by_class(a) for a in grouped]
                + [rows(ob), _col_block(w_in, gate_off, 2 * d),
                   _resident(wa), _resident(wb), _resident(wo)])
    scratch = [pltpu.VMEM((a.shape[3] // LANES, tm, LANES), F32)
               for a in grouped if a.shape[1] > 1]
    return _pallas(
        _mix_kernel,
        name="mix_out",
        grid=(batch, per_b), sem=("parallel", "parallel"),
        in_specs=in_specs, args=args,
        out_shape=jax.ShapeDtypeStruct((n, d), F32), out_specs=pl.BlockSpec((tm, d), row),
        scratch=scratch,
        casts=casts, step_of=lambda b, i: b * per_b + i,
    )


def _mem_kv_kernel(m_ref, g_ref, wk_ref, wv_ref, k_ref, v_ref):
    h = _rms(m_ref[...], g_ref[...]).astype(BF16)
    k_ref[...] = _dot(h, wk_ref[...]).astype(BF16)
    v_ref[...] = _dot(h, wv_ref[...]).astype(BF16)


def _mem_kv(mem, norm, wkv, layer):
    n, d = mem.shape
    tm = min(TOKEN_TILE, n)
    width = wkv.shape[1] // 2
    row = lambda i: (i, 0)
    out = jax.ShapeDtypeStruct((n, width), BF16)
    return pl.pallas_call(
        _mem_kv_kernel,
        out_shape=(out, out),
        grid=(n // tm,),
        in_specs=[pl.BlockSpec((tm, d), row), _layer_block(norm, layer),
                  _col_block(wkv, 0, width), _col_block(wkv, width, width)],
        out_specs=(pl.BlockSpec((tm, width), row),) * 2,
        compiler_params=_params("parallel"),
        name="mem_kv",
    )(mem, norm, wkv, wkv)


def _xattn_kernel(x_ref, g_ref, k_ref, v_ref, wq_ref, wo_ref, out_ref):
    x = x_ref[...]
    h = _rms(x, g_ref[...]).astype(BF16)
    dh = wq_ref.shape[1] // X_HEADS
    q = (_dot(h, wq_ref[...]) * (dh ** -0.5)).astype(BF16)
    outs = []
    for hd in range(X_HEADS):
        sl = slice(hd * dh, (hd + 1) * dh)
        s = _dot_nt(q[:, sl], k_ref[:, sl])
        m = jnp.max(s, axis=-1, keepdims=True)
        p = jnp.exp(s - m)
        den = jnp.sum(p, axis=-1, keepdims=True)
        outs.append(_dot(p.astype(BF16), v_ref[:, sl]) / den)
    o = jnp.concatenate(outs, axis=-1).astype(BF16)
    out_ref[...] = x + _dot(o, wo_ref[...])


def _xattn(x, norm, k, v, wq, wo, layer, batch, casts=()):
    n, d = x.shape
    seq = n // batch
    tm = min(TOKEN_TILE, seq)
    n_mem = k.shape[0] // batch
    per_b = seq // tm
    row = lambda b, i: (b * per_b + i, 0)
    kv = pl.BlockSpec((n_mem, k.shape[1]), lambda b, i: (b, 0))
    return _pallas(
        _xattn_kernel,
        name="xattn",
        grid=(batch, per_b), sem=("parallel", "parallel"),
        in_specs=[pl.BlockSpec((tm, d), row), _layer_block(norm, layer), kv, kv,
                  _resident(wq), _resident(wo)],
        args=[x, norm, k, v, wq, wo],
        out_shape=jax.ShapeDtypeStruct((n, d), F32), out_specs=pl.BlockSpec((tm, d), row),
        casts=casts, step_of=lambda b, i: b * per_b + i,
    )


def kernel(x, mem, positions, ffn1_norm, ffn1_w_gu, ffn1_w_down, mix_norm, w_in,
           hgrn_lower_bounds, hgrn_head_norm, w_att_branch, w_hgrn_branch, w_mix_out,
           xattn_norm, mem_norm, xattn_wq, xattn_wkv, xattn_wo,
           ffn2_norm, ffn2_w_gu, ffn2_w_down, final_norm):
    batch, seq, d = x.shape
    depth = w_in.shape[0]
    n = batch * seq
    att_w = len(ATT_GROUPS) * ATT_GROUP_WIDTH
    hg_w = hgrn_head_norm.shape[1]
    hgrn_off = 3 * att_w
    gate_off = hgrn_off + 4 * hg_w

    xf = x.reshape(n, d)
    memf = mem.reshape(batch * mem.shape[1], d)
    cos2, sin2 = _rope_tables(positions.reshape(n, 1).astype(F32))

    gain = lambda a: a.reshape(a.shape[0], 1, a.shape[1])
    ffn1_norm, mix_norm, hgrn_head_norm, xattn_norm, mem_norm, ffn2_norm = map(
        gain, (ffn1_norm, mix_norm, hgrn_head_norm, xattn_norm, mem_norm, ffn2_norm))

    ffn1_w = (ffn1_w_gu[0].astype(BF16), ffn1_w_down[0].astype(BF16))

    for l in range(depth):
        xf, (w_in_l,) = _ffn(xf, ffn1_norm, *ffn1_w, l, casts=[(w_in, l)])

        groups, (w_att_l, w_hgrn_l, w_mix_l) = _proj_attn(
            xf, mix_norm, cos2, sin2, w_in_l, l, batch,
            casts=[(w_att_branch, l), (w_hgrn_branch, l), (w_mix_out, l)])
        o_groups, s_groups = [], []
        for kind, *arrs in groups:
            o_g, s_g = arrs if kind == "attn" else _attn_group(*arrs)
            o_groups.append(o_g)
            s_groups.append(s_g)
        o_b, (wkv_l, wq_l, wo_l) = _hgrn_mixer(
            xf, mix_norm, hgrn_lower_bounds, w_in_l, hgrn_off, hgrn_head_norm, l, batch,
            casts=[(xattn_wkv, l), (xattn_wq, l), (xattn_wo, l)])
        xf, _ = _mix(xf, mix_norm, o_groups, s_groups, o_b, w_in_l, gate_off,
                     w_att_l, w_hgrn_l, w_mix_l, l)

        k_x, v_x = _mem_kv(memf, mem_norm, wkv_l, l)
        xf, (ffn2_gu_l, ffn2_down_l) = _xattn(xf, xattn_norm, k_x, v_x, wq_l, wo_l, l, batch,
                                              casts=[(ffn2_w_gu, l), (ffn2_w_down, l)])

        last = l == depth - 1
        xf, ffn1_w = _ffn(xf, ffn2_norm, ffn2_gu_l, ffn2_down_l, l,
                          final_g=final_norm.reshape(1, -1) if last else None,
                          casts=[] if last else [(ffn1_w_gu, l + 1), (ffn1_w_down, l + 1)])

    return xf.reshape(batch, seq, d)
```

```python
import functools
import math

import jax
import jax.numpy as jnp
from jax import lax
from jax.experimental import pallas as pl
from jax.experimental.pallas import tpu as pltpu

F32 = jnp.float32
BF16 = jnp.bfloat16

ATT_GROUPS = ((128, 1), (512, 4), (2048, 16))
ATT_HEADS = 4
ATT_HEAD_DIM = 128
ATT_GROUP_WIDTH = ATT_HEADS * ATT_HEAD_DIM
HGRN_HEAD_DIM = 128
X_HEADS = 4
ROPE_THETA = 10000.0
EPS = 1e-6

LANES = 128
SUBLANES = 8
MXU_TILE = 256
VMEM_LIMIT_BYTES = 56 * 1024 * 1024

TOKEN_TILE = 1024
FFN_CHUNK_TILES = 3
MXU_PERMUTE_MIN_STRIDE = 8
ATT_BLOCK = 128
ATT_CHAINS = 8
HGRN_CHUNK = 128
HGRN_SAFE_EXPONENT = 115.0
NEG_BIG = -1e30


def _params(*sem):
    return pltpu.CompilerParams(dimension_semantics=sem,
                                vmem_limit_bytes=VMEM_LIMIT_BYTES)


def _rms(x, g):
    ms = jnp.mean(x * x, axis=-1, keepdims=True)
    return x * lax.rsqrt(ms + EPS) * g


def _dot(a, b):
    return jnp.dot(a, b, preferred_element_type=F32)


def _dot_nt(a, b):
    return lax.dot_general(a, b, (((1,), (1,)), ((), ())), preferred_element_type=F32)


def _dot_tn(a, b):
    return lax.dot_general(a, b, (((0,), (0,)), ((), ())), preferred_element_type=F32)


def _resident(a):
    return pl.BlockSpec(a.shape, lambda *_: (0,) * a.ndim, pipeline_mode=pl.Buffered(1))


def _layer_block(a, layer):
    _, rows, width = a.shape
    return pl.BlockSpec((None, rows, width), lambda *_: (layer, 0, 0),
                        pipeline_mode=pl.Buffered(1))


def _col_block(a, off, size):
    return pl.BlockSpec((pl.Element(a.shape[0]), pl.Element(size)), lambda *_: (0, off),
                        pipeline_mode=pl.Buffered(1))


def _pallas(body, *, name, grid, sem, in_specs, args, out_shape, out_specs, scratch=(),
            casts=(), step_of=None):
    single = not isinstance(out_shape, (tuple, list))
    out_shape = [out_shape] if single else list(out_shape)
    out_specs = [out_specs] if single else list(out_specs)
    n_in, n_out, n_cast = len(in_specs), len(out_shape), len(casts)
    steps = math.prod(grid)
    cast_in, cast_out, cast_shape = [], [], []
    for a, layer in casts:
        _, rows, cols = a.shape
        chunk = rows // steps
        assert chunk * steps == rows and chunk % (2 * SUBLANES) == 0
        cast_in.append(pl.BlockSpec((None, chunk, cols),
                                    lambda *g, layer=layer: (layer, step_of(*g), 0)))
        cast_out.append(pl.BlockSpec((chunk, cols), lambda *g: (step_of(*g), 0)))
        cast_shape.append(jax.ShapeDtypeStruct((rows, cols), BF16))

    def hosted(*refs):
        ins = refs[:n_in]
        srcs = refs[n_in:n_in + n_cast]
        outs = refs[n_in + n_cast:n_in + n_cast + n_out]
        dsts = refs[n_in + n_cast + n_out:n_in + 2 * n_cast + n_out]
        for src, dst in zip(srcs, dsts):
            dst[...] = src[...].astype(BF16)
        body(*ins, *outs, *refs[n_in + 2 * n_cast + n_out:])

    res = pl.pallas_call(
        hosted,
        out_shape=tuple(out_shape + cast_shape),
        grid=grid,
        in_specs=list(in_specs) + cast_in,
        out_specs=tuple(out_specs + cast_out),
        scratch_shapes=list(scratch),
        compiler_params=_params(*sem),
        name=name,
    )(*args, *[a for a, _ in casts])
    main = res[:n_out]
    return (main[0] if single else tuple(main)), list(res[n_out:])


def _rope_table_kernel(pos_a_ref, pos_b_ref, cos_ref, sin_ref):
    half = ATT_HEAD_DIM // 2
    rows = pos_a_ref.shape[0]
    lane = lax.broadcasted_iota(jnp.int32, (1, ATT_HEAD_DIM), 1)
    upper = lane >= half
    idx = jnp.where(upper, lane - half, lane).astype(F32)
    inv_freq = jnp.exp(idx * (-2.0 * math.log(ROPE_THETA) / ATT_HEAD_DIM))
    ang = jnp.where(upper, pos_b_ref[...], pos_a_ref[...]) * inv_freq
    sign = jnp.where(upper, 1.0, -1.0)
    for val, out_ref, mult in ((jnp.cos(ang), cos_ref, 1.0), (jnp.sin(ang), sin_ref, sign)):
        swapped = pltpu.roll(val, half, 1)
        out_ref[0:rows, :] = jnp.where(upper, swapped, val) * mult
        out_ref[rows:2 * rows, :] = jnp.where(upper, val, swapped) * mult


def _rope_tables(pos_col, casts=()):
    n = pos_col.shape[0]
    tm = min(n, 2048)
    out = jax.ShapeDtypeStruct((n, ATT_HEAD_DIM), F32)
    return _pallas(
        _rope_table_kernel,
        name="rope_tables",
        grid=(n // tm,), sem=("parallel",),
        in_specs=[pl.BlockSpec((tm // 2, 1), lambda i: (2 * i, 0)),
                  pl.BlockSpec((tm // 2, 1), lambda i: (2 * i + 1, 0))],
        args=[pos_col, pos_col],
        out_shape=(out, out),
        out_specs=(pl.BlockSpec((tm, ATT_HEAD_DIM), lambda i: (i, 0)),) * 2,
        casts=casts, step_of=lambda i: i,
    )


def _ffn_chunks(d_ff):
    assert d_ff % MXU_TILE == 0
    tiles = d_ff // MXU_TILE
    n_chunks = -(-tiles // FFN_CHUNK_TILES)
    bounds = [MXU_TILE * (tiles * k // n_chunks) for k in range(n_chunks + 1)]
    return list(zip(bounds[:-1], bounds[1:]))


def _ffn_kernel(*refs, final):
    if final:
        x_ref, g_ref, wg_ref, wu_ref, wd_ref, fg_ref, o_ref = refs
    else:
        x_ref, g_ref, wg_ref, wu_ref, wd_ref, o_ref = refs
    x = x_ref[...]
    h = _rms(x, g_ref[...]).astype(BF16)
    acc = None
    for lo, hi in _ffn_chunks(wg_ref.shape[1]):
        gate = _dot(h, wg_ref[:, lo:hi])
        up = _dot(h, wu_ref[:, lo:hi])
        act = (gate * jax.nn.sigmoid(gate) * up).astype(BF16)
        part = _dot(act, wd_ref[lo:hi, :])
        acc = part if acc is None else acc + part
    y = x + 0.5 * acc
    if final:
        y = _rms(y, fg_ref[...])
    o_ref[...] = y


def _ffn(x, norm, w_gu, w_down, layer, final_g=None, casts=()):
    n, d = x.shape
    d_ff = w_down.shape[0]
    tm = min(TOKEN_TILE, n)
    final = final_g is not None
    row = pl.BlockSpec((tm, d), lambda i: (i, 0))
    in_specs = [row, _layer_block(norm, layer), _col_block(w_gu, 0, d_ff),
                _col_block(w_gu, d_ff, d_ff), _resident(w_down)]
    args = [x, norm, w_gu, w_gu, w_down]
    if final:
        in_specs.append(_resident(final_g))
        args.append(final_g)
    return _pallas(
        functools.partial(_ffn_kernel, final=final),
        name="ffn_final" if final else "ffn",
        grid=(n // tm,), sem=("parallel",),
        in_specs=in_specs, args=args,
        out_shape=jax.ShapeDtypeStruct((n, d), F32), out_specs=row,
        casts=casts, step_of=lambda i: i,
    )


def _class_permutation(dil):
    per = MXU_TILE // dil
    i = lax.broadcasted_iota(jnp.int32, (MXU_TILE, MXU_TILE), 0)
    j = lax.broadcasted_iota(jnp.int32, (MXU_TILE, MXU_TILE), 1)
    return jnp.where(j == (i % per) * dil + i // per, 1.0, 0.0).astype(BF16)


def _store_by_class(out_ref, scr_ref, val, dil):
    if dil == 1:
        out_ref[0] = val.astype(out_ref.dtype)
        return
    if dil < MXU_PERMUTE_MIN_STRIDE:
        rows = val.shape[0] // dil
        for c in range(val.shape[1] // LANES):
            cols = slice(c * LANES, (c + 1) * LANES)
            scr_ref[c] = val[:, cols]
            for r in range(dil):
                out_ref[r, :, cols] = (
                    scr_ref[c, pl.ds(r, rows, stride=dil), :].astype(out_ref.dtype))
        return
    val = val.astype(out_ref.dtype)
    perm = _class_permutation(dil)
    per = MXU_TILE // dil
    for s in range(val.shape[0] // MXU_TILE):
        y = _dot(perm, val[s * MXU_TILE:(s + 1) * MXU_TILE, :]).astype(out_ref.dtype)
        for r in range(dil):
            out_ref[r, s * per:(s + 1) * per, :] = y[r * per:(r + 1) * per, :]


def _proj_attn_kernel(x_ref, g_ref, cos_ref, sin_ref, w_ref, *rest, dils, fused, strided):
    n_g = len(dils)
    gw = ATT_GROUP_WIDTH
    n_out = sum(2 if f else 3 for f in fused)
    outs, scr = list(rest[:n_out]), list(rest[n_out:])
    stage = [scr.pop(0) for _ in range(3)] if strided else [None] * 3
    first = pl.program_id(1) == 0
    h = _rms(x_ref[...], g_ref[...]).astype(BF16)
    cos = cos_ref[...]
    sin = sin_ref[...]
    scale = ATT_HEAD_DIM ** -0.5 * math.log2(math.e)
    for gi, dil in enumerate(dils):
        part = lambda p: w_ref[:, (p * n_g + gi) * gw:(p * n_g + gi + 1) * gw]
        zq = _dot(h, part(0))
        zk = _dot(h, part(1))
        rq, rk = [], []
        for hd in range(ATT_HEADS):
            sl = slice(hd * ATT_HEAD_DIM, (hd + 1) * ATT_HEAD_DIM)
            xq = zq[:, sl]
            xk = zk[:, sl]
            rq.append((xq * cos + pltpu.roll(xq, ATT_HEAD_DIM // 2, 1) * sin) * scale)
            rk.append(xk * cos + pltpu.roll(xk, ATT_HEAD_DIM // 2, 1) * sin)
        if fused[gi]:
            o_ref, st_ref = outs.pop(0), outs.pop(0)
            q_ref, k_ref, v_ref, kp_ref, vp_ref = (scr.pop(0) for _ in range(5))
        else:
            q_ref, k_ref, v_ref = outs.pop(0), outs.pop(0), outs.pop(0)
        _store_by_class(q_ref, stage[0], jnp.concatenate(rq, axis=-1), dil)
        _store_by_class(k_ref, stage[1], jnp.concatenate(rk, axis=-1), dil)
        _store_by_class(v_ref, stage[2], _dot(h, part(2)), dil)
        if fused[gi]:
            @pl.when(first)
            def _():
                kp_ref[...] = jnp.zeros_like(kp_ref)
                vp_ref[...] = jnp.zeros_like(vp_ref)

            rows = q_ref.shape[1]
            _attn_window(q_ref, kp_ref, k_ref, vp_ref, v_ref, o_ref, st_ref,
                         first, dil, rows // ATT_BLOCK)
            kp_ref[...] = k_ref[:, rows - ATT_BLOCK:, :]
            vp_ref[...] = v_ref[:, rows - ATT_BLOCK:, :]


def _proj_attn(x, norm, cos2, sin2, w_in, layer, batch, casts=()):
    n, d = x.shape
    seq = n // batch
    tm = min(TOKEN_TILE, seq)
    per_b = seq // tm
    gw = ATT_GROUP_WIDTH
    dils = tuple(dil for _, dil in ATT_GROUPS)
    fused = tuple((tm // dil) % ATT_BLOCK == 0 for dil in dils)
    strided = any(1 < dil < MXU_PERMUTE_MIN_STRIDE for dil in dils)
    row = lambda b, i: (b * per_b + i, 0)
    by_class = lambda dil, width: pl.BlockSpec((None, dil, tm // dil, width),
                                               lambda b, i: (b, 0, i, 0))
    out_shape, out_specs = [], []
    scratch = [pltpu.VMEM((gw // LANES, tm, LANES), F32)] * 3 if strided else []
    for dil, f in zip(dils, fused):
        act = jax.ShapeDtypeStruct((batch, dil, seq // dil, gw), BF16)
        if f:
            out_shape += [act, jax.ShapeDtypeStruct((batch, dil, seq // dil, LANES), F32)]
            out_specs += [by_class(dil, gw), by_class(dil, LANES)]
            scratch += [pltpu.VMEM((dil, tm // dil, gw), BF16)] * 3
            scratch += [pltpu.VMEM((dil, ATT_BLOCK, gw), BF16)] * 2
        else:
            out_shape += [act] * 3
            out_specs += [by_class(dil, gw)] * 3
    outs, cast_res = _pallas(
        functools.partial(_proj_attn_kernel, dils=dils, fused=fused, strided=strided),
        name="proj_attn",
        grid=(batch, per_b), sem=("parallel", "arbitrary"),
        in_specs=[
            pl.BlockSpec((tm, d), row),
            _layer_block(norm, layer),
            pl.BlockSpec((tm, ATT_HEAD_DIM), row),
            pl.BlockSpec((tm, ATT_HEAD_DIM), row),
            _col_block(w_in, 0, 3 * len(dils) * gw),
        ],
        args=[x, norm, cos2, sin2, w_in],
        out_shape=out_shape, out_specs=out_specs, scratch=scratch,
        casts=casts, step_of=lambda b, i: b * per_b + i,
    )
    outs, groups = list(outs), []
    for f in fused:
        groups.append(("attn", outs.pop(0), outs.pop(0)) if f
                      else ("qkv", outs.pop(0), outs.pop(0), outs.pop(0)))
    return groups, cast_res


def _hgrn_mixer_kernel(x_ref, g_ref, lbraw_ref, w_ref, hn_ref, o_ref,
                       q_ref, i_ref, og_ref, kk_ref, b_ref, st_ref, reach_ref,
                       *, layer, tiles_per_seq, chunk, heads):
    h = _rms(x_ref[...], g_ref[...]).astype(BF16)
    raw = lbraw_ref[...]
    p = jnp.exp(raw - jnp.max(raw, axis=0, keepdims=True))
    p = p / jnp.sum(p, axis=0, keepdims=True)
    lb = jnp.sum(p[:layer + 1], axis=0, keepdims=True) - p[0:1]

    hw = q_ref.shape[1]
    sec = lambda k: w_ref[:, k * hw:(k + 1) * hw]
    q_ref[...] = (_dot(h, sec(0)) * (HGRN_HEAD_DIM ** -0.5)).astype(BF16)
    f = lb + (1.0 - lb) * jax.nn.sigmoid(_dot(h, sec(1)))
    kk_ref[...] = 1.0 - f
    i_ref[...] = _dot(h, sec(2)).astype(BF16)
    og = _dot(h, sec(3))
    og_ref[...] = (og * jax.nn.sigmoid(og)).astype(BF16)
    _hgrn_decay(jnp.log2(f), b_ref, reach_ref, chunk)

    @pl.when(pl.program_id(0) % tiles_per_seq == 0)
    def _():
        st_ref[...] = jnp.zeros_like(st_ref)

    def one_chunk(ci, carry):
        rows = pl.ds(pl.multiple_of(ci * chunk, chunk), chunk)
        midref_ok = reach_ref[ci] < HGRN_SAFE_EXPONENT
        for take, score_fn in ((midref_ok, _hgrn_scores_midref),
                               (jnp.logical_not(midref_ok), _hgrn_scores_bounded)):
            @pl.when(take)
            def _():
                _hgrn_chunk(q_ref, i_ref, og_ref, hn_ref, o_ref, st_ref,
                            b_ref[rows, :], kk_ref[rows, :], rows, score_fn, heads)
        return carry

    lax.fori_loop(0, x_ref.shape[0] // chunk, one_chunk, 0)


def _hgrn_mixer(x, norm, lb_raw, w_in, col_off, head_gain, layer, batch, casts=()):
    n, d = x.shape
    seq = n // batch
    tm = min(TOKEN_TILE, seq)
    width = lb_raw.shape[1]
    chunk = min(HGRN_CHUNK, tm)
    heads = width // HGRN_HEAD_DIM
    row = lambda i: (i, 0)
    tile = lambda dtype: pltpu.VMEM((tm, width), dtype)
    return _pallas(
        functools.partial(_hgrn_mixer_kernel, layer=layer, tiles_per_seq=seq // tm,
                          chunk=chunk, heads=heads),
        name="hgrn_mixer",
        grid=(n // tm,), sem=("arbitrary",),
        in_specs=[pl.BlockSpec((tm, d), row), _layer_block(norm, layer), _resident(lb_raw),
                  _col_block(w_in, col_off, 4 * width), _layer_block(head_gain, layer)],
        args=[x, norm, lb_raw, w_in, head_gain],
        out_shape=jax.ShapeDtypeStruct((n, width), BF16),
        out_specs=pl.BlockSpec((tm, width), row),
        scratch=[tile(BF16), tile(BF16), tile(BF16), tile(F32), tile(F32),
                 pltpu.VMEM((heads, HGRN_HEAD_DIM, HGRN_HEAD_DIM), F32),
                 pltpu.SMEM((tm // chunk,), F32)],
        casts=casts, step_of=lambda i: i,
    )


def _attn_kernel(q_ref, kp_ref, kc_ref, vp_ref, vc_ref, o_ref, st_ref, *, nres, nblk):
    _attn_window(q_ref, kp_ref, kc_ref, vp_ref, vc_ref, o_ref, st_ref,
                 pl.program_id(2) == 0, nres, nblk)


def _attn_window(q_ref, kp_ref, kc_ref, vp_ref, vc_ref, o_ref, st_ref, first, nres, nblk):
    row = lax.broadcasted_iota(jnp.int32, (ATT_BLOCK, 2 * ATT_BLOCK), 0)
    col = lax.broadcasted_iota(jnp.int32, (ATT_BLOCK, 2 * ATT_BLOCK), 1)
    band = jnp.logical_and(col >= row, col <= row + ATT_BLOCK)
    low = jnp.where(first, ATT_BLOCK, 0)
    band_first = jnp.logical_and(col >= jnp.maximum(row, low), col <= row + ATT_BLOCK)
    ones = jnp.ones((2 * ATT_BLOCK, ATT_HEAD_DIM), BF16)
    lane = lax.broadcasted_iota(jnp.int32, (ATT_BLOCK, LANES), 1)
    for res in range(nres):
        for blk in range(nblk):
            rows = slice(blk * ATT_BLOCK, (blk + 1) * ATT_BLOCK)
            stats = jnp.ones((ATT_BLOCK, LANES), F32)
            for hd in range(ATT_HEADS):
                sl = slice(hd * ATT_HEAD_DIM, (hd + 1) * ATT_HEAD_DIM)
                if blk == 0:
                    k2 = jnp.concatenate([kp_ref[res, :, sl], kc_ref[res, 0:ATT_BLOCK, sl]], axis=0)
                    v2 = jnp.concatenate([vp_ref[res, :, sl], vc_ref[res, 0:ATT_BLOCK, sl]], axis=0)
                    mask = band_first
                else:
                    win = slice((blk - 1) * ATT_BLOCK, (blk + 1) * ATT_BLOCK)
                    k2 = kc_ref[res, win, sl]
                    v2 = vc_ref[res, win, sl]
                    mask = band
                s = jnp.where(mask, _dot_nt(q_ref[res, rows, sl], k2), NEG_BIG)
                m = jnp.max(s, axis=-1, keepdims=True)
                p = jnp.exp2(s - m).astype(BF16)
                o_den = _dot(p, jnp.concatenate([v2, ones], axis=1))
                den = o_den[:, ATT_HEAD_DIM:]
                o_ref[res, rows, sl] = o_den[:, :ATT_HEAD_DIM].astype(BF16)
                stats = jnp.where(lane == hd, m, jnp.where(lane == ATT_HEADS + hd, den, stats))
            st_ref[res, rows, :] = stats


def _attn_group(q, k, v):
    batch, dil, cls, gw = q.shape
    assert cls % ATT_BLOCK == 0
    nblk = min(ATT_CHAINS, cls // ATT_BLOCK)
    nres = max(1, min(dil, ATT_CHAINS // nblk))
    qb = nblk * ATT_BLOCK
    cur = lambda b, r, i: (b, r, i, 0)
    prev = lambda b, r, i: (b, r, jnp.maximum(i * nblk - 1, 0), 0)
    cur_spec = pl.BlockSpec((None, nres, qb, gw), cur)
    prev_spec = pl.BlockSpec((None, nres, ATT_BLOCK, gw), prev)
    return pl.pallas_call(
        functools.partial(_attn_kernel, nres=nres, nblk=nblk),
        out_shape=(jax.ShapeDtypeStruct((batch, dil, cls, gw), BF16),
                   jax.ShapeDtypeStruct((batch, dil, cls, LANES), F32)),
        grid=(batch, dil // nres, cls // qb),
        in_specs=[cur_spec, prev_spec, cur_spec, prev_spec, cur_spec],
        out_specs=(pl.BlockSpec((None, nres, qb, gw), cur),
                   pl.BlockSpec((None, nres, qb, LANES), cur)),
        compiler_params=_params("parallel", "parallel", "arbitrary"),
        name=f"attn_dil{dil}",
    )(q, k, k, v, v)


def _hgrn_decay(lf_all, b_ref, reach_ref, c):
    r_i = lax.broadcasted_iota(jnp.int32, (c, c), 0)
    c_i = lax.broadcasted_iota(jnp.int32, (c, c), 1)
    tri = jnp.where(c_i <= r_i, 1.0, 0.0).astype(BF16)
    for k in range(lf_all.shape[0] // c):
        lf = lf_all[k * c:(k + 1) * c, :]
        lf_hi = lf.astype(BF16)
        lf_lo = (lf - lf_hi.astype(F32)).astype(BF16)
        b = _dot(tri, lf_hi) + _dot(tri, lf_lo)
        b_ref[k * c:(k + 1) * c, :] = b
        b_mid = b[c // 2 - 1:c // 2, :]
        reach_ref[k] = jnp.max(jnp.maximum(b[0:1, :] - b_mid, b_mid - b[c - 1:c, :]))


def _hgrn_scores_midref(q, kk, b, heads):
    c = b.shape[0]
    b_mid = b[c // 2 - 1:c // 2, :]
    b_last = b[c - 1:c, :]
    q_s = q * jnp.exp2(b - b_mid).astype(BF16)
    k_s = (kk * jnp.exp2(b_mid - b)).astype(BF16)
    q_in = q_s * jnp.exp2(b_mid).astype(BF16)
    k_out = k_s * jnp.exp2(b_last - b_mid).astype(BF16)
    r_i = lax.broadcasted_iota(jnp.int32, (c, c), 0)
    c_i = lax.broadcasted_iota(jnp.int32, (c, c), 1)
    causal = c_i <= r_i
    scores = []
    for hd in range(heads):
        sl = slice(hd * HGRN_HEAD_DIM, (hd + 1) * HGRN_HEAD_DIM)
        scores.append(jnp.where(causal, _dot_nt(q_s[:, sl], k_s[:, sl]), 0.0).astype(BF16))
    return scores, q_in, k_out


def _hgrn_scores_bounded(q, kk, b, heads):
    c, w = b.shape
    q = q.astype(F32)
    q_in = (q * jnp.exp2(b)).astype(BF16)
    k_out = (kk * jnp.exp2(b[c - 1:c, :] - b)).astype(BF16)
    r_i = lax.broadcasted_iota(jnp.int32, (c, c), 0)
    c_i = lax.broadcasted_iota(jnp.int32, (c, c), 1)
    rows = lax.broadcasted_iota(jnp.int32, (c, w), 0)

    levels = []
    half = SUBLANES
    while half < c:
        span = 2 * half
        pieces = [jnp.broadcast_to(b[s + half - 1:s + half, :], (span, w))
                  for s in range(0, c, span)]
        bref = pieces[0] if len(pieces) == 1 else jnp.concatenate(pieces, axis=0)
        upper = (rows & (span - 1)) >= half
        e = jnp.exp2(jnp.where(upper, b - bref, bref - b))
        q_up = jnp.where(upper, q * e, 0.0).astype(BF16)
        k_lo = jnp.where(upper, 0.0, kk * e).astype(BF16)
        same = (r_i & ~(span - 1)) == (c_i & ~(span - 1))
        levels.append((q_up, k_lo, same))
        half = span

    nb = c // SUBLANES
    sub = lax.broadcasted_iota(jnp.int32, (nb, SUBLANES, w), 1)
    q3 = q.reshape(nb, SUBLANES, w)
    k3 = kk.reshape(nb, SUBLANES, w)
    b3 = b.reshape(nb, SUBLANES, w)
    near = []
    for dist in range(SUBLANES):
        if dist == 0:
            prod = q3 * k3
        else:
            k_r = pltpu.roll(k3, dist, 1)
            b_r = pltpu.roll(b3, dist, 1)
            prod = jnp.where(sub >= dist, q3 * k_r * jnp.exp2(b3 - b_r), 0.0)
        near.append(prod.reshape(c, w))

    scores = []
    for hd in range(heads):
        sl = slice(hd * HGRN_HEAD_DIM, (hd + 1) * HGRN_HEAD_DIM)
        a = jnp.zeros((c, c), F32)
        for q_up, k_lo, same in levels:
            a = a + jnp.where(same, _dot_nt(q_up[:, sl], k_lo[:, sl]), 0.0)
        for dist in range(SUBLANES):
            col_val = jnp.sum(near[dist][:, sl], axis=-1, keepdims=True)
            a = a + jnp.where(c_i == r_i - dist, col_val, 0.0)
        scores.append(a.astype(BF16))
    return scores, q_in, k_out


def _hgrn_chunk(q_ref, i_ref, og_ref, hn_ref, o_ref, st_ref, b, kk, rows, score_fn, heads):
    c = b.shape[0]
    v_bf = i_ref[rows, :]
    scores, q_in, k_out = score_fn(q_ref[rows, :], kk, b, heads)
    st_decay = jnp.exp2(b[c - 1:c, :])
    gain = hn_ref[...]
    for hd in range(heads):
        sl = slice(hd * HGRN_HEAD_DIM, (hd + 1) * HGRN_HEAD_DIM)
        v_h = v_bf[:, sl]
        st_h = st_ref[hd]
        o_h = _dot(scores[hd], v_h) + _dot_nt(q_in[:, sl], st_h.astype(BF16))
        st_ref[hd] = st_h * st_decay[:, sl] + _dot_tn(v_h, k_out[:, sl])
        o_n = _rms(o_h, gain[:, sl])
        o_ref[rows, sl] = (o_n * og_ref[rows, sl].astype(F32)).astype(BF16)


def _load_by_token(src_ref, scr_ref):
    dil, rows, width = src_ref.shape
    if dil == 1:
        return src_ref[0].astype(F32)
    parts = []
    for c in range(width // LANES):
        cols = slice(c * LANES, (c + 1) * LANES)
        for r in range(dil):
            scr_ref[c, pl.ds(r, rows, stride=dil), :] = src_ref[r, :, cols].astype(F32)
        parts.append(scr_ref[c])
    return parts[0] if len(parts) == 1 else jnp.concatenate(parts, axis=-1)


def _mix_kernel(x_ref, g_ref, o0_ref, o1_ref, o2_ref, s0_ref, s1_ref, s2_ref, ob_ref,
                wg_ref, wa_ref, wb_ref, wo_ref, out_ref, *scr):
    x = x_ref[...]
    d = x.shape[1]
    h = _rms(x, g_ref[...]).astype(BF16)
    gate_a = jax.nn.sigmoid(_dot(h, wg_ref[:, :d]))
    gate_b = jax.nn.sigmoid(_dot(h, wg_ref[:, d:]))

    scr = list(scr)
    take = lambda ref: _load_by_token(ref, scr.pop(0) if ref.shape[0] > 1 else None)
    o0, o1, o2 = take(o0_ref), take(o1_ref), take(o2_ref)
    s0, s1, s2 = take(s0_ref), take(s1_ref), take(s2_ref)
    head_lane = lax.broadcasted_iota(jnp.int32, s0.shape, 1) < ATT_HEADS
    lse = [s + jnp.log2(jnp.where(head_lane, pltpu.roll(s, LANES - ATT_HEADS, 1), 1.0))
           for s in (s0, s1, s2)]
    mx = jnp.maximum(jnp.maximum(lse[0], lse[1]), lse[2])
    inv = 1.0 / (jnp.exp2(lse[0] - mx) + jnp.exp2(lse[1] - mx) + jnp.exp2(lse[2] - mx))
    w0, w1, w2 = (jnp.exp2(s - mx) * inv for s in (s0, s1, s2))
    parts = []
    for hd in range(ATT_HEADS):
        sl = slice(hd * ATT_HEAD_DIM, (hd + 1) * ATT_HEAD_DIM)
        parts.append(o0[:, sl] * w0[:, hd:hd + 1] + o1[:, sl] * w1[:, hd:hd + 1]
                     + o2[:, sl] * w2[:, hd:hd + 1])
    att = jnp.concatenate(parts, axis=-1).astype(BF16)

    y_a = _dot(att, wa_ref[...])
    y_b = _dot(ob_ref[...], wb_ref[...])
    merged = (gate_a * y_a + gate_b * y_b).astype(BF16)
    out_ref[...] = x + _dot(merged, wo_ref[...])


def _mix(x, norm, o_groups, s_groups, ob, w_in, gate_off, wa, wb, wo, layer, casts=()):
    n, d = x.shape
    batch = o_groups[0].shape[0]
    seq = n // batch
    tm = min(TOKEN_TILE, seq)
    per_b = seq // tm
    row = lambda b, i: (b * per_b + i, 0)
    rows = lambda a: pl.BlockSpec((tm, a.shape[1]), row)
    by_class = lambda a: pl.BlockSpec((None, a.shape[1], tm // a.shape[1], a.shape[3]),
                                      lambda b, i: (b, 0, i, 0))
    grouped = [*o_groups, *s_groups]
    args = [x, norm, *grouped, ob, w_in, wa, wb, wo]
    in_specs = ([rows(x), _layer_block(norm, layer)] + [by_class(a) for a in grouped]
                + [rows(ob), _col_block(w_in, gate_off, 2 * d),
                   _resident(wa), _resident(wb), _resident(wo)])
    scratch = [pltpu.VMEM((a.shape[3] // LANES, tm, LANES), F32)
               for a in grouped if a.shape[1] > 1]
    return _pallas(
        _mix_kernel,
        name="mix_out",
        grid=(batch, per_b), sem=("parallel", "parallel"),
        in_specs=in_specs, args=args,
        out_shape=jax.ShapeDtypeStruct((n, d), F32), out_specs=pl.BlockSpec((tm, d), row),
        scratch=scratch,
        casts=casts, step_of=lambda b, i: b * per_b + i,
    )


def _xattn_kernel(x_ref, g_ref, m_ref, mg_ref, wk_ref, wv_ref, wq_ref, wo_ref, out_ref,
                  k_ref, v_ref):
    @pl.when(pl.program_id(1) == 0)
    def _():
        hm = _rms(m_ref[...], mg_ref[...]).astype(BF16)
        k_ref[...] = _dot(hm, wk_ref[...]).astype(BF16)
        v_ref[...] = _dot(hm, wv_ref[...]).astype(BF16)

    x = x_ref[...]
    h = _rms(x, g_ref[...]).astype(BF16)
    dh = wq_ref.shape[1] // X_HEADS
    q = (_dot(h, wq_ref[...]) * (dh ** -0.5)).astype(BF16)
    outs = []
    for hd in range(X_HEADS):
        sl = slice(hd * dh, (hd + 1) * dh)
        s = _dot_nt(q[:, sl], k_ref[:, sl])
        m = jnp.max(s, axis=-1, keepdims=True)
        p = jnp.exp(s - m)
        den = jnp.sum(p, axis=-1, keepdims=True)
        outs.append(_dot(p.astype(BF16), v_ref[:, sl]) / den)
    o = jnp.concatenate(outs, axis=-1).astype(BF16)
    out_ref[...] = x + _dot(o, wo_ref[...])


def _xattn(x, norm, mem, mem_norm, wkv, wq, wo, layer, batch, casts=()):
    n, d = x.shape
    seq = n // batch
    tm = min(TOKEN_TILE, seq)
    n_mem = mem.shape[0] // batch
    width = wkv.shape[1] // 2
    per_b = seq // tm
    row = lambda b, i: (b * per_b + i, 0)
    kv_scratch = pltpu.VMEM((n_mem, width), BF16)
    return _pallas(
        _xattn_kernel,
        name="xattn",
        grid=(batch, per_b), sem=("parallel", "arbitrary"),
        in_specs=[pl.BlockSpec((tm, d), row), _layer_block(norm, layer),
                  pl.BlockSpec((n_mem, d), lambda b, i: (b, 0)), _layer_block(mem_norm, layer),
                  _col_block(wkv, 0, width), _col_block(wkv, width, width),
                  _resident(wq), _resident(wo)],
        args=[x, norm, mem, mem_norm, wkv, wkv, wq, wo],
        out_shape=jax.ShapeDtypeStruct((n, d), F32), out_specs=pl.BlockSpec((tm, d), row),
        scratch=[kv_scratch, kv_scratch],
        casts=casts, step_of=lambda b, i: b * per_b + i,
    )


def kernel(x, mem, positions, ffn1_norm, ffn1_w_gu, ffn1_w_down, mix_norm, w_in,
           hgrn_lower_bounds, hgrn_head_norm, w_att_branch, w_hgrn_branch, w_mix_out,
           xattn_norm, mem_norm, xattn_wq, xattn_wkv, xattn_wo,
           ffn2_norm, ffn2_w_gu, ffn2_w_down, final_norm):
    batch, seq, d = x.shape
    depth = w_in.shape[0]
    n = batch * seq
    att_w = len(ATT_GROUPS) * ATT_GROUP_WIDTH
    hg_w = hgrn_head_norm.shape[1]
    hgrn_off = 3 * att_w
    gate_off = hgrn_off + 4 * hg_w

    xf = x.reshape(n, d)
    memf = mem.reshape(batch * mem.shape[1], d)
    (cos2, sin2), ffn1_w = _rope_tables(positions.reshape(n, 1).astype(F32),
                                        casts=[(ffn1_w_gu, 0), (ffn1_w_down, 0)])

    gain = lambda a: a.reshape(a.shape[0], 1, a.shape[1])
    ffn1_norm, mix_norm, hgrn_head_norm, xattn_norm, mem_norm, ffn2_norm = map(
        gain, (ffn1_norm, mix_norm, hgrn_head_norm, xattn_norm, mem_norm, ffn2_norm))

    for l in range(depth):
        xf, (w_in_l,) = _ffn(xf, ffn1_norm, *ffn1_w, l, casts=[(w_in, l)])

        groups, (w_att_l, w_hgrn_l, w_mix_l) = _proj_attn(
            xf, mix_norm, cos2, sin2, w_in_l, l, batch,
            casts=[(w_att_branch, l), (w_hgrn_branch, l), (w_mix_out, l)])
        o_groups, s_groups = [], []
        for kind, *arrs in groups:
            o_g, s_g = arrs if kind == "attn" else _attn_group(*arrs)
            o_groups.append(o_g)
            s_groups.append(s_g)
        o_b, (wkv_l, wq_l, wo_l) = _hgrn_mixer(
            xf, mix_norm, hgrn_lower_bounds, w_in_l, hgrn_off, hgrn_head_norm, l, batch,
            casts=[(xattn_wkv, l), (xattn_wq, l), (xattn_wo, l)])
        xf, _ = _mix(xf, mix_norm, o_groups, s_groups, o_b, w_in_l, gate_off,
                     w_att_l, w_hgrn_l, w_mix_l, l)

        xf, (ffn2_gu_l, ffn2_down_l) = _xattn(xf, xattn_norm, memf, mem_norm, wkv_l, wq_l, wo_l,
                                              l, batch,
                                              casts=[(ffn2_w_gu, l), (ffn2_w_down, l)])

        last = l == depth - 1
        xf, ffn1_w = _ffn(xf, ffn2_norm, ffn2_gu_l, ffn2_down_l, l,
                          final_g=final_norm.reshape(1, -1) if last else None,
                          casts=[] if last else [(ffn1_w_gu, l + 1), (ffn1_w_down, l + 1)])

    return xf.reshape(batch, seq, d)
```

```python
import functools
import math

import jax
import jax.numpy as jnp
from jax import lax
from jax.experimental import pallas as pl
from jax.experimental.pallas import tpu as pltpu

F32 = jnp.float32
BF16 = jnp.bfloat16

ATT_GROUPS = ((128, 1), (512, 4), (2048, 16))
ATT_HEADS = 4
ATT_HEAD_DIM = 128
ATT_GROUP_WIDTH = ATT_HEADS * ATT_HEAD_DIM
HGRN_HEAD_DIM = 128
X_HEADS = 4
ROPE_THETA = 10000.0
EPS = 1e-6

LANES = 128
SUBLANES = 8
MXU_TILE = 256
VMEM_LIMIT_BYTES = 56 * 1024 * 1024

TOKEN_TILE = 1024
FFN_CHUNK_TILES = 3
MXU_PERMUTE_MIN_STRIDE = 8
ATT_BLOCK = 128
ATT_CHAINS = 8
HGRN_CHUNK = 128
HGRN_SAFE_EXPONENT = 115.0
NEG_BIG = -1e30


def _params(*sem):
    return pltpu.CompilerParams(dimension_semantics=sem,
                                vmem_limit_bytes=VMEM_LIMIT_BYTES)


def _rms(x, g):
    ms = jnp.mean(x * x, axis=-1, keepdims=True)
    return x * lax.rsqrt(ms + EPS) * g


def _dot(a, b):
    return jnp.dot(a, b, preferred_element_type=F32)


def _dot_nt(a, b):
    return lax.dot_general(a, b, (((1,), (1,)), ((), ())), preferred_element_type=F32)


def _dot_tn(a, b):
    return lax.dot_general(a, b, (((0,), (0,)), ((), ())), preferred_element_type=F32)


def _resident(a):
    return pl.BlockSpec(a.shape, lambda *_: (0,) * a.ndim, pipeline_mode=pl.Buffered(1))


def _layer_block(a, layer):
    _, rows, width = a.shape
    return pl.BlockSpec((None, rows, width), lambda *_: (layer, 0, 0),
                        pipeline_mode=pl.Buffered(1))


def _col_block(a, off, size):
    return pl.BlockSpec((pl.Element(a.shape[0]), pl.Element(size)), lambda *_: (0, off),
                        pipeline_mode=pl.Buffered(1))


def _pallas(body, *, name, grid, sem, in_specs, args, out_shape, out_specs, scratch=(),
            casts=(), step_of=None):
    single = not isinstance(out_shape, (tuple, list))
    out_shape = [out_shape] if single else list(out_shape)
    out_specs = [out_specs] if single else list(out_specs)
    n_in, n_out, n_cast = len(in_specs), len(out_shape), len(casts)
    steps = math.prod(grid)
    cast_in, cast_out, cast_shape = [], [], []
    for a, layer in casts:
        _, rows, cols = a.shape
        chunk = rows // steps
        assert chunk * steps == rows and chunk % (2 * SUBLANES) == 0
        cast_in.append(pl.BlockSpec((None, chunk, cols),
                                    lambda *g, layer=layer: (layer, step_of(*g), 0)))
        cast_out.append(pl.BlockSpec((chunk, cols), lambda *g: (step_of(*g), 0)))
        cast_shape.append(jax.ShapeDtypeStruct((rows, cols), BF16))

    def hosted(*refs):
        ins = refs[:n_in]
        srcs = refs[n_in:n_in + n_cast]
        outs = refs[n_in + n_cast:n_in + n_cast + n_out]
        dsts = refs[n_in + n_cast + n_out:n_in + 2 * n_cast + n_out]
        for src, dst in zip(srcs, dsts):
            dst[...] = src[...].astype(BF16)
        body(*ins, *outs, *refs[n_in + 2 * n_cast + n_out:])

    res = pl.pallas_call(
        hosted,
        out_shape=tuple(out_shape + cast_shape),
        grid=grid,
        in_specs=list(in_specs) + cast_in,
        out_specs=tuple(out_specs + cast_out),
        scratch_shapes=list(scratch),
        compiler_params=_params(*sem),
        name=name,
    )(*args, *[a for a, _ in casts])
    main = res[:n_out]
    return (main[0] if single else tuple(main)), list(res[n_out:])


def _rope_table_kernel(pos_a_ref, pos_b_ref, cos_ref, sin_ref):
    half = ATT_HEAD_DIM // 2
    rows = pos_a_ref.shape[0]
    lane = lax.broadcasted_iota(jnp.int32, (1, ATT_HEAD_DIM), 1)
    upper = lane >= half
    idx = jnp.where(upper, lane - half, lane).astype(F32)
    inv_freq = jnp.exp(idx * (-2.0 * math.log(ROPE_THETA) / ATT_HEAD_DIM))
    ang = jnp.where(upper, pos_b_ref[...], pos_a_ref[...]) * inv_freq
    sign = jnp.where(upper, 1.0, -1.0)
    for val, out_ref, mult in ((jnp.cos(ang), cos_ref, 1.0), (jnp.sin(ang), sin_ref, sign)):
        swapped = pltpu.roll(val, half, 1)
        out_ref[0:rows, :] = jnp.where(upper, swapped, val) * mult
        out_ref[rows:2 * rows, :] = jnp.where(upper, val, swapped) * mult


def _rope_tables(pos_col, casts=()):
    n = pos_col.shape[0]
    tm = min(n, 2048)
    out = jax.ShapeDtypeStruct((n, ATT_HEAD_DIM), F32)
    return _pallas(
        _rope_table_kernel,
        name="rope_tables",
        grid=(n // tm,), sem=("parallel",),
        in_specs=[pl.BlockSpec((tm // 2, 1), lambda i: (2 * i, 0)),
                  pl.BlockSpec((tm // 2, 1), lambda i: (2 * i + 1, 0))],
        args=[pos_col, pos_col],
        out_shape=(out, out),
        out_specs=(pl.BlockSpec((tm, ATT_HEAD_DIM), lambda i: (i, 0)),) * 2,
        casts=casts, step_of=lambda i: i,
    )


def _ffn_chunks(d_ff):
    assert d_ff % MXU_TILE == 0
    tiles = d_ff // MXU_TILE
    n_chunks = -(-tiles // FFN_CHUNK_TILES)
    bounds = [MXU_TILE * (tiles * k // n_chunks) for k in range(n_chunks + 1)]
    return list(zip(bounds[:-1], bounds[1:]))


def _ffn_kernel(*refs, final):
    if final:
        x_ref, g_ref, wg_ref, wu_ref, wd_ref, fg_ref, o_ref = refs
    else:
        x_ref, g_ref, wg_ref, wu_ref, wd_ref, o_ref = refs
    x = x_ref[...]
    h = _rms(x, g_ref[...]).astype(BF16)
    acc = None
    for lo, hi in _ffn_chunks(wg_ref.shape[1]):
        gate = _dot(h, wg_ref[:, lo:hi])
        up = _dot(h, wu_ref[:, lo:hi])
        act = (gate * jax.nn.sigmoid(gate) * up).astype(BF16)
        part = _dot(act, wd_ref[lo:hi, :])
        acc = part if acc is None else acc + part
    y = x + 0.5 * acc
    if final:
        y = _rms(y, fg_ref[...])
    o_ref[...] = y


def _ffn(x, norm, w_gu, w_down, layer, final_g=None, casts=()):
    n, d = x.shape
    d_ff = w_down.shape[0]
    tm = min(TOKEN_TILE, n)
    final = final_g is not None
    row = pl.BlockSpec((tm, d), lambda i: (i, 0))
    in_specs = [row, _layer_block(norm, layer), _col_block(w_gu, 0, d_ff),
                _col_block(w_gu, d_ff, d_ff), _resident(w_down)]
    args = [x, norm, w_gu, w_gu, w_down]
    if final:
        in_specs.append(_resident(final_g))
        args.append(final_g)
    return _pallas(
        functools.partial(_ffn_kernel, final=final),
        name="ffn_final" if final else "ffn",
        grid=(n // tm,), sem=("parallel",),
        in_specs=in_specs, args=args,
        out_shape=jax.ShapeDtypeStruct((n, d), F32), out_specs=row,
        casts=casts, step_of=lambda i: i,
    )


def _class_permutation(dil):
    per = MXU_TILE // dil
    i = lax.broadcasted_iota(jnp.int32, (MXU_TILE, MXU_TILE), 0)
    j = lax.broadcasted_iota(jnp.int32, (MXU_TILE, MXU_TILE), 1)
    return jnp.where(j == (i % per) * dil + i // per, 1.0, 0.0).astype(BF16)


def _store_by_class(out_ref, scr_ref, val, dil):
    if dil == 1:
        out_ref[0] = val.astype(out_ref.dtype)
        return
    if dil < MXU_PERMUTE_MIN_STRIDE:
        rows = val.shape[0] // dil
        for c in range(val.shape[1] // LANES):
            cols = slice(c * LANES, (c + 1) * LANES)
            scr_ref[c] = val[:, cols]
            for r in range(dil):
                out_ref[r, :, cols] = (
                    scr_ref[c, pl.ds(r, rows, stride=dil), :].astype(out_ref.dtype))
        return
    val = val.astype(out_ref.dtype)
    perm = _class_permutation(dil)
    per = MXU_TILE // dil
    for s in range(val.shape[0] // MXU_TILE):
        y = _dot(perm, val[s * MXU_TILE:(s + 1) * MXU_TILE, :]).astype(out_ref.dtype)
        for r in range(dil):
            out_ref[r, s * per:(s + 1) * per, :] = y[r * per:(r + 1) * per, :]


def _proj_attn_kernel(x_ref, g_ref, cos_ref, sin_ref, w_ref, *rest, dils, fused, strided):
    n_g = len(dils)
    gw = ATT_GROUP_WIDTH
    n_out = sum(2 if f else 3 for f in fused)
    outs, scr = list(rest[:n_out]), list(rest[n_out:])
    stage = [scr.pop(0) for _ in range(3)] if strided else [None] * 3
    first = pl.program_id(1) == 0
    h = _rms(x_ref[...], g_ref[...]).astype(BF16)
    cos = cos_ref[...]
    sin = sin_ref[...]
    scale = ATT_HEAD_DIM ** -0.5 * math.log2(math.e)
    for gi, dil in enumerate(dils):
        part = lambda p: w_ref[:, (p * n_g + gi) * gw:(p * n_g + gi + 1) * gw]
        zq = _dot(h, part(0))
        zk = _dot(h, part(1))
        rq, rk = [], []
        for hd in range(ATT_HEADS):
            sl = slice(hd * ATT_HEAD_DIM, (hd + 1) * ATT_HEAD_DIM)
            xq = zq[:, sl]
            xk = zk[:, sl]
            rq.append((xq * cos + pltpu.roll(xq, ATT_HEAD_DIM // 2, 1) * sin) * scale)
            rk.append(xk * cos + pltpu.roll(xk, ATT_HEAD_DIM // 2, 1) * sin)
        if fused[gi]:
            o_ref, st_ref = outs.pop(0), outs.pop(0)
            q_ref, k_ref, v_ref, kp_ref, vp_ref = (scr.pop(0) for _ in range(5))
        else:
            q_ref, k_ref, v_ref = outs.pop(0), outs.pop(0), outs.pop(0)
        _store_by_class(q_ref, stage[0], jnp.concatenate(rq, axis=-1), dil)
        _store_by_class(k_ref, stage[1], jnp.concatenate(rk, axis=-1), dil)
        _store_by_class(v_ref, stage[2], _dot(h, part(2)), dil)
        if fused[gi]:
            @pl.when(first)
            def _():
                kp_ref[...] = jnp.zeros_like(kp_ref)
                vp_ref[...] = jnp.zeros_like(vp_ref)

            rows = q_ref.shape[1]
            _attn_window(q_ref, kp_ref, k_ref, vp_ref, v_ref, o_ref, st_ref,
                         first, dil, rows // ATT_BLOCK)
            kp_ref[...] = k_ref[:, rows - ATT_BLOCK:, :]
            vp_ref[...] = v_ref[:, rows - ATT_BLOCK:, :]


def _proj_attn(x, norm, cos2, sin2, w_in, layer, batch, casts=()):
    n, d = x.shape
    seq = n // batch
    tm = min(TOKEN_TILE, seq)
    per_b = seq // tm
    gw = ATT_GROUP_WIDTH
    dils = tuple(dil for _, dil in ATT_GROUPS)
    fused = tuple((tm // dil) % ATT_BLOCK == 0 for dil in dils)
    strided = any(1 < dil < MXU_PERMUTE_MIN_STRIDE for dil in dils)
    row = lambda b, i: (b * per_b + i, 0)
    by_class = lambda dil, width: pl.BlockSpec((None, dil, tm // dil, width),
                                               lambda b, i: (b, 0, i, 0))
    out_shape, out_specs = [], []
    scratch = [pltpu.VMEM((gw // LANES, tm, LANES), F32)] * 3 if strided else []
    for dil, f in zip(dils, fused):
        act = jax.ShapeDtypeStruct((batch, dil, seq // dil, gw), BF16)
        if f:
            out_shape += [act, jax.ShapeDtypeStruct((batch, dil, seq // dil, LANES), F32)]
            out_specs += [by_class(dil, gw), by_class(dil, LANES)]
            scratch += [pltpu.VMEM((dil, tm // dil, gw), BF16)] * 3
            scratch += [pltpu.VMEM((dil, ATT_BLOCK, gw), BF16)] * 2
        else:
            out_shape += [act] * 3
            out_specs += [by_class(dil, gw)] * 3
    outs, cast_res = _pallas(
        functools.partial(_proj_attn_kernel, dils=dils, fused=fused, strided=strided),
        name="proj_attn",
        grid=(batch, per_b), sem=("parallel", "arbitrary"),
        in_specs=[
            pl.BlockSpec((tm, d), row),
            _layer_block(norm, layer),
            pl.BlockSpec((tm, ATT_HEAD_DIM), row),
            pl.BlockSpec((tm, ATT_HEAD_DIM), row),
            _col_block(w_in, 0, 3 * len(dils) * gw),
        ],
        args=[x, norm, cos2, sin2, w_in],
        out_shape=out_shape, out_specs=out_specs, scratch=scratch,
        casts=casts, step_of=lambda b, i: b * per_b + i,
    )
    outs, groups = list(outs), []
    for f in fused:
        groups.append(("attn", outs.pop(0), outs.pop(0)) if f
                      else ("qkv", outs.pop(0), outs.pop(0), outs.pop(0)))
    return groups, cast_res


def _hgrn_mixer_kernel(x_ref, g_ref, lbraw_ref, w_ref, hn_ref, o_ref,
                       q_ref, i_ref, og_ref, kk_ref, b_ref, st_ref, reach_ref,
                       *, layer, tiles_per_seq, chunk, heads):
    h = _rms(x_ref[...], g_ref[...]).astype(BF16)
    raw = lbraw_ref[...]
    p = jnp.exp(raw - jnp.max(raw, axis=0, keepdims=True))
    p = p / jnp.sum(p, axis=0, keepdims=True)
    lb = jnp.sum(p[:layer + 1], axis=0, keepdims=True) - p[0:1]

    hw = q_ref.shape[1]
    sec = lambda k: w_ref[:, k * hw:(k + 1) * hw]
    q_ref[...] = (_dot(h, sec(0)) * (HGRN_HEAD_DIM ** -0.5)).astype(BF16)
    f = lb + (1.0 - lb) * jax.nn.sigmoid(_dot(h, sec(1)))
    kk_ref[...] = 1.0 - f
    i_ref[...] = _dot(h, sec(2)).astype(BF16)
    og = _dot(h, sec(3))
    og_ref[...] = (og * jax.nn.sigmoid(og)).astype(BF16)
    _hgrn_decay(jnp.log2(f), b_ref, reach_ref, chunk)

    @pl.when(pl.program_id(0) % tiles_per_seq == 0)
    def _():
        st_ref[...] = jnp.zeros_like(st_ref)

    def one_chunk(ci, carry):
        rows = pl.ds(pl.multiple_of(ci * chunk, chunk), chunk)
        midref_ok = reach_ref[ci] < HGRN_SAFE_EXPONENT
        for take, score_fn in ((midref_ok, _hgrn_scores_midref),
                               (jnp.logical_not(midref_ok), _hgrn_scores_bounded)):
            @pl.when(take)
            def _():
                _hgrn_chunk(q_ref, i_ref, og_ref, hn_ref, o_ref, st_ref,
                            b_ref[rows, :], kk_ref[rows, :], rows, score_fn, heads)
        return carry

    lax.fori_loop(0, x_ref.shape[0] // chunk, one_chunk, 0)


def _hgrn_mixer(x, norm, lb_raw, w_in, col_off, head_gain, layer, batch, casts=()):
    n, d = x.shape
    seq = n // batch
    tm = min(TOKEN_TILE, seq)
    width = lb_raw.shape[1]
    chunk = min(HGRN_CHUNK, tm)
    heads = width // HGRN_HEAD_DIM
    row = lambda i: (i, 0)
    tile = lambda dtype: pltpu.VMEM((tm, width), dtype)
    return _pallas(
        functools.partial(_hgrn_mixer_kernel, layer=layer, tiles_per_seq=seq // tm,
                          chunk=chunk, heads=heads),
        name="hgrn_mixer",
        grid=(n // tm,), sem=("arbitrary",),
        in_specs=[pl.BlockSpec((tm, d), row), _layer_block(norm, layer), _resident(lb_raw),
                  _col_block(w_in, col_off, 4 * width), _layer_block(head_gain, layer)],
        args=[x, norm, lb_raw, w_in, head_gain],
        out_shape=jax.ShapeDtypeStruct((n, width), BF16),
        out_specs=pl.BlockSpec((tm, width), row),
        scratch=[tile(BF16), tile(BF16), tile(BF16), tile(F32), tile(F32),
                 pltpu.VMEM((heads, HGRN_HEAD_DIM, HGRN_HEAD_DIM), F32),
                 pltpu.SMEM((tm // chunk,), F32)],
        casts=casts, step_of=lambda i: i,
    )


def _attn_kernel(q_ref, kp_ref, kc_ref, vp_ref, vc_ref, o_ref, st_ref, *, nres, nblk):
    _attn_window(q_ref, kp_ref, kc_ref, vp_ref, vc_ref, o_ref, st_ref,
                 pl.program_id(2) == 0, nres, nblk)


def _attn_window(q_ref, kp_ref, kc_ref, vp_ref, vc_ref, o_ref, st_ref, first, nres, nblk):
    row = lax.broadcasted_iota(jnp.int32, (ATT_BLOCK, 2 * ATT_BLOCK), 0)
    col = lax.broadcasted_iota(jnp.int32, (ATT_BLOCK, 2 * ATT_BLOCK), 1)
    band = jnp.logical_and(col >= row, col <= row + ATT_BLOCK)
    low = jnp.where(first, ATT_BLOCK, 0)
    band_first = jnp.logical_and(col >= jnp.maximum(row, low), col <= row + ATT_BLOCK)
    ones = jnp.ones((2 * ATT_BLOCK, ATT_HEAD_DIM), BF16)
    lane = lax.broadcasted_iota(jnp.int32, (ATT_BLOCK, LANES), 1)
    for res in range(nres):
        for blk in range(nblk):
            rows = slice(blk * ATT_BLOCK, (blk + 1) * ATT_BLOCK)
            stats = jnp.ones((ATT_BLOCK, LANES), F32)
            for hd in range(ATT_HEADS):
                sl = slice(hd * ATT_HEAD_DIM, (hd + 1) * ATT_HEAD_DIM)
                if blk == 0:
                    k2 = jnp.concatenate([kp_ref[res, :, sl], kc_ref[res, 0:ATT_BLOCK, sl]], axis=0)
                    v2 = jnp.concatenate([vp_ref[res, :, sl], vc_ref[res, 0:ATT_BLOCK, sl]], axis=0)
                    mask = band_first
                else:
                    win = slice((blk - 1) * ATT_BLOCK, (blk + 1) * ATT_BLOCK)
                    k2 = kc_ref[res, win, sl]
                    v2 = vc_ref[res, win, sl]
                    mask = band
                s = jnp.where(mask, _dot_nt(q_ref[res, rows, sl], k2), NEG_BIG)
                m = jnp.max(s, axis=-1, keepdims=True)
                p = jnp.exp2(s - m).astype(BF16)
                o_den = _dot(p, jnp.concatenate([v2, ones], axis=1))
                den = o_den[:, ATT_HEAD_DIM:]
                o_ref[res, rows, sl] = o_den[:, :ATT_HEAD_DIM].astype(BF16)
                stats = jnp.where(lane == hd, m, jnp.where(lane == ATT_HEADS + hd, den, stats))
            st_ref[res, rows, :] = stats


def _attn_group(q, k, v):
    batch, dil, cls, gw = q.shape
    assert cls % ATT_BLOCK == 0
    nblk = min(ATT_CHAINS, cls // ATT_BLOCK)
    nres = max(1, min(dil, ATT_CHAINS // nblk))
    qb = nblk * ATT_BLOCK
    cur = lambda b, r, i: (b, r, i, 0)
    prev = lambda b, r, i: (b, r, jnp.maximum(i * nblk - 1, 0), 0)
    cur_spec = pl.BlockSpec((None, nres, qb, gw), cur)
    prev_spec = pl.BlockSpec((None, nres, ATT_BLOCK, gw), prev)
    return pl.pallas_call(
        functools.partial(_attn_kernel, nres=nres, nblk=nblk),
        out_shape=(jax.ShapeDtypeStruct((batch, dil, cls, gw), BF16),
                   jax.ShapeDtypeStruct((batch, dil, cls, LANES), F32)),
        grid=(batch, dil // nres, cls // qb),
        in_specs=[cur_spec, prev_spec, cur_spec, prev_spec, cur_spec],
        out_specs=(pl.BlockSpec((None, nres, qb, gw), cur),
                   pl.BlockSpec((None, nres, qb, LANES), cur)),
        compiler_params=_params("parallel", "parallel", "arbitrary"),
        name=f"attn_dil{dil}",
    )(q, k, k, v, v)


def _hgrn_decay(lf_all, b_ref, reach_ref, c):
    r_i = lax.broadcasted_iota(jnp.int32, (c, c), 0)
    c_i = lax.broadcasted_iota(jnp.int32, (c, c), 1)
    tri = jnp.where(c_i <= r_i, 1.0, 0.0).astype(BF16)
    for k in range(lf_all.shape[0] // c):
        lf = lf_all[k * c:(k + 1) * c, :]
        lf_hi = lf.astype(BF16)
        lf_lo = (lf - lf_hi.astype(F32)).astype(BF16)
        b = _dot(tri, lf_hi) + _dot(tri, lf_lo)
        b_ref[k * c:(k + 1) * c, :] = b
        b_mid = b[c // 2 - 1:c // 2, :]
        reach_ref[k] = jnp.max(jnp.maximum(b[0:1, :] - b_mid, b_mid - b[c - 1:c, :]))


def _hgrn_scores_midref(q, kk, b, heads):
    c = b.shape[0]
    b_mid = b[c // 2 - 1:c // 2, :]
    b_last = b[c - 1:c, :]
    q_s = q * jnp.exp2(b - b_mid).astype(BF16)
    k_s = (kk * jnp.exp2(b_mid - b)).astype(BF16)
    q_in = q_s * jnp.exp2(b_mid).astype(BF16)
    k_out = k_s * jnp.exp2(b_last - b_mid).astype(BF16)
    r_i = lax.broadcasted_iota(jnp.int32, (c, c), 0)
    c_i = lax.broadcasted_iota(jnp.int32, (c, c), 1)
    causal = c_i <= r_i
    scores = []
    for hd in range(heads):
        sl = slice(hd * HGRN_HEAD_DIM, (hd + 1) * HGRN_HEAD_DIM)
        scores.append(jnp.where(causal, _dot_nt(q_s[:, sl], k_s[:, sl]), 0.0).astype(BF16))
    return scores, q_in, k_out


def _hgrn_scores_bounded(q, kk, b, heads):
    c, w = b.shape
    q = q.astype(F32)
    q_in = (q * jnp.exp2(b)).astype(BF16)
    k_out = (kk * jnp.exp2(b[c - 1:c, :] - b)).astype(BF16)
    r_i = lax.broadcasted_iota(jnp.int32, (c, c), 0)
    c_i = lax.broadcasted_iota(jnp.int32, (c, c), 1)
    rows = lax.broadcasted_iota(jnp.int32, (c, w), 0)

    levels = []
    half = SUBLANES
    while half < c:
        span = 2 * half
        pieces = [jnp.broadcast_to(b[s + half - 1:s + half, :], (span, w))
                  for s in range(0, c, span)]
        bref = pieces[0] if len(pieces) == 1 else jnp.concatenate(pieces, axis=0)
        upper = (rows & (span - 1)) >= half
        e = jnp.exp2(jnp.where(upper, b - bref, bref - b))
        q_up = jnp.where(upper, q * e, 0.0).astype(BF16)
        k_lo = jnp.where(upper, 0.0, kk * e).astype(BF16)
        same = (r_i & ~(span - 1)) == (c_i & ~(span - 1))
        levels.append((q_up, k_lo, same))
        half = span

    nb = c // SUBLANES
    sub = lax.broadcasted_iota(jnp.int32, (nb, SUBLANES, w), 1)
    q3 = q.reshape(nb, SUBLANES, w)
    k3 = kk.reshape(nb, SUBLANES, w)
    b3 = b.reshape(nb, SUBLANES, w)
    near = []
    for dist in range(SUBLANES):
        if dist == 0:
            prod = q3 * k3
        else:
            k_r = pltpu.roll(k3, dist, 1)
            b_r = pltpu.roll(b3, dist, 1)
            prod = jnp.where(sub >= dist, q3 * k_r * jnp.exp2(b3 - b_r), 0.0)
        near.append(prod.reshape(c, w))

    scores = []
    for hd in range(heads):
        sl = slice(hd * HGRN_HEAD_DIM, (hd + 1) * HGRN_HEAD_DIM)
        a = jnp.zeros((c, c), F32)
        for q_up, k_lo, same in levels:
            a = a + jnp.where(same, _dot_nt(q_up[:, sl], k_lo[:, sl]), 0.0)
        for dist in range(SUBLANES):
            col_val = jnp.sum(near[dist][:, sl], axis=-1, keepdims=True)
            a = a + jnp.where(c_i == r_i - dist, col_val, 0.0)
        scores.append(a.astype(BF16))
    return scores, q_in, k_out


def _hgrn_chunk(q_ref, i_ref, og_ref, hn_ref, o_ref, st_ref, b, kk, rows, score_fn, heads):
    c = b.shape[0]
    v_bf = i_ref[rows, :]
    scores, q_in, k_out = score_fn(q_ref[rows, :], kk, b, heads)
    st_decay = jnp.exp2(b[c - 1:c, :])
    gain = hn_ref[...]
    for hd in range(heads):
        sl = slice(hd * HGRN_HEAD_DIM, (hd + 1) * HGRN_HEAD_DIM)
        v_h = v_bf[:, sl]
        st_h = st_ref[hd]
        o_h = _dot(scores[hd], v_h) + _dot_nt(q_in[:, sl], st_h.astype(BF16))
        st_ref[hd] = st_h * st_decay[:, sl] + _dot_tn(v_h, k_out[:, sl])
        o_n = _rms(o_h, gain[:, sl])
        o_ref[rows, sl] = (o_n * og_ref[rows, sl].astype(F32)).astype(BF16)


def _load_by_token(src_ref, scr_ref):
    dil, rows, width = src_ref.shape
    if dil == 1:
        return src_ref[0].astype(F32)
    parts = []
    for c in range(width // LANES):
        cols = slice(c * LANES, (c + 1) * LANES)
        for r in range(dil):
            scr_ref[c, pl.ds(r, rows, stride=dil), :] = src_ref[r, :, cols].astype(F32)
        parts.append(scr_ref[c])
    return parts[0] if len(parts) == 1 else jnp.concatenate(parts, axis=-1)


def _mix_kernel(x_ref, g_ref, o0_ref, o1_ref, o2_ref, s0_ref, s1_ref, s2_ref, ob_ref,
                wg_ref, wa_ref, wb_ref, wo_ref, out_ref, *scr):
    x = x_ref[...]
    d = x.shape[1]
    h = _rms(x, g_ref[...]).astype(BF16)
    gate_a = jax.nn.sigmoid(_dot(h, wg_ref[:, :d]))
    gate_b = jax.nn.sigmoid(_dot(h, wg_ref[:, d:]))

    scr = list(scr)
    take = lambda ref: _load_by_token(ref, scr.pop(0) if ref.shape[0] > 1 else None)
    o0, o1, o2 = take(o0_ref), take(o1_ref), take(o2_ref)
    s0, s1, s2 = take(s0_ref), take(s1_ref), take(s2_ref)
    head_lane = lax.broadcasted_iota(jnp.int32, s0.shape, 1) < ATT_HEADS
    lse = [s + jnp.log2(jnp.where(head_lane, pltpu.roll(s, LANES - ATT_HEADS, 1), 1.0))
           for s in (s0, s1, s2)]
    mx = jnp.maximum(jnp.maximum(lse[0], lse[1]), lse[2])
    inv = 1.0 / (jnp.exp2(lse[0] - mx) + jnp.exp2(lse[1] - mx) + jnp.exp2(lse[2] - mx))
    w0, w1, w2 = (jnp.exp2(s - mx) * inv for s in (s0, s1, s2))
    parts = []
    for hd in range(ATT_HEADS):
        sl = slice(hd * ATT_HEAD_DIM, (hd + 1) * ATT_HEAD_DIM)
        parts.append(o0[:, sl] * w0[:, hd:hd + 1] + o1[:, sl] * w1[:, hd:hd + 1]
                     + o2[:, sl] * w2[:, hd:hd + 1])
    att = jnp.concatenate(parts, axis=-1).astype(BF16)

    y_a = _dot(att, wa_ref[...])
    y_b = _dot(ob_ref[...], wb_ref[...])
    merged = (gate_a * y_a + gate_b * y_b).astype(BF16)
    out_ref[...] = x + _dot(merged, wo_ref[...])


def _mix(x, norm, o_groups, s_groups, ob, w_in, gate_off, wa, wb, wo, layer, casts=()):
    n, d = x.shape
    batch = o_groups[0].shape[0]
    seq = n // batch
    tm = min(TOKEN_TILE, seq)
    per_b = seq // tm
    row = lambda b, i: (b * per_b + i, 0)
    rows = lambda a: pl.BlockSpec((tm, a.shape[1]), row)
    by_class = lambda a: pl.BlockSpec((None, a.shape[1], tm // a.shape[1], a.shape[3]),
                                      lambda b, i: (b, 0, i, 0))
    grouped = [*o_groups, *s_groups]
    args = [x, norm, *grouped, ob, w_in, wa, wb, wo]
    in_specs = ([rows(x), _layer_block(norm, layer)] + [by_class(a) for a in grouped]
                + [rows(ob), _col_block(w_in, gate_off, 2 * d),
                   _resident(wa), _resident(wb), _resident(wo)])
    scratch = [pltpu.VMEM((a.shape[3] // LANES, tm, LANES), F32)
               for a in grouped if a.shape[1] > 1]
    return _pallas(
        _mix_kernel,
        name="mix_out",
        grid=(batch, per_b), sem=("parallel", "parallel"),
        in_specs=in_specs, args=args,
        out_shape=jax.ShapeDtypeStruct((n, d), F32), out_specs=pl.BlockSpec((tm, d), row),
        scratch=scratch,
        casts=casts, step_of=lambda b, i: b * per_b + i,
    )


def _xattn_kernel(x_ref, g_ref, m_ref, mg_ref, wk_ref, wv_ref, wq_ref, wo_ref, out_ref,
                  k_ref, v_ref):
    @pl.when(pl.program_id(1) == 0)
    def _():
        hm = _rms(m_ref[...], mg_ref[...]).astype(BF16)
        k_ref[...] = _dot(hm, wk_ref[...]).astype(BF16)
        v_ref[...] = _dot(hm, wv_ref[...]).astype(BF16)

    x = x_ref[...]
    h = _rms(x, g_ref[...]).astype(BF16)
    dh = wq_ref.shape[1] // X_HEADS
    q = (_dot(h, wq_ref[...]) * (dh ** -0.5 * math.log2(math.e))).astype(BF16)
    outs = []
    for hd in range(X_HEADS):
        sl = slice(hd * dh, (hd + 1) * dh)
        s = _dot_nt(q[:, sl], k_ref[:, sl])
        m = jnp.max(s, axis=-1, keepdims=True)
        p = jnp.exp2(s - m)
        den = jnp.sum(p, axis=-1, keepdims=True)
        outs.append(_dot(p.astype(BF16), v_ref[:, sl]) / den)
    o = jnp.concatenate(outs, axis=-1).astype(BF16)
    out_ref[...] = x + _dot(o, wo_ref[...])


def _xattn(x, norm, mem, mem_norm, wkv, wq, wo, layer, batch, casts=()):
    n, d = x.shape
    seq = n // batch
    tm = min(TOKEN_TILE, seq)
    n_mem = mem.shape[0] // batch
    width = wkv.shape[1] // 2
    per_b = seq // tm
    row = lambda b, i: (b * per_b + i, 0)
    kv_scratch = pltpu.VMEM((n_mem, width), BF16)
    return _pallas(
        _xattn_kernel,
        name="xattn",
        grid=(batch, per_b), sem=("parallel", "arbitrary"),
        in_specs=[pl.BlockSpec((tm, d), row), _layer_block(norm, layer),
                  pl.BlockSpec((n_mem, d), lambda b, i: (b, 0)), _layer_block(mem_norm, layer),
                  _col_block(wkv, 0, width), _col_block(wkv, width, width),
                  _resident(wq), _resident(wo)],
        args=[x, norm, mem, mem_norm, wkv, wkv, wq, wo],
        out_shape=jax.ShapeDtypeStruct((n, d), F32), out_specs=pl.BlockSpec((tm, d), row),
        scratch=[kv_scratch, kv_scratch],
        casts=casts, step_of=lambda b, i: b * per_b + i,
    )


def kernel(x, mem, positions, ffn1_norm, ffn1_w_gu, ffn1_w_down, mix_norm, w_in,
           hgrn_lower_bounds, hgrn_head_norm, w_att_branch, w_hgrn_branch, w_mix_out,
           xattn_norm, mem_norm, xattn_wq, xattn_wkv, xattn_wo,
           ffn2_norm, ffn2_w_gu, ffn2_w_down, final_norm):
    batch, seq, d = x.shape
    depth = w_in.shape[0]
    n = batch * seq
    att_w = len(ATT_GROUPS) * ATT_GROUP_WIDTH
    hg_w = hgrn_head_norm.shape[1]
    hgrn_off = 3 * att_w
    gate_off = hgrn_off + 4 * hg_w

    xf = x.reshape(n, d)
    memf = mem.reshape(batch * mem.shape[1], d)
    (cos2, sin2), ffn1_w = _rope_tables(positions.reshape(n, 1).astype(F32),
                                        casts=[(ffn1_w_gu, 0), (ffn1_w_down, 0)])

    gain = lambda a: a.reshape(a.shape[0], 1, a.shape[1])
    ffn1_norm, mix_norm, hgrn_head_norm, xattn_norm, mem_norm, ffn2_norm = map(
        gain, (ffn1_norm, mix_norm, hgrn_head_norm, xattn_norm, mem_norm, ffn2_norm))

    for l in range(depth):
        xf, (w_in_l,) = _ffn(xf, ffn1_norm, *ffn1_w, l, casts=[(w_in, l)])

        groups, (w_att_l, w_hgrn_l, w_mix_l) = _proj_attn(
            xf, mix_norm, cos2, sin2, w_in_l, l, batch,
            casts=[(w_att_branch, l), (w_hgrn_branch, l), (w_mix_out, l)])
        o_groups, s_groups = [], []
        for kind, *arrs in groups:
            o_g, s_g = arrs if kind == "attn" else _attn_group(*arrs)
            o_groups.append(o_g)
            s_groups.append(s_g)
        o_b, (wkv_l, wq_l, wo_l) = _hgrn_mixer(
            xf, mix_norm, hgrn_lower_bounds, w_in_l, hgrn_off, hgrn_head_norm, l, batch,
            casts=[(xattn_wkv, l), (xattn_wq, l), (xattn_wo, l)])
        xf, _ = _mix(xf, mix_norm, o_groups, s_groups, o_b, w_in_l, gate_off,
                     w_att_l, w_hgrn_l, w_mix_l, l)

        xf, (ffn2_gu_l, ffn2_down_l) = _xattn(xf, xattn_norm, memf, mem_norm, wkv_l, wq_l, wo_l,
                                              l, batch,
                                              casts=[(ffn2_w_gu, l), (ffn2_w_down, l)])

        last = l == depth - 1
        xf, ffn1_w = _ffn(xf, ffn2_norm, ffn2_gu_l, ffn2_down_l, l,
                          final_g=final_norm.reshape(1, -1) if last else None,
                          casts=[] if last else [(ffn1_w_gu, l + 1), (ffn1_w_down, l + 1)])

    return xf.reshape(batch, seq, d)
```

```python
import functools
import math

import jax
import jax.numpy as jnp
from jax import lax
from jax.experimental import pallas as pl
from jax.experimental.pallas import tpu as pltpu

F32 = jnp.float32
BF16 = jnp.bfloat16

ATT_GROUPS = ((128, 1), (512, 4), (2048, 16))
ATT_HEADS = 4
ATT_HEAD_DIM = 128
ATT_GROUP_WIDTH = ATT_HEADS * ATT_HEAD_DIM
HGRN_HEAD_DIM = 128
X_HEADS = 4
ROPE_THETA = 10000.0
EPS = 1e-6

LANES = 128
SUBLANES = 8
MXU_TILE = 256
VMEM_LIMIT_BYTES = 56 * 1024 * 1024

TOKEN_TILE = 1024
FFN_CHUNK_TILES = 3
MXU_PERMUTE_MIN_STRIDE = 8
ATT_BLOCK = 128
ATT_CHAINS = 8
HGRN_CHUNK = 128
HGRN_SAFE_EXPONENT = 115.0
NEG_BIG = -1e30


def _params(*sem):
    return pltpu.CompilerParams(dimension_semantics=sem,
                                vmem_limit_bytes=VMEM_LIMIT_BYTES)


def _rms(x, g):
    ms = jnp.mean(x * x, axis=-1, keepdims=True)
    return x * lax.rsqrt(ms + EPS) * g


def _dot(a, b):
    return jnp.dot(a, b, preferred_element_type=F32)


def _dot_nt(a, b):
    return lax.dot_general(a, b, (((1,), (1,)), ((), ())), preferred_element_type=F32)


def _dot_tn(a, b):
    return lax.dot_general(a, b, (((0,), (0,)), ((), ())), preferred_element_type=F32)


def _resident(a):
    return pl.BlockSpec(a.shape, lambda *_: (0,) * a.ndim, pipeline_mode=pl.Buffered(1))


def _layer_block(a, layer):
    _, rows, width = a.shape
    return pl.BlockSpec((None, rows, width), lambda *_: (layer, 0, 0),
                        pipeline_mode=pl.Buffered(1))


def _col_block(a, off, size):
    return pl.BlockSpec((pl.Element(a.shape[0]), pl.Element(size)), lambda *_: (0, off),
                        pipeline_mode=pl.Buffered(1))


def _pallas(body, *, name, grid, sem, in_specs, args, out_shape, out_specs, scratch=(),
            casts=(), step_of=None):
    single = not isinstance(out_shape, (tuple, list))
    out_shape = [out_shape] if single else list(out_shape)
    out_specs = [out_specs] if single else list(out_specs)
    n_in, n_out, n_cast = len(in_specs), len(out_shape), len(casts)
    steps = math.prod(grid)
    cast_in, cast_out, cast_shape = [], [], []
    for a, layer in casts:
        _, rows, cols = a.shape
        chunk = rows // steps
        assert chunk * steps == rows and chunk % (2 * SUBLANES) == 0
        cast_in.append(pl.BlockSpec((None, chunk, cols),
                                    lambda *g, layer=layer: (layer, step_of(*g), 0)))
        cast_out.append(pl.BlockSpec((chunk, cols), lambda *g: (step_of(*g), 0)))
        cast_shape.append(jax.ShapeDtypeStruct((rows, cols), BF16))

    def hosted(*refs):
        ins = refs[:n_in]
        srcs = refs[n_in:n_in + n_cast]
        outs = refs[n_in + n_cast:n_in + n_cast + n_out]
        dsts = refs[n_in + n_cast + n_out:n_in + 2 * n_cast + n_out]
        for src, dst in zip(srcs, dsts):
            dst[...] = src[...].astype(BF16)
        body(*ins, *outs, *refs[n_in + 2 * n_cast + n_out:])

    res = pl.pallas_call(
        hosted,
        out_shape=tuple(out_shape + cast_shape),
        grid=grid,
        in_specs=list(in_specs) + cast_in,
        out_specs=tuple(out_specs + cast_out),
        scratch_shapes=list(scratch),
        compiler_params=_params(*sem),
        name=name,
    )(*args, *[a for a, _ in casts])
    main = res[:n_out]
    return (main[0] if single else tuple(main)), list(res[n_out:])


def _rope_table_kernel(pos_a_ref, pos_b_ref, cos_ref, sin_ref):
    half = ATT_HEAD_DIM // 2
    rows = pos_a_ref.shape[0]
    lane = lax.broadcasted_iota(jnp.int32, (1, ATT_HEAD_DIM), 1)
    upper = lane >= half
    idx = jnp.where(upper, lane - half, lane).astype(F32)
    inv_freq = jnp.exp(idx * (-2.0 * math.log(ROPE_THETA) / ATT_HEAD_DIM))
    ang = jnp.where(upper, pos_b_ref[...], pos_a_ref[...]) * inv_freq
    sign = jnp.where(upper, 1.0, -1.0)
    for val, out_ref, mult in ((jnp.cos(ang), cos_ref, 1.0), (jnp.sin(ang), sin_ref, sign)):
        swapped = pltpu.roll(val, half, 1)
        out_ref[0:rows, :] = jnp.where(upper, swapped, val) * mult
        out_ref[rows:2 * rows, :] = jnp.where(upper, val, swapped) * mult


def _rope_tables(pos_col, casts=()):
    n = pos_col.shape[0]
    tm = min(n, 2048)
    out = jax.ShapeDtypeStruct((n, ATT_HEAD_DIM), F32)
    return _pallas(
        _rope_table_kernel,
        name="rope_tables",
        grid=(n // tm,), sem=("parallel",),
        in_specs=[pl.BlockSpec((tm // 2, 1), lambda i: (2 * i, 0)),
                  pl.BlockSpec((tm // 2, 1), lambda i: (2 * i + 1, 0))],
        args=[pos_col, pos_col],
        out_shape=(out, out),
        out_specs=(pl.BlockSpec((tm, ATT_HEAD_DIM), lambda i: (i, 0)),) * 2,
        casts=casts, step_of=lambda i: i,
    )


def _ffn_chunks(d_ff):
    assert d_ff % MXU_TILE == 0
    tiles = d_ff // MXU_TILE
    n_chunks = -(-tiles // FFN_CHUNK_TILES)
    bounds = [MXU_TILE * (tiles * k // n_chunks) for k in range(n_chunks + 1)]
    return list(zip(bounds[:-1], bounds[1:]))


def _ffn_kernel(*refs, final):
    if final:
        x_ref, g_ref, wg_ref, wu_ref, wd_ref, fg_ref, o_ref = refs
    else:
        x_ref, g_ref, wg_ref, wu_ref, wd_ref, o_ref = refs
    x = x_ref[...]
    h = _rms(x, g_ref[...]).astype(BF16)
    acc = None
    for lo, hi in _ffn_chunks(wg_ref.shape[1]):
        gate = _dot(h, wg_ref[:, lo:hi])
        up = _dot(h, wu_ref[:, lo:hi])
        act = (gate * jax.nn.sigmoid(gate) * up).astype(BF16)
        part = _dot(act, wd_ref[lo:hi, :])
        acc = part if acc is None else acc + part
    y = x + 0.5 * acc
    if final:
        y = _rms(y, fg_ref[...])
    o_ref[...] = y


def _ffn(x, norm, w_gu, w_down, layer, final_g=None, casts=()):
    n, d = x.shape
    d_ff = w_down.shape[0]
    tm = min(TOKEN_TILE, n)
    final = final_g is not None
    row = pl.BlockSpec((tm, d), lambda i: (i, 0))
    in_specs = [row, _layer_block(norm, layer), _col_block(w_gu, 0, d_ff),
                _col_block(w_gu, d_ff, d_ff), _resident(w_down)]
    args = [x, norm, w_gu, w_gu, w_down]
    if final:
        in_specs.append(_resident(final_g))
        args.append(final_g)
    return _pallas(
        functools.partial(_ffn_kernel, final=final),
        name="ffn_final" if final else "ffn",
        grid=(n // tm,), sem=("parallel",),
        in_specs=in_specs, args=args,
        out_shape=jax.ShapeDtypeStruct((n, d), F32), out_specs=row,
        casts=casts, step_of=lambda i: i,
    )


def _class_permutation(dil):
    per = MXU_TILE // dil
    i = lax.broadcasted_iota(jnp.int32, (MXU_TILE, MXU_TILE), 0)
    j = lax.broadcasted_iota(jnp.int32, (MXU_TILE, MXU_TILE), 1)
    return jnp.where(j == (i % per) * dil + i // per, 1.0, 0.0).astype(BF16)


def _store_by_class(out_ref, scr_ref, val, dil):
    if dil == 1:
        out_ref[0] = val.astype(out_ref.dtype)
        return
    if dil < MXU_PERMUTE_MIN_STRIDE:
        rows = val.shape[0] // dil
        for c in range(val.shape[1] // LANES):
            cols = slice(c * LANES, (c + 1) * LANES)
            scr_ref[c] = val[:, cols]
            for r in range(dil):
                out_ref[r, :, cols] = (
                    scr_ref[c, pl.ds(r, rows, stride=dil), :].astype(out_ref.dtype))
        return
    val = val.astype(out_ref.dtype)
    perm = _class_permutation(dil)
    per = MXU_TILE // dil
    for s in range(val.shape[0] // MXU_TILE):
        y = _dot(perm, val[s * MXU_TILE:(s + 1) * MXU_TILE, :]).astype(out_ref.dtype)
        for r in range(dil):
            out_ref[r, s * per:(s + 1) * per, :] = y[r * per:(r + 1) * per, :]


def _proj_attn_kernel(x_ref, g_ref, cos_ref, sin_ref, w_ref, *rest, dils, fused, strided):
    n_g = len(dils)
    gw = ATT_GROUP_WIDTH
    n_out = sum(2 if f else 3 for f in fused)
    outs, scr = list(rest[:n_out]), list(rest[n_out:])
    stage = [scr.pop(0) for _ in range(3)] if strided else [None] * 3
    first = pl.program_id(1) == 0
    h = _rms(x_ref[...], g_ref[...]).astype(BF16)
    cos = cos_ref[...]
    sin = sin_ref[...]
    scale = ATT_HEAD_DIM ** -0.5 * math.log2(math.e)
    for gi, dil in enumerate(dils):
        part = lambda p: w_ref[:, (p * n_g + gi) * gw:(p * n_g + gi + 1) * gw]
        zq = _dot(h, part(0))
        zk = _dot(h, part(1))
        rq, rk = [], []
        for hd in range(ATT_HEADS):
            sl = slice(hd * ATT_HEAD_DIM, (hd + 1) * ATT_HEAD_DIM)
            xq = zq[:, sl]
            xk = zk[:, sl]
            rq.append((xq * cos + pltpu.roll(xq, ATT_HEAD_DIM // 2, 1) * sin) * scale)
            rk.append(xk * cos + pltpu.roll(xk, ATT_HEAD_DIM // 2, 1) * sin)
        if fused[gi]:
            o_ref, st_ref = outs.pop(0), outs.pop(0)
            q_ref, k_ref, v_ref, kp_ref, vp_ref = (scr.pop(0) for _ in range(5))
        else:
            q_ref, k_ref, v_ref = outs.pop(0), outs.pop(0), outs.pop(0)
        _store_by_class(q_ref, stage[0], jnp.concatenate(rq, axis=-1), dil)
        _store_by_class(k_ref, stage[1], jnp.concatenate(rk, axis=-1), dil)
        _store_by_class(v_ref, stage[2], _dot(h, part(2)), dil)
        if fused[gi]:
            @pl.when(first)
            def _():
                kp_ref[...] = jnp.zeros_like(kp_ref)
                vp_ref[...] = jnp.zeros_like(vp_ref)

            rows = q_ref.shape[1]
            _attn_window(q_ref, kp_ref, k_ref, vp_ref, v_ref, o_ref, st_ref,
                         first, dil, rows // ATT_BLOCK)
            kp_ref[...] = k_ref[:, rows - ATT_BLOCK:, :]
            vp_ref[...] = v_ref[:, rows - ATT_BLOCK:, :]


def _proj_attn(x, norm, cos2, sin2, w_in, layer, batch, casts=()):
    n, d = x.shape
    seq = n // batch
    tm = min(TOKEN_TILE, seq)
    per_b = seq // tm
    gw = ATT_GROUP_WIDTH
    dils = tuple(dil for _, dil in ATT_GROUPS)
    fused = tuple((tm // dil) % ATT_BLOCK == 0 for dil in dils)
    strided = any(1 < dil < MXU_PERMUTE_MIN_STRIDE for dil in dils)
    row = lambda b, i: (b * per_b + i, 0)
    by_class = lambda dil, width: pl.BlockSpec((None, dil, tm // dil, width),
                                               lambda b, i: (b, 0, i, 0))
    out_shape, out_specs = [], []
    scratch = [pltpu.VMEM((gw // LANES, tm, LANES), F32)] * 3 if strided else []
    for dil, f in zip(dils, fused):
        act = jax.ShapeDtypeStruct((batch, dil, seq // dil, gw), BF16)
        if f:
            out_shape += [act, jax.ShapeDtypeStruct((batch, dil, seq // dil, LANES), F32)]
            out_specs += [by_class(dil, gw), by_class(dil, LANES)]
            scratch += [pltpu.VMEM((dil, tm // dil, gw), BF16)] * 3
            scratch += [pltpu.VMEM((dil, ATT_BLOCK, gw), BF16)] * 2
        else:
            out_shape += [act] * 3
            out_specs += [by_class(dil, gw)] * 3
    outs, cast_res = _pallas(
        functools.partial(_proj_attn_kernel, dils=dils, fused=fused, strided=strided),
        name="proj_attn",
        grid=(batch, per_b), sem=("parallel", "arbitrary"),
        in_specs=[
            pl.BlockSpec((tm, d), row),
            _layer_block(norm, layer),
            pl.BlockSpec((tm, ATT_HEAD_DIM), row),
            pl.BlockSpec((tm, ATT_HEAD_DIM), row),
            _col_block(w_in, 0, 3 * len(dils) * gw),
        ],
        args=[x, norm, cos2, sin2, w_in],
        out_shape=out_shape, out_specs=out_specs, scratch=scratch,
        casts=casts, step_of=lambda b, i: b * per_b + i,
    )
    outs, groups = list(outs), []
    for f in fused:
        groups.append(("attn", outs.pop(0), outs.pop(0)) if f
                      else ("qkv", outs.pop(0), outs.pop(0), outs.pop(0)))
    return groups, cast_res


def _hgrn_mixer_kernel(x_ref, g_ref, lbraw_ref, w_ref, hn_ref, o_ref,
                       q_ref, i_ref, og_ref, kk_ref, b_ref, st_ref, reach_ref,
                       *, layer, tiles_per_seq, chunk, heads):
    h = _rms(x_ref[...], g_ref[...]).astype(BF16)
    raw = lbraw_ref[...]
    p = jnp.exp(raw - jnp.max(raw, axis=0, keepdims=True))
    p = p / jnp.sum(p, axis=0, keepdims=True)
    lb = jnp.sum(p[:layer + 1], axis=0, keepdims=True) - p[0:1]

    hw = q_ref.shape[1]
    sec = lambda k: w_ref[:, k * hw:(k + 1) * hw]
    q_ref[...] = (_dot(h, sec(0)) * (HGRN_HEAD_DIM ** -0.5)).astype(BF16)
    f = lb + (1.0 - lb) * jax.nn.sigmoid(_dot(h, sec(1)))
    kk_ref[...] = 1.0 - f
    i_ref[...] = _dot(h, sec(2)).astype(BF16)
    og = _dot(h, sec(3))
    og_ref[...] = (og * jax.nn.sigmoid(og)).astype(BF16)
    _hgrn_decay(jnp.log2(f), b_ref, reach_ref, chunk)

    @pl.when(pl.program_id(0) % tiles_per_seq == 0)
    def _():
        st_ref[...] = jnp.zeros_like(st_ref)

    n_chunks = x_ref.shape[0] // chunk
    reach = reach_ref[0]
    for k in range(1, n_chunks):
        reach = jnp.maximum(reach, reach_ref[k])
    all_midref = reach < HGRN_SAFE_EXPONENT

    @pl.when(all_midref)
    def _():
        for k in range(n_chunks):
            rows = slice(k * chunk, (k + 1) * chunk)
            _hgrn_chunk(q_ref, i_ref, og_ref, hn_ref, o_ref, st_ref,
                        b_ref[rows, :], kk_ref[rows, :], rows, _hgrn_scores_midref, heads)

    @pl.when(jnp.logical_not(all_midref))
    def _():
        def one_chunk(ci, carry):
            rows = pl.ds(pl.multiple_of(ci * chunk, chunk), chunk)
            midref_ok = reach_ref[ci] < HGRN_SAFE_EXPONENT
            for take, score_fn in ((midref_ok, _hgrn_scores_midref),
                                   (jnp.logical_not(midref_ok), _hgrn_scores_bounded)):
                @pl.when(take)
                def _():
                    _hgrn_chunk(q_ref, i_ref, og_ref, hn_ref, o_ref, st_ref,
                                b_ref[rows, :], kk_ref[rows, :], rows, score_fn, heads)
            return carry

        lax.fori_loop(0, n_chunks, one_chunk, 0)


def _hgrn_mixer(x, norm, lb_raw, w_in, col_off, head_gain, layer, batch, casts=()):
    n, d = x.shape
    seq = n // batch
    tm = min(TOKEN_TILE, seq)
    width = lb_raw.shape[1]
    chunk = min(HGRN_CHUNK, tm)
    heads = width // HGRN_HEAD_DIM
    row = lambda i: (i, 0)
    tile = lambda dtype: pltpu.VMEM((tm, width), dtype)
    return _pallas(
        functools.partial(_hgrn_mixer_kernel, layer=layer, tiles_per_seq=seq // tm,
                          chunk=chunk, heads=heads),
        name="hgrn_mixer",
        grid=(n // tm,), sem=("arbitrary",),
        in_specs=[pl.BlockSpec((tm, d), row), _layer_block(norm, layer), _resident(lb_raw),
                  _col_block(w_in, col_off, 4 * width), _layer_block(head_gain, layer)],
        args=[x, norm, lb_raw, w_in, head_gain],
        out_shape=jax.ShapeDtypeStruct((n, width), BF16),
        out_specs=pl.BlockSpec((tm, width), row),
        scratch=[tile(BF16), tile(BF16), tile(BF16), tile(F32), tile(F32),
                 pltpu.VMEM((heads, HGRN_HEAD_DIM, HGRN_HEAD_DIM), F32),
                 pltpu.SMEM((tm // chunk,), F32)],
        casts=casts, step_of=lambda i: i,
    )


def _attn_kernel(q_ref, kp_ref, kc_ref, vp_ref, vc_ref, o_ref, st_ref, *, nres, nblk):
    _attn_window(q_ref, kp_ref, kc_ref, vp_ref, vc_ref, o_ref, st_ref,
                 pl.program_id(2) == 0, nres, nblk)


def _attn_window(q_ref, kp_ref, kc_ref, vp_ref, vc_ref, o_ref, st_ref, first, nres, nblk):
    row = lax.broadcasted_iota(jnp.int32, (ATT_BLOCK, 2 * ATT_BLOCK), 0)
    col = lax.broadcasted_iota(jnp.int32, (ATT_BLOCK, 2 * ATT_BLOCK), 1)
    band = jnp.logical_and(col >= row, col <= row + ATT_BLOCK)
    low = jnp.where(first, ATT_BLOCK, 0)
    band_first = jnp.logical_and(col >= jnp.maximum(row, low), col <= row + ATT_BLOCK)
    ones = jnp.ones((2 * ATT_BLOCK, ATT_HEAD_DIM), BF16)
    lane = lax.broadcasted_iota(jnp.int32, (ATT_BLOCK, LANES), 1)
    for res in range(nres):
        for blk in range(nblk):
            rows = slice(blk * ATT_BLOCK, (blk + 1) * ATT_BLOCK)
            stats = jnp.ones((ATT_BLOCK, LANES), F32)
            for hd in range(ATT_HEADS):
                sl = slice(hd * ATT_HEAD_DIM, (hd + 1) * ATT_HEAD_DIM)
                if blk == 0:
                    k2 = jnp.concatenate([kp_ref[res, :, sl], kc_ref[res, 0:ATT_BLOCK, sl]], axis=0)
                    v2 = jnp.concatenate([vp_ref[res, :, sl], vc_ref[res, 0:ATT_BLOCK, sl]], axis=0)
                    mask = band_first
                else:
                    win = slice((blk - 1) * ATT_BLOCK, (blk + 1) * ATT_BLOCK)
                    k2 = kc_ref[res, win, sl]
                    v2 = vc_ref[res, win, sl]
                    mask = band
                s = jnp.where(mask, _dot_nt(q_ref[res, rows, sl], k2), NEG_BIG)
                m = jnp.max(s, axis=-1, keepdims=True)
                p = jnp.exp2(s - m).astype(BF16)
                o_den = _dot(p, jnp.concatenate([v2, ones], axis=1))
                den = o_den[:, ATT_HEAD_DIM:]
                o_ref[res, rows, sl] = o_den[:, :ATT_HEAD_DIM].astype(BF16)
                stats = jnp.where(lane == hd, m, jnp.where(lane == ATT_HEADS + hd, den, stats))
            st_ref[res, rows, :] = stats


def _attn_group(q, k, v):
    batch, dil, cls, gw = q.shape
    assert cls % ATT_BLOCK == 0
    nblk = min(ATT_CHAINS, cls // ATT_BLOCK)
    nres = max(1, min(dil, ATT_CHAINS // nblk))
    qb = nblk * ATT_BLOCK
    cur = lambda b, r, i: (b, r, i, 0)
    prev = lambda b, r, i: (b, r, jnp.maximum(i * nblk - 1, 0), 0)
    cur_spec = pl.BlockSpec((None, nres, qb, gw), cur)
    prev_spec = pl.BlockSpec((None, nres, ATT_BLOCK, gw), prev)
    return pl.pallas_call(
        functools.partial(_attn_kernel, nres=nres, nblk=nblk),
        out_shape=(jax.ShapeDtypeStruct((batch, dil, cls, gw), BF16),
                   jax.ShapeDtypeStruct((batch, dil, cls, LANES), F32)),
        grid=(batch, dil // nres, cls // qb),
        in_specs=[cur_spec, prev_spec, cur_spec, prev_spec, cur_spec],
        out_specs=(pl.BlockSpec((None, nres, qb, gw), cur),
                   pl.BlockSpec((None, nres, qb, LANES), cur)),
        compiler_params=_params("parallel", "parallel", "arbitrary"),
        name=f"attn_dil{dil}",
    )(q, k, k, v, v)


def _hgrn_decay(lf_all, b_ref, reach_ref, c):
    r_i = lax.broadcasted_iota(jnp.int32, (c, c), 0)
    c_i = lax.broadcasted_iota(jnp.int32, (c, c), 1)
    tri = jnp.where(c_i <= r_i, 1.0, 0.0).astype(BF16)
    for k in range(lf_all.shape[0] // c):
        lf = lf_all[k * c:(k + 1) * c, :]
        lf_hi = lf.astype(BF16)
        lf_lo = (lf - lf_hi.astype(F32)).astype(BF16)
        b = _dot(tri, lf_hi) + _dot(tri, lf_lo)
        b_ref[k * c:(k + 1) * c, :] = b
        b_mid = b[c // 2 - 1:c // 2, :]
        reach_ref[k] = jnp.max(jnp.maximum(b[0:1, :] - b_mid, b_mid - b[c - 1:c, :]))


def _hgrn_scores_midref(q, kk, b, heads):
    c = b.shape[0]
    b_mid = b[c // 2 - 1:c // 2, :]
    b_last = b[c - 1:c, :]
    q_s = q * jnp.exp2(b - b_mid).astype(BF16)
    k_s = (kk * jnp.exp2(b_mid - b)).astype(BF16)
    q_in = q_s * jnp.exp2(b_mid).astype(BF16)
    k_out = k_s * jnp.exp2(b_last - b_mid).astype(BF16)
    r_i = lax.broadcasted_iota(jnp.int32, (c, c), 0)
    c_i = lax.broadcasted_iota(jnp.int32, (c, c), 1)
    causal = c_i <= r_i
    scores = []
    for hd in range(heads):
        sl = slice(hd * HGRN_HEAD_DIM, (hd + 1) * HGRN_HEAD_DIM)
        scores.append(jnp.where(causal, _dot_nt(q_s[:, sl], k_s[:, sl]), 0.0).astype(BF16))
    return scores, q_in, k_out


def _hgrn_scores_bounded(q, kk, b, heads):
    c, w = b.shape
    q = q.astype(F32)
    q_in = (q * jnp.exp2(b)).astype(BF16)
    k_out = (kk * jnp.exp2(b[c - 1:c, :] - b)).astype(BF16)
    r_i = lax.broadcasted_iota(jnp.int32, (c, c), 0)
    c_i = lax.broadcasted_iota(jnp.int32, (c, c), 1)
    rows = lax.broadcasted_iota(jnp.int32, (c, w), 0)

    levels = []
    half = SUBLANES
    while half < c:
        span = 2 * half
        pieces = [jnp.broadcast_to(b[s + half - 1:s + half, :], (span, w))
                  for s in range(0, c, span)]
        bref = pieces[0] if len(pieces) == 1 else jnp.concatenate(pieces, axis=0)
        upper = (rows & (span - 1)) >= half
        e = jnp.exp2(jnp.where(upper, b - bref, bref - b))
        q_up = jnp.where(upper, q * e, 0.0).astype(BF16)
        k_lo = jnp.where(upper, 0.0, kk * e).astype(BF16)
        same = (r_i & ~(span - 1)) == (c_i & ~(span - 1))
        levels.append((q_up, k_lo, same))
        half = span

    nb = c // SUBLANES
    sub = lax.broadcasted_iota(jnp.int32, (nb, SUBLANES, w), 1)
    q3 = q.reshape(nb, SUBLANES, w)
    k3 = kk.reshape(nb, SUBLANES, w)
    b3 = b.reshape(nb, SUBLANES, w)
    near = []
    for dist in range(SUBLANES):
        if dist == 0:
            prod = q3 * k3
        else:
            k_r = pltpu.roll(k3, dist, 1)
            b_r = pltpu.roll(b3, dist, 1)
            prod = jnp.where(sub >= dist, q3 * k_r * jnp.exp2(b3 - b_r), 0.0)
        near.append(prod.reshape(c, w))

    scores = []
    for hd in range(heads):
        sl = slice(hd * HGRN_HEAD_DIM, (hd + 1) * HGRN_HEAD_DIM)
        a = jnp.zeros((c, c), F32)
        for q_up, k_lo, same in levels:
            a = a + jnp.where(same, _dot_nt(q_up[:, sl], k_lo[:, sl]), 0.0)
        for dist in range(SUBLANES):
            col_val = jnp.sum(near[dist][:, sl], axis=-1, keepdims=True)
            a = a + jnp.where(c_i == r_i - dist, col_val, 0.0)
        scores.append(a.astype(BF16))
    return scores, q_in, k_out


def _hgrn_chunk(q_ref, i_ref, og_ref, hn_ref, o_ref, st_ref, b, kk, rows, score_fn, heads):
    c = b.shape[0]
    v_bf = i_ref[rows, :]
    scores, q_in, k_out = score_fn(q_ref[rows, :], kk, b, heads)
    st_decay = jnp.exp2(b[c - 1:c, :])
    gain = hn_ref[...]
    for hd in range(heads):
        sl = slice(hd * HGRN_HEAD_DIM, (hd + 1) * HGRN_HEAD_DIM)
        v_h = v_bf[:, sl]
        st_h = st_ref[hd]
        o_h = _dot(scores[hd], v_h) + _dot_nt(q_in[:, sl], st_h.astype(BF16))
        st_ref[hd] = st_h * st_decay[:, sl] + _dot_tn(v_h, k_out[:, sl])
        o_n = _rms(o_h, gain[:, sl])
        o_ref[rows, sl] = (o_n * og_ref[rows, sl].astype(F32)).astype(BF16)


def _load_by_token(src_ref, scr_ref):
    dil, rows, width = src_ref.shape
    if dil == 1:
        return src_ref[0].astype(F32)
    parts = []
    for c in range(width // LANES):
        cols = slice(c * LANES, (c + 1) * LANES)
        for r in range(dil):
            scr_ref[c, pl.ds(r, rows, stride=dil), :] = src_ref[r, :, cols].astype(F32)
        parts.append(scr_ref[c])
    return parts[0] if len(parts) == 1 else jnp.concatenate(parts, axis=-1)


def _mix_kernel(x_ref, g_ref, o0_ref, o1_ref, o2_ref, s0_ref, s1_ref, s2_ref, ob_ref,
                wg_ref, wa_ref, wb_ref, wo_ref, out_ref, *scr):
    x = x_ref[...]
    d = x.shape[1]
    h = _rms(x, g_ref[...]).astype(BF16)
    gate_a = jax.nn.sigmoid(_dot(h, wg_ref[:, :d]))
    gate_b = jax.nn.sigmoid(_dot(h, wg_ref[:, d:]))

    scr = list(scr)
    take = lambda ref: _load_by_token(ref, scr.pop(0) if ref.shape[0] > 1 else None)
    o0, o1, o2 = take(o0_ref), take(o1_ref), take(o2_ref)
    s0, s1, s2 = take(s0_ref), take(s1_ref), take(s2_ref)
    head_lane = lax.broadcasted_iota(jnp.int32, s0.shape, 1) < ATT_HEADS
    lse = [s + jnp.log2(jnp.where(head_lane, pltpu.roll(s, LANES - ATT_HEADS, 1), 1.0))
           for s in (s0, s1, s2)]
    mx = jnp.maximum(jnp.maximum(lse[0], lse[1]), lse[2])
    inv = 1.0 / (jnp.exp2(lse[0] - mx) + jnp.exp2(lse[1] - mx) + jnp.exp2(lse[2] - mx))
    w0, w1, w2 = (jnp.exp2(s - mx) * inv for s in (s0, s1, s2))
    parts = []
    for hd in range(ATT_HEADS):
        sl = slice(hd * ATT_HEAD_DIM, (hd + 1) * ATT_HEAD_DIM)
        parts.append(o0[:, sl] * w0[:, hd:hd + 1] + o1[:, sl] * w1[:, hd:hd + 1]
                     + o2[:, sl] * w2[:, hd:hd + 1])
    att = jnp.concatenate(parts, axis=-1).astype(BF16)

    y_a = _dot(att, wa_ref[...])
    y_b = _dot(ob_ref[...], wb_ref[...])
    merged = (gate_a * y_a + gate_b * y_b).astype(BF16)
    out_ref[...] = x + _dot(merged, wo_ref[...])


def _mix(x, norm, o_groups, s_groups, ob, w_in, gate_off, wa, wb, wo, layer, casts=()):
    n, d = x.shape
    batch = o_groups[0].shape[0]
    seq = n // batch
    tm = min(TOKEN_TILE, seq)
    per_b = seq // tm
    row = lambda b, i: (b * per_b + i, 0)
    rows = lambda a: pl.BlockSpec((tm, a.shape[1]), row)
    by_class = lambda a: pl.BlockSpec((None, a.shape[1], tm // a.shape[1], a.shape[3]),
                                      lambda b, i: (b, 0, i, 0))
    grouped = [*o_groups, *s_groups]
    args = [x, norm, *grouped, ob, w_in, wa, wb, wo]
    in_specs = ([rows(x), _layer_block(norm, layer)] + [by_class(a) for a in grouped]
                + [rows(ob), _col_block(w_in, gate_off, 2 * d),
                   _resident(wa), _resident(wb), _resident(wo)])
    scratch = [pltpu.VMEM((a.shape[3] // LANES, tm, LANES), F32)
               for a in grouped if a.shape[1] > 1]
    return _pallas(
        _mix_kernel,
        name="mix_out",
        grid=(batch, per_b), sem=("parallel", "parallel"),
        in_specs=in_specs, args=args,
        out_shape=jax.ShapeDtypeStruct((n, d), F32), out_specs=pl.BlockSpec((tm, d), row),
        scratch=scratch,
        casts=casts, step_of=lambda b, i: b * per_b + i,
    )


def _xattn_kernel(x_ref, g_ref, m_ref, mg_ref, wk_ref, wv_ref, wq_ref, wo_ref, out_ref,
                  k_ref, v_ref):
    @pl.when(pl.program_id(1) == 0)
    def _():
        hm = _rms(m_ref[...], mg_ref[...]).astype(BF16)
        k_ref[...] = _dot(hm, wk_ref[...]).astype(BF16)
        v_ref[...] = _dot(hm, wv_ref[...]).astype(BF16)

    x = x_ref[...]
    h = _rms(x, g_ref[...]).astype(BF16)
    dh = wq_ref.shape[1] // X_HEADS
    q = (_dot(h, wq_ref[...]) * (dh ** -0.5 * math.log2(math.e))).astype(BF16)
    outs = []
    for hd in range(X_HEADS):
        sl = slice(hd * dh, (hd + 1) * dh)
        s = _dot_nt(q[:, sl], k_ref[:, sl])
        m = jnp.max(s, axis=-1, keepdims=True)
        p = jnp.exp2(s - m)
        den = jnp.sum(p, axis=-1, keepdims=True)
        outs.append(_dot(p.astype(BF16), v_ref[:, sl]) / den)
    o = jnp.concatenate(outs, axis=-1).astype(BF16)
    out_ref[...] = x + _dot(o, wo_ref[...])


def _xattn(x, norm, mem, mem_norm, wkv, wq, wo, layer, batch, casts=()):
    n, d = x.shape
    seq = n // batch
    tm = min(TOKEN_TILE, seq)
    n_mem = mem.shape[0] // batch
    width = wkv.shape[1] // 2
    per_b = seq // tm
    row = lambda b, i: (b * per_b + i, 0)
    kv_scratch = pltpu.VMEM((n_mem, width), BF16)
    return _pallas(
        _xattn_kernel,
        name="xattn",
        grid=(batch, per_b), sem=("parallel", "arbitrary"),
        in_specs=[pl.BlockSpec((tm, d), row), _layer_block(norm, layer),
                  pl.BlockSpec((n_mem, d), lambda b, i: (b, 0)), _layer_block(mem_norm, layer),
                  _col_block(wkv, 0, width), _col_block(wkv, width, width),
                  _resident(wq), _resident(wo)],
        args=[x, norm, mem, mem_norm, wkv, wkv, wq, wo],
        out_shape=jax.ShapeDtypeStruct((n, d), F32), out_specs=pl.BlockSpec((tm, d), row),
        scratch=[kv_scratch, kv_scratch],
        casts=casts, step_of=lambda b, i: b * per_b + i,
    )


def kernel(x, mem, positions, ffn1_norm, ffn1_w_gu, ffn1_w_down, mix_norm, w_in,
           hgrn_lower_bounds, hgrn_head_norm, w_att_branch, w_hgrn_branch, w_mix_out,
           xattn_norm, mem_norm, xattn_wq, xattn_wkv, xattn_wo,
           ffn2_norm, ffn2_w_gu, ffn2_w_down, final_norm):
    batch, seq, d = x.shape
    depth = w_in.shape[0]
    n = batch * seq
    att_w = len(ATT_GROUPS) * ATT_GROUP_WIDTH
    hg_w = hgrn_head_norm.shape[1]
    hgrn_off = 3 * att_w
    gate_off = hgrn_off + 4 * hg_w

    xf = x.reshape(n, d)
    memf = mem.reshape(batch * mem.shape[1], d)
    (cos2, sin2), ffn1_w = _rope_tables(positions.reshape(n, 1).astype(F32),
                                        casts=[(ffn1_w_gu, 0), (ffn1_w_down, 0)])

    gain = lambda a: a.reshape(a.shape[0], 1, a.shape[1])
    ffn1_norm, mix_norm, hgrn_head_norm, xattn_norm, mem_norm, ffn2_norm = map(
        gain, (ffn1_norm, mix_norm, hgrn_head_norm, xattn_norm, mem_norm, ffn2_norm))

    for l in range(depth):
        xf, (w_in_l,) = _ffn(xf, ffn1_norm, *ffn1_w, l, casts=[(w_in, l)])

        groups, (w_att_l, w_hgrn_l, w_mix_l) = _proj_attn(
            xf, mix_norm, cos2, sin2, w_in_l, l, batch,
            casts=[(w_att_branch, l), (w_hgrn_branch, l), (w_mix_out, l)])
        o_groups, s_groups = [], []
        for kind, *arrs in groups:
            o_g, s_g = arrs if kind == "attn" else _attn_group(*arrs)
            o_groups.append(o_g)
            s_groups.append(s_g)
        o_b, (wkv_l, wq_l, wo_l) = _hgrn_mixer(
            xf, mix_norm, hgrn_lower_bounds, w_in_l, hgrn_off, hgrn_head_norm, l, batch,
            casts=[(xattn_wkv, l), (xattn_wq, l), (xattn_wo, l)])
        xf, _ = _mix(xf, mix_norm, o_groups, s_groups, o_b, w_in_l, gate_off,
                     w_att_l, w_hgrn_l, w_mix_l, l)

        xf, (ffn2_gu_l, ffn2_down_l) = _xattn(xf, xattn_norm, memf, mem_norm, wkv_l, wq_l, wo_l,
                                              l, batch,
                                              casts=[(ffn2_w_gu, l), (ffn2_w_down, l)])

        last = l == depth - 1
        xf, ffn1_w = _ffn(xf, ffn2_norm, ffn2_gu_l, ffn2_down_l, l,
                          final_g=final_norm.reshape(1, -1) if last else None,
                          casts=[] if last else [(ffn1_w_gu, l + 1), (ffn1_w_down, l + 1)])

    return xf.reshape(batch, seq, d)
```

```python
import functools
import math

import jax
import jax.numpy as jnp
from jax import lax
from jax.experimental import pallas as pl
from jax.experimental.pallas import tpu as pltpu

F32 = jnp.float32
BF16 = jnp.bfloat16

ATT_GROUPS = ((128, 1), (512, 4), (2048, 16))
ATT_HEADS = 4
ATT_HEAD_DIM = 128
ATT_GROUP_WIDTH = ATT_HEADS * ATT_HEAD_DIM
HGRN_HEAD_DIM = 128
X_HEADS = 4
ROPE_THETA = 10000.0
EPS = 1e-6

LANES = 128
SUBLANES = 8
MXU_TILE = 256
VMEM_LIMIT_BYTES = 56 * 1024 * 1024

TOKEN_TILE = 1024
FFN_CHUNK_TILES = 3
MXU_PERMUTE_MIN_STRIDE = 8
ATT_BLOCK = 128
ATT_CHAINS = 8
HGRN_CHUNK = 128
HGRN_SAFE_EXPONENT = 115.0
NEG_BIG = -1e30


def _params(*sem):
    return pltpu.CompilerParams(dimension_semantics=sem,
                                vmem_limit_bytes=VMEM_LIMIT_BYTES)


def _rms(x, g):
    ms = jnp.mean(x * x, axis=-1, keepdims=True)
    return x * lax.rsqrt(ms + EPS) * g


def _dot(a, b):
    return jnp.dot(a, b, preferred_element_type=F32)


def _dot_nt(a, b):
    return lax.dot_general(a, b, (((1,), (1,)), ((), ())), preferred_element_type=F32)


def _dot_tn(a, b):
    return lax.dot_general(a, b, (((0,), (0,)), ((), ())), preferred_element_type=F32)


def _resident(a):
    return pl.BlockSpec(a.shape, lambda *_: (0,) * a.ndim, pipeline_mode=pl.Buffered(1))


def _layer_block(a, layer):
    _, rows, width = a.shape
    return pl.BlockSpec((None, rows, width), lambda *_: (layer, 0, 0),
                        pipeline_mode=pl.Buffered(1))


def _col_block(a, off, size):
    return pl.BlockSpec((pl.Element(a.shape[0]), pl.Element(size)), lambda *_: (0, off),
                        pipeline_mode=pl.Buffered(1))


def _pallas(body, *, name, grid, sem, in_specs, args, out_shape, out_specs, scratch=(),
            casts=(), step_of=None):
    single = not isinstance(out_shape, (tuple, list))
    out_shape = [out_shape] if single else list(out_shape)
    out_specs = [out_specs] if single else list(out_specs)
    n_in, n_out, n_cast = len(in_specs), len(out_shape), len(casts)
    steps = math.prod(grid)
    cast_in, cast_out, cast_shape = [], [], []
    for a, layer in casts:
        _, rows, cols = a.shape
        chunk = rows // steps
        assert chunk * steps == rows and chunk % (2 * SUBLANES) == 0
        cast_in.append(pl.BlockSpec((None, chunk, cols),
                                    lambda *g, layer=layer: (layer, step_of(*g), 0)))
        cast_out.append(pl.BlockSpec((chunk, cols), lambda *g: (step_of(*g), 0)))
        cast_shape.append(jax.ShapeDtypeStruct((rows, cols), BF16))

    def hosted(*refs):
        ins = refs[:n_in]
        srcs = refs[n_in:n_in + n_cast]
        outs = refs[n_in + n_cast:n_in + n_cast + n_out]
        dsts = refs[n_in + n_cast + n_out:n_in + 2 * n_cast + n_out]
        for src, dst in zip(srcs, dsts):
            dst[...] = src[...].astype(BF16)
        body(*ins, *outs, *refs[n_in + 2 * n_cast + n_out:])

    res = pl.pallas_call(
        hosted,
        out_shape=tuple(out_shape + cast_shape),
        grid=grid,
        in_specs=list(in_specs) + cast_in,
        out_specs=tuple(out_specs + cast_out),
        scratch_shapes=list(scratch),
        compiler_params=_params(*sem),
        name=name,
    )(*args, *[a for a, _ in casts])
    main = res[:n_out]
    return (main[0] if single else tuple(main)), list(res[n_out:])


def _rope_table_kernel(pos_a_ref, pos_b_ref, cos_ref, sin_ref):
    half = ATT_HEAD_DIM // 2
    rows = pos_a_ref.shape[0]
    lane = lax.broadcasted_iota(jnp.int32, (1, ATT_HEAD_DIM), 1)
    upper = lane >= half
    idx = jnp.where(upper, lane - half, lane).astype(F32)
    inv_freq = jnp.exp(idx * (-2.0 * math.log(ROPE_THETA) / ATT_HEAD_DIM))
    ang = jnp.where(upper, pos_b_ref[...], pos_a_ref[...]) * inv_freq
    sign = jnp.where(upper, 1.0, -1.0)
    for val, out_ref, mult in ((jnp.cos(ang), cos_ref, 1.0), (jnp.sin(ang), sin_ref, sign)):
        swapped = pltpu.roll(val, half, 1)
        out_ref[0:rows, :] = jnp.where(upper, swapped, val) * mult
        out_ref[rows:2 * rows, :] = jnp.where(upper, val, swapped) * mult


def _rope_tables(pos_col, casts=()):
    n = pos_col.shape[0]
    tm = min(n, 2048)
    out = jax.ShapeDtypeStruct((n, ATT_HEAD_DIM), F32)
    return _pallas(
        _rope_table_kernel,
        name="rope_tables",
        grid=(n // tm,), sem=("parallel",),
        in_specs=[pl.BlockSpec((tm // 2, 1), lambda i: (2 * i, 0)),
                  pl.BlockSpec((tm // 2, 1), lambda i: (2 * i + 1, 0))],
        args=[pos_col, pos_col],
        out_shape=(out, out),
        out_specs=(pl.BlockSpec((tm, ATT_HEAD_DIM), lambda i: (i, 0)),) * 2,
        casts=casts, step_of=lambda i: i,
    )


def _ffn_chunks(d_ff):
    assert d_ff % MXU_TILE == 0
    tiles = d_ff // MXU_TILE
    n_chunks = -(-tiles // FFN_CHUNK_TILES)
    bounds = [MXU_TILE * (tiles * k // n_chunks) for k in range(n_chunks + 1)]
    return list(zip(bounds[:-1], bounds[1:]))


def _ffn_kernel(*refs, final):
    if final:
        x_ref, g_ref, wg_ref, wu_ref, wd_ref, fg_ref, o_ref = refs
    else:
        x_ref, g_ref, wg_ref, wu_ref, wd_ref, o_ref = refs
    x = x_ref[...]
    h = _rms(x, g_ref[...]).astype(BF16)
    acc = None
    for lo, hi in _ffn_chunks(wg_ref.shape[1]):
        gate = _dot(h, wg_ref[:, lo:hi])
        up = _dot(h, wu_ref[:, lo:hi])
        act = (gate * jax.nn.sigmoid(gate) * up).astype(BF16)
        part = _dot(act, wd_ref[lo:hi, :])
        acc = part if acc is None else acc + part
    y = x + 0.5 * acc
    if final:
        y = _rms(y, fg_ref[...])
    o_ref[...] = y


def _ffn(x, norm, w_gu, w_down, layer, final_g=None, casts=()):
    n, d = x.shape
    d_ff = w_down.shape[0]
    tm = min(TOKEN_TILE, n)
    final = final_g is not None
    row = pl.BlockSpec((tm, d), lambda i: (i, 0))
    in_specs = [row, _layer_block(norm, layer), _col_block(w_gu, 0, d_ff),
                _col_block(w_gu, d_ff, d_ff), _resident(w_down)]
    args = [x, norm, w_gu, w_gu, w_down]
    if final:
        in_specs.append(_resident(final_g))
        args.append(final_g)
    return _pallas(
        functools.partial(_ffn_kernel, final=final),
        name="ffn_final" if final else "ffn",
        grid=(n // tm,), sem=("parallel",),
        in_specs=in_specs, args=args,
        out_shape=jax.ShapeDtypeStruct((n, d), F32), out_specs=row,
        casts=casts, step_of=lambda i: i,
    )


def _class_permutation(dil):
    per = MXU_TILE // dil
    i = lax.broadcasted_iota(jnp.int32, (MXU_TILE, MXU_TILE), 0)
    j = lax.broadcasted_iota(jnp.int32, (MXU_TILE, MXU_TILE), 1)
    return jnp.where(j == (i % per) * dil + i // per, 1.0, 0.0).astype(BF16)


def _store_by_class(out_ref, scr_ref, val, dil):
    if dil == 1:
        out_ref[0] = val.astype(out_ref.dtype)
        return
    if dil < MXU_PERMUTE_MIN_STRIDE:
        rows = val.shape[0] // dil
        for c in range(val.shape[1] // LANES):
            cols = slice(c * LANES, (c + 1) * LANES)
            scr_ref[c] = val[:, cols]
            for r in range(dil):
                out_ref[r, :, cols] = (
                    scr_ref[c, pl.ds(r, rows, stride=dil), :].astype(out_ref.dtype))
        return
    val = val.astype(out_ref.dtype)
    perm = _class_permutation(dil)
    per = MXU_TILE // dil
    for s in range(val.shape[0] // MXU_TILE):
        y = _dot(perm, val[s * MXU_TILE:(s + 1) * MXU_TILE, :]).astype(out_ref.dtype)
        for r in range(dil):
            out_ref[r, s * per:(s + 1) * per, :] = y[r * per:(r + 1) * per, :]


def _proj_attn_kernel(x_ref, g_ref, cos_ref, sin_ref, w_ref, *rest, dils, fused, strided):
    n_g = len(dils)
    gw = ATT_GROUP_WIDTH
    n_out = sum(2 if f else 3 for f in fused)
    outs, scr = list(rest[:n_out]), list(rest[n_out:])
    stage = [scr.pop(0) for _ in range(3)] if strided else [None] * 3
    first = pl.program_id(1) == 0
    h = _rms(x_ref[...], g_ref[...]).astype(BF16)
    cos = cos_ref[...]
    sin = sin_ref[...]
    scale = ATT_HEAD_DIM ** -0.5 * math.log2(math.e)
    for gi, dil in enumerate(dils):
        part = lambda p: w_ref[:, (p * n_g + gi) * gw:(p * n_g + gi + 1) * gw]
        zq = _dot(h, part(0))
        zk = _dot(h, part(1))
        rq, rk = [], []
        for hd in range(ATT_HEADS):
            sl = slice(hd * ATT_HEAD_DIM, (hd + 1) * ATT_HEAD_DIM)
            xq = zq[:, sl]
            xk = zk[:, sl]
            rq.append((xq * cos + pltpu.roll(xq, ATT_HEAD_DIM // 2, 1) * sin) * scale)
            rk.append(xk * cos + pltpu.roll(xk, ATT_HEAD_DIM // 2, 1) * sin)
        if fused[gi]:
            o_ref, st_ref = outs.pop(0), outs.pop(0)
            q_ref, k_ref, v_ref, kp_ref, vp_ref = (scr.pop(0) for _ in range(5))
        else:
            q_ref, k_ref, v_ref = outs.pop(0), outs.pop(0), outs.pop(0)
        _store_by_class(q_ref, stage[0], jnp.concatenate(rq, axis=-1), dil)
        _store_by_class(k_ref, stage[1], jnp.concatenate(rk, axis=-1), dil)
        _store_by_class(v_ref, stage[2], _dot(h, part(2)), dil)
        if fused[gi]:
            @pl.when(first)
            def _():
                kp_ref[...] = jnp.zeros_like(kp_ref)
                vp_ref[...] = jnp.zeros_like(vp_ref)

            rows = q_ref.shape[1]
            _attn_window(q_ref, kp_ref, k_ref, vp_ref, v_ref, o_ref, st_ref,
                         first, dil, rows // ATT_BLOCK)
            kp_ref[...] = k_ref[:, rows - ATT_BLOCK:, :]
            vp_ref[...] = v_ref[:, rows - ATT_BLOCK:, :]


def _proj_attn(x, norm, cos2, sin2, w_in, layer, batch, casts=()):
    n, d = x.shape
    seq = n // batch
    tm = min(TOKEN_TILE, seq)
    per_b = seq // tm
    gw = ATT_GROUP_WIDTH
    dils = tuple(dil for _, dil in ATT_GROUPS)
    fused = tuple((tm // dil) % ATT_BLOCK == 0 for dil in dils)
    strided = any(1 < dil < MXU_PERMUTE_MIN_STRIDE for dil in dils)
    row = lambda b, i: (b * per_b + i, 0)
    by_class = lambda dil, width: pl.BlockSpec((None, dil, tm // dil, width),
                                               lambda b, i: (b, 0, i, 0))
    out_shape, out_specs = [], []
    scratch = [pltpu.VMEM((gw // LANES, tm, LANES), F32)] * 3 if strided else []
    for dil, f in zip(dils, fused):
        act = jax.ShapeDtypeStruct((batch, dil, seq // dil, gw), BF16)
        if f:
            out_shape += [act, jax.ShapeDtypeStruct((batch, dil, seq // dil, LANES), F32)]
            out_specs += [by_class(dil, gw), by_class(dil, LANES)]
            scratch += [pltpu.VMEM((dil, tm // dil, gw), BF16)] * 3
            scratch += [pltpu.VMEM((dil, ATT_BLOCK, gw), BF16)] * 2
        else:
            out_shape += [act] * 3
            out_specs += [by_class(dil, gw)] * 3
    outs, cast_res = _pallas(
        functools.partial(_proj_attn_kernel, dils=dils, fused=fused, strided=strided),
        name="proj_attn",
        grid=(batch, per_b), sem=("parallel", "arbitrary"),
        in_specs=[
            pl.BlockSpec((tm, d), row),
            _layer_block(norm, layer),
            pl.BlockSpec((tm, ATT_HEAD_DIM), row),
            pl.BlockSpec((tm, ATT_HEAD_DIM), row),
            _col_block(w_in, 0, 3 * len(dils) * gw),
        ],
        args=[x, norm, cos2, sin2, w_in],
        out_shape=out_shape, out_specs=out_specs, scratch=scratch,
        casts=casts, step_of=lambda b, i: b * per_b + i,
    )
    outs, groups = list(outs), []
    for f in fused:
        groups.append(("attn", outs.pop(0), outs.pop(0)) if f
                      else ("qkv", outs.pop(0), outs.pop(0), outs.pop(0)))
    return groups, cast_res


def _hgrn_mixer_kernel(x_ref, g_ref, lbraw_ref, w_ref, hn_ref, o_ref,
                       q_ref, i_ref, og_ref, kk_ref, b_ref, st_ref, reach_ref,
                       *, layer, tiles_per_seq, chunk, heads):
    h = _rms(x_ref[...], g_ref[...]).astype(BF16)
    raw = lbraw_ref[...]
    p = jnp.exp(raw - jnp.max(raw, axis=0, keepdims=True))
    p = p / jnp.sum(p, axis=0, keepdims=True)
    lb = jnp.sum(p[:layer + 1], axis=0, keepdims=True) - p[0:1]

    hw = q_ref.shape[1]
    sec = lambda k: w_ref[:, k * hw:(k + 1) * hw]
    q_ref[...] = (_dot(h, sec(0)) * (HGRN_HEAD_DIM ** -0.5)).astype(BF16)
    f = lb + (1.0 - lb) * jax.nn.sigmoid(_dot(h, sec(1)))
    kk_ref[...] = 1.0 - f
    i_ref[...] = _dot(h, sec(2)).astype(BF16)
    og = _dot(h, sec(3))
    og_ref[...] = (og * jax.nn.sigmoid(og)).astype(BF16)
    _hgrn_decay(jnp.log2(f), b_ref, reach_ref, chunk)

    @pl.when(pl.program_id(0) % tiles_per_seq == 0)
    def _():
        st_ref[...] = jnp.zeros_like(st_ref)

    n_chunks = x_ref.shape[0] // chunk
    reach = reach_ref[0]
    for k in range(1, n_chunks):
        reach = jnp.maximum(reach, reach_ref[k])
    all_midref = reach < HGRN_SAFE_EXPONENT

    @pl.when(all_midref)
    def _():
        for k in range(n_chunks):
            rows = slice(k * chunk, (k + 1) * chunk)
            _hgrn_chunk(q_ref, i_ref, og_ref, hn_ref, o_ref, st_ref,
                        b_ref[rows, :], kk_ref[rows, :], rows, _hgrn_scores_midref, heads)

    @pl.when(jnp.logical_not(all_midref))
    def _():
        def one_chunk(ci, carry):
            rows = pl.ds(pl.multiple_of(ci * chunk, chunk), chunk)
            midref_ok = reach_ref[ci] < HGRN_SAFE_EXPONENT
            for take, score_fn in ((midref_ok, _hgrn_scores_midref),
                                   (jnp.logical_not(midref_ok), _hgrn_scores_bounded)):
                @pl.when(take)
                def _():
                    _hgrn_chunk(q_ref, i_ref, og_ref, hn_ref, o_ref, st_ref,
                                b_ref[rows, :], kk_ref[rows, :], rows, score_fn, heads)
            return carry

        lax.fori_loop(0, n_chunks, one_chunk, 0)


def _hgrn_mixer(x, norm, lb_raw, w_in, col_off, head_gain, layer, batch, casts=()):
    n, d = x.shape
    seq = n // batch
    tm = min(TOKEN_TILE, seq)
    width = lb_raw.shape[1]
    chunk = min(HGRN_CHUNK, tm)
    heads = width // HGRN_HEAD_DIM
    row = lambda i: (i, 0)
    tile = lambda dtype: pltpu.VMEM((tm, width), dtype)
    return _pallas(
        functools.partial(_hgrn_mixer_kernel, layer=layer, tiles_per_seq=seq // tm,
                          chunk=chunk, heads=heads),
        name="hgrn_mixer",
        grid=(n // tm,), sem=("arbitrary",),
        in_specs=[pl.BlockSpec((tm, d), row), _layer_block(norm, layer), _resident(lb_raw),
                  _col_block(w_in, col_off, 4 * width), _layer_block(head_gain, layer)],
        args=[x, norm, lb_raw, w_in, head_gain],
        out_shape=jax.ShapeDtypeStruct((n, width), BF16),
        out_specs=pl.BlockSpec((tm, width), row),
        scratch=[tile(BF16), tile(BF16), tile(BF16), tile(F32), tile(F32),
                 pltpu.VMEM((heads, HGRN_HEAD_DIM, HGRN_HEAD_DIM), F32),
                 pltpu.SMEM((tm // chunk,), F32)],
        casts=casts, step_of=lambda i: i,
    )


def _attn_kernel(q_ref, kp_ref, kc_ref, vp_ref, vc_ref, o_ref, st_ref, *, nres, nblk):
    _attn_window(q_ref, kp_ref, kc_ref, vp_ref, vc_ref, o_ref, st_ref,
                 pl.program_id(2) == 0, nres, nblk)


def _attn_whole_class_kernel(q_ref, k_ref, v_ref, o_ref, st_ref, *, nres, nblk):
    _attn_window(q_ref, k_ref, k_ref, v_ref, v_ref, o_ref, st_ref, True, nres, nblk)


def _attn_window(q_ref, kp_ref, kc_ref, vp_ref, vc_ref, o_ref, st_ref, first, nres, nblk):
    row = lax.broadcasted_iota(jnp.int32, (ATT_BLOCK, 2 * ATT_BLOCK), 0)
    col = lax.broadcasted_iota(jnp.int32, (ATT_BLOCK, 2 * ATT_BLOCK), 1)
    band = jnp.logical_and(col >= row, col <= row + ATT_BLOCK)
    low = jnp.where(first, ATT_BLOCK, 0)
    band_first = jnp.logical_and(col >= jnp.maximum(row, low), col <= row + ATT_BLOCK)
    ones = jnp.ones((2 * ATT_BLOCK, ATT_HEAD_DIM), BF16)
    lane = lax.broadcasted_iota(jnp.int32, (ATT_BLOCK, LANES), 1)
    for res in range(nres):
        for blk in range(nblk):
            rows = slice(blk * ATT_BLOCK, (blk + 1) * ATT_BLOCK)
            stats = jnp.ones((ATT_BLOCK, LANES), F32)
            for hd in range(ATT_HEADS):
                sl = slice(hd * ATT_HEAD_DIM, (hd + 1) * ATT_HEAD_DIM)
                if blk == 0:
                    head = slice(0, ATT_BLOCK)
                    k2 = jnp.concatenate([kp_ref[res, head, sl], kc_ref[res, head, sl]], axis=0)
                    v2 = jnp.concatenate([vp_ref[res, head, sl], vc_ref[res, head, sl]], axis=0)
                    mask = band_first
                else:
                    win = slice((blk - 1) * ATT_BLOCK, (blk + 1) * ATT_BLOCK)
                    k2 = kc_ref[res, win, sl]
                    v2 = vc_ref[res, win, sl]
                    mask = band
                s = jnp.where(mask, _dot_nt(q_ref[res, rows, sl], k2), NEG_BIG)
                m = jnp.max(s, axis=-1, keepdims=True)
                p = jnp.exp2(s - m).astype(BF16)
                o_den = _dot(p, jnp.concatenate([v2, ones], axis=1))
                den = o_den[:, ATT_HEAD_DIM:]
                o_ref[res, rows, sl] = o_den[:, :ATT_HEAD_DIM].astype(BF16)
                stats = jnp.where(lane == hd, m, jnp.where(lane == ATT_HEADS + hd, den, stats))
            st_ref[res, rows, :] = stats


def _attn_group(q, k, v):
    batch, dil, cls, gw = q.shape
    assert cls % ATT_BLOCK == 0
    nblk = min(ATT_CHAINS, cls // ATT_BLOCK)
    nres = max(1, min(dil, ATT_CHAINS // nblk))
    qb = nblk * ATT_BLOCK
    cur = lambda b, r, i: (b, r, i, 0)
    prev = lambda b, r, i: (b, r, jnp.maximum(i * nblk - 1, 0), 0)
    cur_spec = pl.BlockSpec((None, nres, qb, gw), cur)
    prev_spec = pl.BlockSpec((None, nres, ATT_BLOCK, gw), prev)
    whole_class = cls == qb
    body = _attn_whole_class_kernel if whole_class else _attn_kernel
    in_specs = ([cur_spec] * 3 if whole_class
                else [cur_spec, prev_spec, cur_spec, prev_spec, cur_spec])
    args = (q, k, v) if whole_class else (q, k, k, v, v)
    return pl.pallas_call(
        functools.partial(body, nres=nres, nblk=nblk),
        out_shape=(jax.ShapeDtypeStruct((batch, dil, cls, gw), BF16),
                   jax.ShapeDtypeStruct((batch, dil, cls, LANES), F32)),
        grid=(batch, dil // nres, cls // qb),
        in_specs=in_specs,
        out_specs=(pl.BlockSpec((None, nres, qb, gw), cur),
                   pl.BlockSpec((None, nres, qb, LANES), cur)),
        compiler_params=_params("parallel", "parallel", "arbitrary"),
        name=f"attn_dil{dil}",
    )(*args)


def _hgrn_decay(lf_all, b_ref, reach_ref, c):
    r_i = lax.broadcasted_iota(jnp.int32, (c, c), 0)
    c_i = lax.broadcasted_iota(jnp.int32, (c, c), 1)
    tri = jnp.where(c_i <= r_i, 1.0, 0.0).astype(BF16)
    for k in range(lf_all.shape[0] // c):
        lf = lf_all[k * c:(k + 1) * c, :]
        lf_hi = lf.astype(BF16)
        lf_lo = (lf - lf_hi.astype(F32)).astype(BF16)
        b = _dot(tri, lf_hi) + _dot(tri, lf_lo)
        b_ref[k * c:(k + 1) * c, :] = b
        b_mid = b[c // 2 - 1:c // 2, :]
        reach_ref[k] = jnp.max(jnp.maximum(b[0:1, :] - b_mid, b_mid - b[c - 1:c, :]))


def _hgrn_scores_midref(q, kk, b, heads):
    c = b.shape[0]
    b_mid = b[c // 2 - 1:c // 2, :]
    b_last = b[c - 1:c, :]
    q_s = q * jnp.exp2(b - b_mid).astype(BF16)
    k_s = (kk * jnp.exp2(b_mid - b)).astype(BF16)
    q_in = q_s * jnp.exp2(b_mid).astype(BF16)
    k_out = k_s * jnp.exp2(b_last - b_mid).astype(BF16)
    r_i = lax.broadcasted_iota(jnp.int32, (c, c), 0)
    c_i = lax.broadcasted_iota(jnp.int32, (c, c), 1)
    causal = c_i <= r_i
    scores = []
    for hd in range(heads):
        sl = slice(hd * HGRN_HEAD_DIM, (hd + 1) * HGRN_HEAD_DIM)
        scores.append(jnp.where(causal, _dot_nt(q_s[:, sl], k_s[:, sl]), 0.0).astype(BF16))
    return scores, q_in, k_out


def _hgrn_scores_bounded(q, kk, b, heads):
    c, w = b.shape
    q = q.astype(F32)
    q_in = (q * jnp.exp2(b)).astype(BF16)
    k_out = (kk * jnp.exp2(b[c - 1:c, :] - b)).astype(BF16)
    r_i = lax.broadcasted_iota(jnp.int32, (c, c), 0)
    c_i = lax.broadcasted_iota(jnp.int32, (c, c), 1)
    rows = lax.broadcasted_iota(jnp.int32, (c, w), 0)

    levels = []
    half = SUBLANES
    while half < c:
        span = 2 * half
        pieces = [jnp.broadcast_to(b[s + half - 1:s + half, :], (span, w))
                  for s in range(0, c, span)]
        bref = pieces[0] if len(pieces) == 1 else jnp.concatenate(pieces, axis=0)
        upper = (rows & (span - 1)) >= half
        e = jnp.exp2(jnp.where(upper, b - bref, bref - b))
        q_up = jnp.where(upper, q * e, 0.0).astype(BF16)
        k_lo = jnp.where(upper, 0.0, kk * e).astype(BF16)
        same = (r_i & ~(span - 1)) == (c_i & ~(span - 1))
        levels.append((q_up, k_lo, same))
        half = span

    nb = c // SUBLANES
    sub = lax.broadcasted_iota(jnp.int32, (nb, SUBLANES, w), 1)
    q3 = q.reshape(nb, SUBLANES, w)
    k3 = kk.reshape(nb, SUBLANES, w)
    b3 = b.reshape(nb, SUBLANES, w)
    near = []
    for dist in range(SUBLANES):
        if dist == 0:
            prod = q3 * k3
        else:
            k_r = pltpu.roll(k3, dist, 1)
            b_r = pltpu.roll(b3, dist, 1)
            prod = jnp.where(sub >= dist, q3 * k_r * jnp.exp2(b3 - b_r), 0.0)
        near.append(prod.reshape(c, w))

    scores = []
    for hd in range(heads):
        sl = slice(hd * HGRN_HEAD_DIM, (hd + 1) * HGRN_HEAD_DIM)
        a = jnp.zeros((c, c), F32)
        for q_up, k_lo, same in levels:
            a = a + jnp.where(same, _dot_nt(q_up[:, sl], k_lo[:, sl]), 0.0)
        for dist in range(SUBLANES):
            col_val = jnp.sum(near[dist][:, sl], axis=-1, keepdims=True)
            a = a + jnp.where(c_i == r_i - dist, col_val, 0.0)
        scores.append(a.astype(BF16))
    return scores, q_in, k_out


def _hgrn_chunk(q_ref, i_ref, og_ref, hn_ref, o_ref, st_ref, b, kk, rows, score_fn, heads):
    c = b.shape[0]
    v_bf = i_ref[rows, :]
    scores, q_in, k_out = score_fn(q_ref[rows, :], kk, b, heads)
    st_decay = jnp.exp2(b[c - 1:c, :])
    gain = hn_ref[...]
    for hd in range(heads):
        sl = slice(hd * HGRN_HEAD_DIM, (hd + 1) * HGRN_HEAD_DIM)
        v_h = v_bf[:, sl]
        st_h = st_ref[hd]
        o_h = _dot(scores[hd], v_h) + _dot_nt(q_in[:, sl], st_h.astype(BF16))
        st_ref[hd] = st_h * st_decay[:, sl] + _dot_tn(v_h, k_out[:, sl])
        o_n = _rms(o_h, gain[:, sl])
        o_ref[rows, sl] = (o_n * og_ref[rows, sl].astype(F32)).astype(BF16)


def _load_by_token(src_ref, scr_ref):
    dil, rows, width = src_ref.shape
    if dil == 1:
        return src_ref[0].astype(F32)
    parts = []
    for c in range(width // LANES):
        cols = slice(c * LANES, (c + 1) * LANES)
        for r in range(dil):
            scr_ref[c, pl.ds(r, rows, stride=dil), :] = src_ref[r, :, cols].astype(F32)
        parts.append(scr_ref[c])
    return parts[0] if len(parts) == 1 else jnp.concatenate(parts, axis=-1)


def _mix_kernel(x_ref, g_ref, o0_ref, o1_ref, o2_ref, s0_ref, s1_ref, s2_ref, ob_ref,
                wg_ref, wa_ref, wb_ref, wo_ref, out_ref, *scr):
    x = x_ref[...]
    d = x.shape[1]
    h = _rms(x, g_ref[...]).astype(BF16)
    gate_a = jax.nn.sigmoid(_dot(h, wg_ref[:, :d]))
    gate_b = jax.nn.sigmoid(_dot(h, wg_ref[:, d:]))

    scr = list(scr)
    take = lambda ref: _load_by_token(ref, scr.pop(0) if ref.shape[0] > 1 else None)
    o0, o1, o2 = take(o0_ref), take(o1_ref), take(o2_ref)
    s0, s1, s2 = take(s0_ref), take(s1_ref), take(s2_ref)
    head_lane = lax.broadcasted_iota(jnp.int32, s0.shape, 1) < ATT_HEADS
    lse = [s + jnp.log2(jnp.where(head_lane, pltpu.roll(s, LANES - ATT_HEADS, 1), 1.0))
           for s in (s0, s1, s2)]
    mx = jnp.maximum(jnp.maximum(lse[0], lse[1]), lse[2])
    inv = 1.0 / (jnp.exp2(lse[0] - mx) + jnp.exp2(lse[1] - mx) + jnp.exp2(lse[2] - mx))
    w0, w1, w2 = (jnp.exp2(s - mx) * inv for s in (s0, s1, s2))
    parts = []
    for hd in range(ATT_HEADS):
        sl = slice(hd * ATT_HEAD_DIM, (hd + 1) * ATT_HEAD_DIM)
        parts.append(o0[:, sl] * w0[:, hd:hd + 1] + o1[:, sl] * w1[:, hd:hd + 1]
                     + o2[:, sl] * w2[:, hd:hd + 1])
    att = jnp.concatenate(parts, axis=-1).astype(BF16)

    y_a = _dot(att, wa_ref[...])
    y_b = _dot(ob_ref[...], wb_ref[...])
    merged = (gate_a * y_a + gate_b * y_b).astype(BF16)
    out_ref[...] = x + _dot(merged, wo_ref[...])


def _mix(x, norm, o_groups, s_groups, ob, w_in, gate_off, wa, wb, wo, layer, casts=()):
    n, d = x.shape
    batch = o_groups[0].shape[0]
    seq = n // batch
    tm = min(TOKEN_TILE, seq)
    per_b = seq // tm
    row = lambda b, i: (b * per_b + i, 0)
    rows = lambda a: pl.BlockSpec((tm, a.shape[1]), row)
    by_class = lambda a: pl.BlockSpec((None, a.shape[1], tm // a.shape[1], a.shape[3]),
                                      lambda b, i: (b, 0, i, 0))
    grouped = [*o_groups, *s_groups]
    args = [x, norm, *grouped, ob, w_in, wa, wb, wo]
    in_specs = ([rows(x), _layer_block(norm, layer)] + [by_class(a) for a in grouped]
                + [rows(ob), _col_block(w_in, gate_off, 2 * d),
                   _resident(wa), _resident(wb), _resident(wo)])
    scratch = [pltpu.VMEM((a.shape[3] // LANES, tm, LANES), F32)
               for a in grouped if a.shape[1] > 1]
    return _pallas(
        _mix_kernel,
        name="mix_out",
        grid=(batch, per_b), sem=("parallel", "parallel"),
        in_specs=in_specs, args=args,
        out_shape=jax.ShapeDtypeStruct((n, d), F32), out_specs=pl.BlockSpec((tm, d), row),
        scratch=scratch,
        casts=casts, step_of=lambda b, i: b * per_b + i,
    )


def _xattn_kernel(x_ref, g_ref, m_ref, mg_ref, wk_ref, wv_ref, wq_ref, wo_ref, out_ref,
                  k_ref, v_ref):
    @pl.when(pl.program_id(1) == 0)
    def _():
        hm = _rms(m_ref[...], mg_ref[...]).astype(BF16)
        k_ref[...] = _dot(hm, wk_ref[...]).astype(BF16)
        v_ref[...] = _dot(hm, wv_ref[...]).astype(BF16)

    x = x_ref[...]
    h = _rms(x, g_ref[...]).astype(BF16)
    dh = wq_ref.shape[1] // X_HEADS
    q = (_dot(h, wq_ref[...]) * (dh ** -0.5 * math.log2(math.e))).astype(BF16)
    outs = []
    for hd in range(X_HEADS):
        sl = slice(hd * dh, (hd + 1) * dh)
        s = _dot_nt(q[:, sl], k_ref[:, sl])
        m = jnp.max(s, axis=-1, keepdims=True)
        p = jnp.exp2(s - m)
        den = jnp.sum(p, axis=-1, keepdims=True)
        outs.append(_dot(p.astype(BF16), v_ref[:, sl]) / den)
    o = jnp.concatenate(outs, axis=-1).astype(BF16)
    out_ref[...] = x + _dot(o, wo_ref[...])


def _xattn(x, norm, mem, mem_norm, wkv, wq, wo, layer, batch, casts=()):
    n, d = x.shape
    seq = n // batch
    tm = min(TOKEN_TILE, seq)
    n_mem = mem.shape[0] // batch
    width = wkv.shape[1] // 2
    per_b = seq // tm
    row = lambda b, i: (b * per_b + i, 0)
    kv_scratch = pltpu.VMEM((n_mem, width), BF16)
    return _pallas(
        _xattn_kernel,
        name="xattn",
        grid=(batch, per_b), sem=("parallel", "arbitrary"),
        in_specs=[pl.BlockSpec((tm, d), row), _layer_block(norm, layer),
                  pl.BlockSpec((n_mem, d), lambda b, i: (b, 0)), _layer_block(mem_norm, layer),
                  _col_block(wkv, 0, width), _col_block(wkv, width, width),
                  _resident(wq), _resident(wo)],
        args=[x, norm, mem, mem_norm, wkv, wkv, wq, wo],
        out_shape=jax.ShapeDtypeStruct((n, d), F32), out_specs=pl.BlockSpec((tm, d), row),
        scratch=[kv_scratch, kv_scratch],
        casts=casts, step_of=lambda b, i: b * per_b + i,
    )


def kernel(x, mem, positions, ffn1_norm, ffn1_w_gu, ffn1_w_down, mix_norm, w_in,
           hgrn_lower_bounds, hgrn_head_norm, w_att_branch, w_hgrn_branch, w_mix_out,
           xattn_norm, mem_norm, xattn_wq, xattn_wkv, xattn_wo,
           ffn2_norm, ffn2_w_gu, ffn2_w_down, final_norm):
    batch, seq, d = x.shape
    depth = w_in.shape[0]
    n = batch * seq
    att_w = len(ATT_GROUPS) * ATT_GROUP_WIDTH
    hg_w = hgrn_head_norm.shape[1]
    hgrn_off = 3 * att_w
    gate_off = hgrn_off + 4 * hg_w

    xf = x.reshape(n, d)
    memf = mem.reshape(batch * mem.shape[1], d)
    (cos2, sin2), ffn1_w = _rope_tables(positions.reshape(n, 1).astype(F32),
                                        casts=[(ffn1_w_gu, 0), (ffn1_w_down, 0)])

    gain = lambda a: a.reshape(a.shape[0], 1, a.shape[1])
    ffn1_norm, mix_norm, hgrn_head_norm, xattn_norm, mem_norm, ffn2_norm = map(
        gain, (ffn1_norm, mix_norm, hgrn_head_norm, xattn_norm, mem_norm, ffn2_norm))

    for l in range(depth):
        xf, (w_in_l,) = _ffn(xf, ffn1_norm, *ffn1_w, l, casts=[(w_in, l)])

        groups, (w_att_l, w_hgrn_l, w_mix_l) = _proj_attn(
            xf, mix_norm, cos2, sin2, w_in_l, l, batch,
            casts=[(w_att_branch, l), (w_hgrn_branch, l), (w_mix_out, l)])
        o_groups, s_groups = [], []
        for kind, *arrs in groups:
            o_g, s_g = arrs if kind == "attn" else _attn_group(*arrs)
            o_groups.append(o_g)
            s_groups.append(s_g)
        o_b, (wkv_l, wq_l, wo_l) = _hgrn_mixer(
            xf, mix_norm, hgrn_lower_bounds, w_in_l, hgrn_off, hgrn_head_norm, l, batch,
            casts=[(xattn_wkv, l), (xattn_wq, l), (xattn_wo, l)])
        xf, _ = _mix(xf, mix_norm, o_groups, s_groups, o_b, w_in_l, gate_off,
                     w_att_l, w_hgrn_l, w_mix_l, l)

        xf, (ffn2_gu_l, ffn2_down_l) = _xattn(xf, xattn_norm, memf, mem_norm, wkv_l, wq_l, wo_l,
                                              l, batch,
                                              casts=[(ffn2_w_gu, l), (ffn2_w_down, l)])

        last = l == depth - 1
        xf, ffn1_w = _ffn(xf, ffn2_norm, ffn2_gu_l, ffn2_down_l, l,
                          final_g=final_norm.reshape(1, -1) if last else None,
                          casts=[] if last else [(ffn1_w_gu, l + 1), (ffn1_w_down, l + 1)])

    return xf.reshape(batch, seq, d)
```

```python
import functools
import math

import jax
import jax.numpy as jnp
from jax import lax
from jax.experimental import pallas as pl
from jax.experimental.pallas import tpu as pltpu

F32 = jnp.float32
BF16 = jnp.bfloat16

ATT_GROUPS = ((128, 1), (512, 4), (2048, 16))
ATT_HEADS = 4
ATT_HEAD_DIM = 128
ATT_GROUP_WIDTH = ATT_HEADS * ATT_HEAD_DIM
HGRN_HEAD_DIM = 128
X_HEADS = 4
ROPE_THETA = 10000.0
EPS = 1e-6

LANES = 128
SUBLANES = 8
MXU_TILE = 256
VMEM_LIMIT_BYTES = 60 * 1024 * 1024

TOKEN_TILE = 1024
FFN_CHUNK_TILES = 3
MXU_PERMUTE_MIN_STRIDE = 8
ATT_BLOCK = 128
ATT_CHAINS = 8
HGRN_CHUNK = 128
HGRN_SAFE_EXPONENT = 115.0
NEG_BIG = -1e30


def _params(*sem):
    return pltpu.CompilerParams(dimension_semantics=sem,
                                vmem_limit_bytes=VMEM_LIMIT_BYTES)


def _rms(x, g):
    ms = jnp.mean(x * x, axis=-1, keepdims=True)
    return x * lax.rsqrt(ms + EPS) * g


def _dot(a, b):
    return jnp.dot(a, b, preferred_element_type=F32)


def _dot_nt(a, b):
    return lax.dot_general(a, b, (((1,), (1,)), ((), ())), preferred_element_type=F32)


def _dot_tn(a, b):
    return lax.dot_general(a, b, (((0,), (0,)), ((), ())), preferred_element_type=F32)


def _resident(a):
    return pl.BlockSpec(a.shape, lambda *_: (0,) * a.ndim, pipeline_mode=pl.Buffered(1))


def _layer_block(a, layer):
    _, rows, width = a.shape
    return pl.BlockSpec((None, rows, width), lambda *_: (layer, 0, 0),
                        pipeline_mode=pl.Buffered(1))


def _col_block(a, off, size):
    return pl.BlockSpec((pl.Element(a.shape[0]), pl.Element(size)), lambda *_: (0, off),
                        pipeline_mode=pl.Buffered(1))


def _pallas(body, *, name, grid, sem, in_specs, args, out_shape, out_specs, scratch=(),
            casts=(), step_of=None):
    single = not isinstance(out_shape, (tuple, list))
    out_shape = [out_shape] if single else list(out_shape)
    out_specs = [out_specs] if single else list(out_specs)
    n_in, n_out, n_cast = len(in_specs), len(out_shape), len(casts)
    steps = math.prod(grid)
    cast_in, cast_out, cast_shape = [], [], []
    for a, layer in casts:
        _, rows, cols = a.shape
        chunk = rows // steps
        assert chunk * steps == rows and chunk % (2 * SUBLANES) == 0
        cast_in.append(pl.BlockSpec((None, chunk, cols),
                                    lambda *g, layer=layer: (layer, step_of(*g), 0)))
        cast_out.append(pl.BlockSpec((chunk, cols), lambda *g: (step_of(*g), 0)))
        cast_shape.append(jax.ShapeDtypeStruct((rows, cols), BF16))

    def hosted(*refs):
        ins = refs[:n_in]
        srcs = refs[n_in:n_in + n_cast]
        outs = refs[n_in + n_cast:n_in + n_cast + n_out]
        dsts = refs[n_in + n_cast + n_out:n_in + 2 * n_cast + n_out]
        for src, dst in zip(srcs, dsts):
            dst[...] = src[...].astype(BF16)
        body(*ins, *outs, *refs[n_in + 2 * n_cast + n_out:])

    res = pl.pallas_call(
        hosted,
        out_shape=tuple(out_shape + cast_shape),
        grid=grid,
        in_specs=list(in_specs) + cast_in,
        out_specs=tuple(out_specs + cast_out),
        scratch_shapes=list(scratch),
        compiler_params=_params(*sem),
        name=name,
    )(*args, *[a for a, _ in casts])
    main = res[:n_out]
    return (main[0] if single else tuple(main)), list(res[n_out:])


def _rope_table_kernel(pos_a_ref, pos_b_ref, cos_ref, sin_ref):
    half = ATT_HEAD_DIM // 2
    rows = pos_a_ref.shape[0]
    lane = lax.broadcasted_iota(jnp.int32, (1, ATT_HEAD_DIM), 1)
    upper = lane >= half
    idx = jnp.where(upper, lane - half, lane).astype(F32)
    inv_freq = jnp.exp(idx * (-2.0 * math.log(ROPE_THETA) / ATT_HEAD_DIM))
    ang = jnp.where(upper, pos_b_ref[...], pos_a_ref[...]) * inv_freq
    sign = jnp.where(upper, 1.0, -1.0)
    for val, out_ref, mult in ((jnp.cos(ang), cos_ref, 1.0), (jnp.sin(ang), sin_ref, sign)):
        swapped = pltpu.roll(val, half, 1)
        out_ref[0:rows, :] = jnp.where(upper, swapped, val) * mult
        out_ref[rows:2 * rows, :] = jnp.where(upper, val, swapped) * mult


def _rope_tables(pos_col, casts=()):
    n = pos_col.shape[0]
    tm = min(n, 2048)
    out = jax.ShapeDtypeStruct((n, ATT_HEAD_DIM), F32)
    return _pallas(
        _rope_table_kernel,
        name="rope_tables",
        grid=(n // tm,), sem=("parallel",),
        in_specs=[pl.BlockSpec((tm // 2, 1), lambda i: (2 * i, 0)),
                  pl.BlockSpec((tm // 2, 1), lambda i: (2 * i + 1, 0))],
        args=[pos_col, pos_col],
        out_shape=(out, out),
        out_specs=(pl.BlockSpec((tm, ATT_HEAD_DIM), lambda i: (i, 0)),) * 2,
        casts=casts, step_of=lambda i: i,
    )


def _ffn_chunks(d_ff):
    assert d_ff % MXU_TILE == 0
    tiles = d_ff // MXU_TILE
    n_chunks = -(-tiles // FFN_CHUNK_TILES)
    bounds = [MXU_TILE * (tiles * k // n_chunks) for k in range(n_chunks + 1)]
    return list(zip(bounds[:-1], bounds[1:]))


def _ffn_kernel(*refs, final):
    if final:
        x_ref, g_ref, wg_ref, wu_ref, wd_ref, fg_ref, o_ref = refs
    else:
        x_ref, g_ref, wg_ref, wu_ref, wd_ref, o_ref = refs
    x = x_ref[...]
    h = _rms(x, g_ref[...]).astype(BF16)
    acc = None
    for lo, hi in _ffn_chunks(wg_ref.shape[1]):
        gate = _dot(h, wg_ref[:, lo:hi])
        up = _dot(h, wu_ref[:, lo:hi])
        act = (gate * jax.nn.sigmoid(gate) * up).astype(BF16)
        part = _dot(act, wd_ref[lo:hi, :])
        acc = part if acc is None else acc + part
    y = x + 0.5 * acc
    if final:
        y = _rms(y, fg_ref[...])
    o_ref[...] = y


def _ffn(x, norm, w_gu, w_down, layer, final_g=None, casts=()):
    n, d = x.shape
    d_ff = w_down.shape[0]
    tm = min(TOKEN_TILE, n)
    final = final_g is not None
    row = pl.BlockSpec((tm, d), lambda i: (i, 0))
    in_specs = [row, _layer_block(norm, layer), _col_block(w_gu, 0, d_ff),
                _col_block(w_gu, d_ff, d_ff), _resident(w_down)]
    args = [x, norm, w_gu, w_gu, w_down]
    if final:
        in_specs.append(_resident(final_g))
        args.append(final_g)
    return _pallas(
        functools.partial(_ffn_kernel, final=final),
        name="ffn_final" if final else "ffn",
        grid=(n // tm,), sem=("parallel",),
        in_specs=in_specs, args=args,
        out_shape=jax.ShapeDtypeStruct((n, d), F32), out_specs=row,
        casts=casts, step_of=lambda i: i,
    )


def _class_permutation(dil):
    per = MXU_TILE // dil
    i = lax.broadcasted_iota(jnp.int32, (MXU_TILE, MXU_TILE), 0)
    j = lax.broadcasted_iota(jnp.int32, (MXU_TILE, MXU_TILE), 1)
    return jnp.where(j == (i % per) * dil + i // per, 1.0, 0.0).astype(BF16)


def _store_by_class(out_ref, scr_ref, val, dil):
    if dil == 1:
        out_ref[0] = val.astype(out_ref.dtype)
        return
    if dil < MXU_PERMUTE_MIN_STRIDE:
        rows = val.shape[0] // dil
        for c in range(val.shape[1] // LANES):
            cols = slice(c * LANES, (c + 1) * LANES)
            scr_ref[c] = val[:, cols]
            for r in range(dil):
                out_ref[r, :, cols] = (
                    scr_ref[c, pl.ds(r, rows, stride=dil), :].astype(out_ref.dtype))
        return
    val = val.astype(out_ref.dtype)
    perm = _class_permutation(dil)
    per = MXU_TILE // dil
    for s in range(val.shape[0] // MXU_TILE):
        y = _dot(perm, val[s * MXU_TILE:(s + 1) * MXU_TILE, :]).astype(out_ref.dtype)
        for r in range(dil):
            out_ref[r, s * per:(s + 1) * per, :] = y[r * per:(r + 1) * per, :]


def _proj_attn_kernel(x_ref, g_ref, cos_ref, sin_ref, w_ref, *rest, dils, fused, strided):
    n_g = len(dils)
    gw = ATT_GROUP_WIDTH
    n_out = sum(2 if f else 3 for f in fused)
    outs, scr = list(rest[:n_out]), list(rest[n_out:])
    stage = [scr.pop(0) for _ in range(3)] if strided else [None] * 3
    first = pl.program_id(1) == 0
    h = _rms(x_ref[...], g_ref[...]).astype(BF16)
    cos = cos_ref[...]
    sin = sin_ref[...]
    scale = ATT_HEAD_DIM ** -0.5 * math.log2(math.e)
    for gi, dil in enumerate(dils):
        part = lambda p: w_ref[:, (p * n_g + gi) * gw:(p * n_g + gi + 1) * gw]
        zq = _dot(h, part(0))
        zk = _dot(h, part(1))
        rq, rk = [], []
        for hd in range(ATT_HEADS):
            sl = slice(hd * ATT_HEAD_DIM, (hd + 1) * ATT_HEAD_DIM)
            xq = zq[:, sl]
            xk = zk[:, sl]
            rq.append((xq * cos + pltpu.roll(xq, ATT_HEAD_DIM // 2, 1) * sin) * scale)
            rk.append(xk * cos + pltpu.roll(xk, ATT_HEAD_DIM // 2, 1) * sin)
        if fused[gi]:
            o_ref, st_ref = outs.pop(0), outs.pop(0)
            q_ref, k_ref, v_ref, kp_ref, vp_ref = (scr.pop(0) for _ in range(5))
        else:
            q_ref, k_ref, v_ref = outs.pop(0), outs.pop(0), outs.pop(0)
        _store_by_class(q_ref, stage[0], jnp.concatenate(rq, axis=-1), dil)
        _store_by_class(k_ref, stage[1], jnp.concatenate(rk, axis=-1), dil)
        _store_by_class(v_ref, stage[2], _dot(h, part(2)), dil)
        if fused[gi]:
            @pl.when(first)
            def _():
                kp_ref[...] = jnp.zeros_like(kp_ref)
                vp_ref[...] = jnp.zeros_like(vp_ref)

            rows = q_ref.shape[1]
            _attn_window(q_ref, kp_ref, k_ref, vp_ref, v_ref, o_ref, st_ref,
                         first, dil, rows // ATT_BLOCK)
            kp_ref[...] = k_ref[:, rows - ATT_BLOCK:, :]
            vp_ref[...] = v_ref[:, rows - ATT_BLOCK:, :]


def _proj_attn(x, norm, cos2, sin2, w_in, layer, batch, casts=()):
    n, d = x.shape
    seq = n // batch
    tm = min(TOKEN_TILE, seq)
    per_b = seq // tm
    gw = ATT_GROUP_WIDTH
    dils = tuple(dil for _, dil in ATT_GROUPS)
    fused = tuple((tm // dil) % ATT_BLOCK == 0 for dil in dils)
    strided = any(1 < dil < MXU_PERMUTE_MIN_STRIDE for dil in dils)
    row = lambda b, i: (b * per_b + i, 0)
    by_class = lambda dil, width: pl.BlockSpec((None, dil, tm // dil, width),
                                               lambda b, i: (b, 0, i, 0))
    out_shape, out_specs = [], []
    scratch = [pltpu.VMEM((gw // LANES, tm, LANES), F32)] * 3 if strided else []
    for dil, f in zip(dils, fused):
        act = jax.ShapeDtypeStruct((batch, dil, seq // dil, gw), BF16)
        if f:
            out_shape += [act, jax.ShapeDtypeStruct((batch, dil, seq // dil, LANES), F32)]
            out_specs += [by_class(dil, gw), by_class(dil, LANES)]
            scratch += [pltpu.VMEM((dil, tm // dil, gw), BF16)] * 3
            scratch += [pltpu.VMEM((dil, ATT_BLOCK, gw), BF16)] * 2
        else:
            out_shape += [act] * 3
            out_specs += [by_class(dil, gw)] * 3
    outs, cast_res = _pallas(
        functools.partial(_proj_attn_kernel, dils=dils, fused=fused, strided=strided),
        name="proj_attn",
        grid=(batch, per_b), sem=("parallel", "arbitrary"),
        in_specs=[
            pl.BlockSpec((tm, d), row),
            _layer_block(norm, layer),
            pl.BlockSpec((tm, ATT_HEAD_DIM), row),
            pl.BlockSpec((tm, ATT_HEAD_DIM), row),
            _col_block(w_in, 0, 3 * len(dils) * gw),
        ],
        args=[x, norm, cos2, sin2, w_in],
        out_shape=out_shape, out_specs=out_specs, scratch=scratch,
        casts=casts, step_of=lambda b, i: b * per_b + i,
    )
    outs, groups = list(outs), []
    for f in fused:
        groups.append(("attn", outs.pop(0), outs.pop(0)) if f
                      else ("qkv", outs.pop(0), outs.pop(0), outs.pop(0)))
    return groups, cast_res


def _hgrn_mixer_kernel(x_ref, g_ref, lbraw_ref, w_ref, hn_ref, *rest,
                       layer, tiles_per_seq, chunk, heads, guest_attention):
    rest = list(rest)
    if guest_attention:
        aq_ref, ak_ref, av_ref = rest[:3]
        o_ref, ao_ref, ast_ref = rest[3:6]
        rest = rest[6:]
        _attn_window(aq_ref, ak_ref, ak_ref, av_ref, av_ref, ao_ref, ast_ref, True,
                     aq_ref.shape[0], aq_ref.shape[1] // ATT_BLOCK)
    else:
        o_ref = rest.pop(0)
    q_ref, i_ref, og_ref, kk_ref, b_ref, st_ref, reach_ref = rest
    h = _rms(x_ref[...], g_ref[...]).astype(BF16)
    raw = lbraw_ref[...]
    p = jnp.exp(raw - jnp.max(raw, axis=0, keepdims=True))
    p = p / jnp.sum(p, axis=0, keepdims=True)
    lb = jnp.sum(p[:layer + 1], axis=0, keepdims=True) - p[0:1]

    hw = q_ref.shape[1]
    sec = lambda k: w_ref[:, k * hw:(k + 1) * hw]
    q_ref[...] = (_dot(h, sec(0)) * (HGRN_HEAD_DIM ** -0.5)).astype(BF16)
    f = lb + (1.0 - lb) * jax.nn.sigmoid(_dot(h, sec(1)))
    kk_ref[...] = 1.0 - f
    i_ref[...] = _dot(h, sec(2)).astype(BF16)
    og = _dot(h, sec(3))
    og_ref[...] = (og * jax.nn.sigmoid(og)).astype(BF16)
    _hgrn_decay(jnp.log2(f), b_ref, reach_ref, chunk)

    @pl.when(pl.program_id(0) % tiles_per_seq == 0)
    def _():
        st_ref[...] = jnp.zeros_like(st_ref)

    n_chunks = x_ref.shape[0] // chunk
    reach = reach_ref[0]
    for k in range(1, n_chunks):
        reach = jnp.maximum(reach, reach_ref[k])
    all_midref = reach < HGRN_SAFE_EXPONENT

    @pl.when(all_midref)
    def _():
        for k in range(n_chunks):
            rows = slice(k * chunk, (k + 1) * chunk)
            _hgrn_chunk(q_ref, i_ref, og_ref, hn_ref, o_ref, st_ref,
                        b_ref[rows, :], kk_ref[rows, :], rows, _hgrn_scores_midref, heads)

    @pl.when(jnp.logical_not(all_midref))
    def _():
        def one_chunk(ci, carry):
            rows = pl.ds(pl.multiple_of(ci * chunk, chunk), chunk)
            midref_ok = reach_ref[ci] < HGRN_SAFE_EXPONENT
            for take, score_fn in ((midref_ok, _hgrn_scores_midref),
                                   (jnp.logical_not(midref_ok), _hgrn_scores_bounded)):
                @pl.when(take)
                def _():
                    _hgrn_chunk(q_ref, i_ref, og_ref, hn_ref, o_ref, st_ref,
                                b_ref[rows, :], kk_ref[rows, :], rows, score_fn, heads)
            return carry

        lax.fori_loop(0, n_chunks, one_chunk, 0)


def _can_host_attention(qkv, n_tokens, batch):
    dil, cls = qkv[0].shape[1:3]
    tiles = (n_tokens // batch) // min(TOKEN_TILE, n_tokens // batch)
    return dil % tiles == 0 and cls % ATT_BLOCK == 0


def _hgrn_mixer(x, norm, lb_raw, w_in, col_off, head_gain, layer, batch, guest_qkv=None,
                casts=()):
    n, d = x.shape
    seq = n // batch
    tm = min(TOKEN_TILE, seq)
    tiles = seq // tm
    width = lb_raw.shape[1]
    chunk = min(HGRN_CHUNK, tm)
    heads = width // HGRN_HEAD_DIM
    row = lambda i: (i, 0)
    tile = lambda dtype: pltpu.VMEM((tm, width), dtype)
    in_specs = [pl.BlockSpec((tm, d), row), _layer_block(norm, layer), _resident(lb_raw),
                _col_block(w_in, col_off, 4 * width), _layer_block(head_gain, layer)]
    args = [x, norm, lb_raw, w_in, head_gain]
    out_shape = [jax.ShapeDtypeStruct((n, width), BF16)]
    out_specs = [pl.BlockSpec((tm, width), row)]
    if guest_qkv is not None:
        _, dil, cls, gw = guest_qkv[0].shape
        share = lambda w: pl.BlockSpec((None, dil // tiles, cls, w),
                                       lambda i: (i // tiles, i % tiles, 0, 0))
        in_specs += [share(gw)] * 3
        args += list(guest_qkv)
        out_shape += [jax.ShapeDtypeStruct((batch, dil, cls, gw), BF16),
                      jax.ShapeDtypeStruct((batch, dil, cls, LANES), F32)]
        out_specs += [share(gw), share(LANES)]
    outs, cast_res = _pallas(
        functools.partial(_hgrn_mixer_kernel, layer=layer, tiles_per_seq=tiles,
                          chunk=chunk, heads=heads, guest_attention=guest_qkv is not None),
        name="hgrn_mixer",
        grid=(n // tm,), sem=("arbitrary",),
        in_specs=in_specs, args=args, out_shape=out_shape, out_specs=out_specs,
        scratch=[tile(BF16), tile(BF16), tile(BF16), tile(F32), tile(F32),
                 pltpu.VMEM((heads, HGRN_HEAD_DIM, HGRN_HEAD_DIM), F32),
                 pltpu.SMEM((tm // chunk,), F32)],
        casts=casts, step_of=lambda i: i,
    )
    return outs[0], list(outs[1:]), cast_res


def _attn_kernel(q_ref, kp_ref, kc_ref, vp_ref, vc_ref, o_ref, st_ref, *, nres, nblk):
    _attn_window(q_ref, kp_ref, kc_ref, vp_ref, vc_ref, o_ref, st_ref,
                 pl.program_id(2) == 0, nres, nblk)


def _attn_whole_class_kernel(q_ref, k_ref, v_ref, o_ref, st_ref, *, nres, nblk):
    _attn_window(q_ref, k_ref, k_ref, v_ref, v_ref, o_ref, st_ref, True, nres, nblk)


def _attn_window(q_ref, kp_ref, kc_ref, vp_ref, vc_ref, o_ref, st_ref, first, nres, nblk):
    row = lax.broadcasted_iota(jnp.int32, (ATT_BLOCK, 2 * ATT_BLOCK), 0)
    col = lax.broadcasted_iota(jnp.int32, (ATT_BLOCK, 2 * ATT_BLOCK), 1)
    band = jnp.logical_and(col >= row, col <= row + ATT_BLOCK)
    low = jnp.where(first, ATT_BLOCK, 0)
    band_first = jnp.logical_and(col >= jnp.maximum(row, low), col <= row + ATT_BLOCK)
    ones = jnp.ones((2 * ATT_BLOCK, ATT_HEAD_DIM), BF16)
    lane = lax.broadcasted_iota(jnp.int32, (ATT_BLOCK, LANES), 1)
    for res in range(nres):
        for blk in range(nblk):
            rows = slice(blk * ATT_BLOCK, (blk + 1) * ATT_BLOCK)
            stats = jnp.ones((ATT_BLOCK, LANES), F32)
            for hd in range(ATT_HEADS):
                sl = slice(hd * ATT_HEAD_DIM, (hd + 1) * ATT_HEAD_DIM)
                if blk == 0:
                    head = slice(0, ATT_BLOCK)
                    k2 = jnp.concatenate([kp_ref[res, head, sl], kc_ref[res, head, sl]], axis=0)
                    v2 = jnp.concatenate([vp_ref[res, head, sl], vc_ref[res, head, sl]], axis=0)
                    mask = band_first
                else:
                    win = slice((blk - 1) * ATT_BLOCK, (blk + 1) * ATT_BLOCK)
                    k2 = kc_ref[res, win, sl]
                    v2 = vc_ref[res, win, sl]
                    mask = band
                s = jnp.where(mask, _dot_nt(q_ref[res, rows, sl], k2), NEG_BIG)
                m = jnp.max(s, axis=-1, keepdims=True)
                p = jnp.exp2(s - m).astype(BF16)
                o_den = _dot(p, jnp.concatenate([v2, ones], axis=1))
                den = o_den[:, ATT_HEAD_DIM:]
                o_ref[res, rows, sl] = o_den[:, :ATT_HEAD_DIM].astype(BF16)
                stats = jnp.where(lane == hd, m, jnp.where(lane == ATT_HEADS + hd, den, stats))
            st_ref[res, rows, :] = stats


def _attn_group(q, k, v):
    batch, dil, cls, gw = q.shape
    assert cls % ATT_BLOCK == 0
    nblk = min(ATT_CHAINS, cls // ATT_BLOCK)
    nres = max(1, min(dil, ATT_CHAINS // nblk))
    qb = nblk * ATT_BLOCK
    cur = lambda b, r, i: (b, r, i, 0)
    prev = lambda b, r, i: (b, r, jnp.maximum(i * nblk - 1, 0), 0)
    cur_spec = pl.BlockSpec((None, nres, qb, gw), cur)
    prev_spec = pl.BlockSpec((None, nres, ATT_BLOCK, gw), prev)
    whole_class = cls == qb
    body = _attn_whole_class_kernel if whole_class else _attn_kernel
    in_specs = ([cur_spec] * 3 if whole_class
                else [cur_spec, prev_spec, cur_spec, prev_spec, cur_spec])
    args = (q, k, v) if whole_class else (q, k, k, v, v)
    return pl.pallas_call(
        functools.partial(body, nres=nres, nblk=nblk),
        out_shape=(jax.ShapeDtypeStruct((batch, dil, cls, gw), BF16),
                   jax.ShapeDtypeStruct((batch, dil, cls, LANES), F32)),
        grid=(batch, dil // nres, cls // qb),
        in_specs=in_specs,
        out_specs=(pl.BlockSpec((None, nres, qb, gw), cur),
                   pl.BlockSpec((None, nres, qb, LANES), cur)),
        compiler_params=_params("parallel", "parallel", "arbitrary"),
        name=f"attn_dil{dil}",
    )(*args)


def _hgrn_decay(lf_all, b_ref, reach_ref, c):
    r_i = lax.broadcasted_iota(jnp.int32, (c, c), 0)
    c_i = lax.broadcasted_iota(jnp.int32, (c, c), 1)
    tri = jnp.where(c_i <= r_i, 1.0, 0.0).astype(BF16)
    for k in range(lf_all.shape[0] // c):
        lf = lf_all[k * c:(k + 1) * c, :]
        lf_hi = lf.astype(BF16)
        lf_lo = (lf - lf_hi.astype(F32)).astype(BF16)
        b = _dot(tri, lf_hi) + _dot(tri, lf_lo)
        b_ref[k * c:(k + 1) * c, :] = b
        b_mid = b[c // 2 - 1:c // 2, :]
        reach_ref[k] = jnp.max(jnp.maximum(b[0:1, :] - b_mid, b_mid - b[c - 1:c, :]))


def _hgrn_scores_midref(q, kk, b, heads):
    c = b.shape[0]
    b_mid = b[c // 2 - 1:c // 2, :]
    b_last = b[c - 1:c, :]
    q_s = q * jnp.exp2(b - b_mid).astype(BF16)
    k_s = (kk * jnp.exp2(b_mid - b)).astype(BF16)
    q_in = q_s * jnp.exp2(b_mid).astype(BF16)
    k_out = k_s * jnp.exp2(b_last - b_mid).astype(BF16)
    r_i = lax.broadcasted_iota(jnp.int32, (c, c), 0)
    c_i = lax.broadcasted_iota(jnp.int32, (c, c), 1)
    causal = c_i <= r_i
    scores = []
    for hd in range(heads):
        sl = slice(hd * HGRN_HEAD_DIM, (hd + 1) * HGRN_HEAD_DIM)
        scores.append(jnp.where(causal, _dot_nt(q_s[:, sl], k_s[:, sl]), 0.0).astype(BF16))
    return scores, q_in, k_out


def _hgrn_scores_bounded(q, kk, b, heads):
    c, w = b.shape
    q = q.astype(F32)
    q_in = (q * jnp.exp2(b)).astype(BF16)
    k_out = (kk * jnp.exp2(b[c - 1:c, :] - b)).astype(BF16)
    r_i = lax.broadcasted_iota(jnp.int32, (c, c), 0)
    c_i = lax.broadcasted_iota(jnp.int32, (c, c), 1)
    rows = lax.broadcasted_iota(jnp.int32, (c, w), 0)

    levels = []
    half = SUBLANES
    while half < c:
        span = 2 * half
        pieces = [jnp.broadcast_to(b[s + half - 1:s + half, :], (span, w))
                  for s in range(0, c, span)]
        bref = pieces[0] if len(pieces) == 1 else jnp.concatenate(pieces, axis=0)
        upper = (rows & (span - 1)) >= half
        e = jnp.exp2(jnp.where(upper, b - bref, bref - b))
        q_up = jnp.where(upper, q * e, 0.0).astype(BF16)
        k_lo = jnp.where(upper, 0.0, kk * e).astype(BF16)
        same = (r_i & ~(span - 1)) == (c_i & ~(span - 1))
        levels.append((q_up, k_lo, same))
        half = span

    nb = c // SUBLANES
    sub = lax.broadcasted_iota(jnp.int32, (nb, SUBLANES, w), 1)
    q3 = q.reshape(nb, SUBLANES, w)
    k3 = kk.reshape(nb, SUBLANES, w)
    b3 = b.reshape(nb, SUBLANES, w)
    near = []
    for dist in range(SUBLANES):
        if dist == 0:
            prod = q3 * k3
        else:
            k_r = pltpu.roll(k3, dist, 1)
            b_r = pltpu.roll(b3, dist, 1)
            prod = jnp.where(sub >= dist, q3 * k_r * jnp.exp2(b3 - b_r), 0.0)
        near.append(prod.reshape(c, w))

    scores = []
    for hd in range(heads):
        sl = slice(hd * HGRN_HEAD_DIM, (hd + 1) * HGRN_HEAD_DIM)
        a = jnp.zeros((c, c), F32)
        for q_up, k_lo, same in levels:
            a = a + jnp.where(same, _dot_nt(q_up[:, sl], k_lo[:, sl]), 0.0)
        for dist in range(SUBLANES):
            col_val = jnp.sum(near[dist][:, sl], axis=-1, keepdims=True)
            a = a + jnp.where(c_i == r_i - dist, col_val, 0.0)
        scores.append(a.astype(BF16))
    return scores, q_in, k_out


def _hgrn_chunk(q_ref, i_ref, og_ref, hn_ref, o_ref, st_ref, b, kk, rows, score_fn, heads):
    c = b.shape[0]
    v_bf = i_ref[rows, :]
    scores, q_in, k_out = score_fn(q_ref[rows, :], kk, b, heads)
    st_decay = jnp.exp2(b[c - 1:c, :])
    gain = hn_ref[...]
    for hd in range(heads):
        sl = slice(hd * HGRN_HEAD_DIM, (hd + 1) * HGRN_HEAD_DIM)
        v_h = v_bf[:, sl]
        st_h = st_ref[hd]
        o_h = _dot(scores[hd], v_h) + _dot_nt(q_in[:, sl], st_h.astype(BF16))
        st_ref[hd] = st_h * st_decay[:, sl] + _dot_tn(v_h, k_out[:, sl])
        o_n = _rms(o_h, gain[:, sl])
        o_ref[rows, sl] = (o_n * og_ref[rows, sl].astype(F32)).astype(BF16)


def _load_by_token(src_ref, scr_ref):
    dil, rows, width = src_ref.shape
    if dil == 1:
        return src_ref[0].astype(F32)
    parts = []
    for c in range(width // LANES):
        cols = slice(c * LANES, (c + 1) * LANES)
        for r in range(dil):
            scr_ref[c, pl.ds(r, rows, stride=dil), :] = src_ref[r, :, cols].astype(F32)
        parts.append(scr_ref[c])
    return parts[0] if len(parts) == 1 else jnp.concatenate(parts, axis=-1)


def _mix_kernel(x_ref, g_ref, o0_ref, o1_ref, o2_ref, s0_ref, s1_ref, s2_ref, ob_ref,
                wg_ref, wa_ref, wb_ref, wo_ref, out_ref, *scr):
    x = x_ref[...]
    d = x.shape[1]
    h = _rms(x, g_ref[...]).astype(BF16)
    gate_a = jax.nn.sigmoid(_dot(h, wg_ref[:, :d]))
    gate_b = jax.nn.sigmoid(_dot(h, wg_ref[:, d:]))

    scr = list(scr)
    take = lambda ref: _load_by_token(ref, scr.pop(0) if ref.shape[0] > 1 else None)
    o0, o1, o2 = take(o0_ref), take(o1_ref), take(o2_ref)
    s0, s1, s2 = take(s0_ref), take(s1_ref), take(s2_ref)
    head_lane = lax.broadcasted_iota(jnp.int32, s0.shape, 1) < ATT_HEADS
    lse = [s + jnp.log2(jnp.where(head_lane, pltpu.roll(s, LANES - ATT_HEADS, 1), 1.0))
           for s in (s0, s1, s2)]
    mx = jnp.maximum(jnp.maximum(lse[0], lse[1]), lse[2])
    inv = 1.0 / (jnp.exp2(lse[0] - mx) + jnp.exp2(lse[1] - mx) + jnp.exp2(lse[2] - mx))
    w0, w1, w2 = (jnp.exp2(s - mx) * inv for s in (s0, s1, s2))
    parts = []
    for hd in range(ATT_HEADS):
        sl = slice(hd * ATT_HEAD_DIM, (hd + 1) * ATT_HEAD_DIM)
        parts.append(o0[:, sl] * w0[:, hd:hd + 1] + o1[:, sl] * w1[:, hd:hd + 1]
                     + o2[:, sl] * w2[:, hd:hd + 1])
    att = jnp.concatenate(parts, axis=-1).astype(BF16)

    y_a = _dot(att, wa_ref[...])
    y_b = _dot(ob_ref[...], wb_ref[...])
    merged = (gate_a * y_a + gate_b * y_b).astype(BF16)
    out_ref[...] = x + _dot(merged, wo_ref[...])


def _mix(x, norm, o_groups, s_groups, ob, w_in, gate_off, wa, wb, wo, layer, casts=()):
    n, d = x.shape
    batch = o_groups[0].shape[0]
    seq = n // batch
    tm = min(TOKEN_TILE, seq)
    per_b = seq // tm
    row = lambda b, i: (b * per_b + i, 0)
    rows = lambda a: pl.BlockSpec((tm, a.shape[1]), row)
    by_class = lambda a: pl.BlockSpec((None, a.shape[1], tm // a.shape[1], a.shape[3]),
                                      lambda b, i: (b, 0, i, 0))
    grouped = [*o_groups, *s_groups]
    args = [x, norm, *grouped, ob, w_in, wa, wb, wo]
    in_specs = ([rows(x), _layer_block(norm, layer)] + [by_class(a) for a in grouped]
                + [rows(ob), _col_block(w_in, gate_off, 2 * d),
                   _resident(wa), _resident(wb), _resident(wo)])
    scratch = [pltpu.VMEM((a.shape[3] // LANES, tm, LANES), F32)
               for a in grouped if a.shape[1] > 1]
    return _pallas(
        _mix_kernel,
        name="mix_out",
        grid=(batch, per_b), sem=("parallel", "parallel"),
        in_specs=in_specs, args=args,
        out_shape=jax.ShapeDtypeStruct((n, d), F32), out_specs=pl.BlockSpec((tm, d), row),
        scratch=scratch,
        casts=casts, step_of=lambda b, i: b * per_b + i,
    )


def _xattn_kernel(x_ref, g_ref, m_ref, mg_ref, wk_ref, wv_ref, wq_ref, wo_ref, out_ref,
                  k_ref, v_ref):
    @pl.when(pl.program_id(1) == 0)
    def _():
        hm = _rms(m_ref[...], mg_ref[...]).astype(BF16)
        k_ref[...] = _dot(hm, wk_ref[...]).astype(BF16)
        v_ref[...] = _dot(hm, wv_ref[...]).astype(BF16)

    x = x_ref[...]
    h = _rms(x, g_ref[...]).astype(BF16)
    dh = wq_ref.shape[1] // X_HEADS
    q = (_dot(h, wq_ref[...]) * (dh ** -0.5 * math.log2(math.e))).astype(BF16)
    outs = []
    for hd in range(X_HEADS):
        sl = slice(hd * dh, (hd + 1) * dh)
        s = _dot_nt(q[:, sl], k_ref[:, sl])
        m = jnp.max(s, axis=-1, keepdims=True)
        p = jnp.exp2(s - m)
        den = jnp.sum(p, axis=-1, keepdims=True)
        outs.append(_dot(p.astype(BF16), v_ref[:, sl]) / den)
    o = jnp.concatenate(outs, axis=-1).astype(BF16)
    out_ref[...] = x + _dot(o, wo_ref[...])


def _xattn(x, norm, mem, mem_norm, wkv, wq, wo, layer, batch, casts=()):
    n, d = x.shape
    seq = n // batch
    tm = min(TOKEN_TILE, seq)
    n_mem = mem.shape[0] // batch
    width = wkv.shape[1] // 2
    per_b = seq // tm
    row = lambda b, i: (b * per_b + i, 0)
    kv_scratch = pltpu.VMEM((n_mem, width), BF16)
    return _pallas(
        _xattn_kernel,
        name="xattn",
        grid=(batch, per_b), sem=("parallel", "arbitrary"),
        in_specs=[pl.BlockSpec((tm, d), row), _layer_block(norm, layer),
                  pl.BlockSpec((n_mem, d), lambda b, i: (b, 0)), _layer_block(mem_norm, layer),
                  _col_block(wkv, 0, width), _col_block(wkv, width, width),
                  _resident(wq), _resident(wo)],
        args=[x, norm, mem, mem_norm, wkv, wkv, wq, wo],
        out_shape=jax.ShapeDtypeStruct((n, d), F32), out_specs=pl.BlockSpec((tm, d), row),
        scratch=[kv_scratch, kv_scratch],
        casts=casts, step_of=lambda b, i: b * per_b + i,
    )


def kernel(x, mem, positions, ffn1_norm, ffn1_w_gu, ffn1_w_down, mix_norm, w_in,
           hgrn_lower_bounds, hgrn_head_norm, w_att_branch, w_hgrn_branch, w_mix_out,
           xattn_norm, mem_norm, xattn_wq, xattn_wkv, xattn_wo,
           ffn2_norm, ffn2_w_gu, ffn2_w_down, final_norm):
    batch, seq, d = x.shape
    depth = w_in.shape[0]
    n = batch * seq
    att_w = len(ATT_GROUPS) * ATT_GROUP_WIDTH
    hg_w = hgrn_head_norm.shape[1]
    hgrn_off = 3 * att_w
    gate_off = hgrn_off + 4 * hg_w

    xf = x.reshape(n, d)
    memf = mem.reshape(batch * mem.shape[1], d)
    (cos2, sin2), ffn1_w = _rope_tables(positions.reshape(n, 1).astype(F32),
                                        casts=[(ffn1_w_gu, 0), (ffn1_w_down, 0)])

    gain = lambda a: a.reshape(a.shape[0], 1, a.shape[1])
    ffn1_norm, mix_norm, hgrn_head_norm, xattn_norm, mem_norm, ffn2_norm = map(
        gain, (ffn1_norm, mix_norm, hgrn_head_norm, xattn_norm, mem_norm, ffn2_norm))

    for l in range(depth):
        xf, (w_in_l,) = _ffn(xf, ffn1_norm, *ffn1_w, l, casts=[(w_in, l)])

        groups, (w_att_l, w_hgrn_l, w_mix_l) = _proj_attn(
            xf, mix_norm, cos2, sin2, w_in_l, l, batch,
            casts=[(w_att_branch, l), (w_hgrn_branch, l), (w_mix_out, l)])
        guest = next((gi for gi, (kind, *arrs) in enumerate(groups)
                      if kind == "qkv" and _can_host_attention(arrs, n, batch)), None)
        o_b, guest_out, (wkv_l, wq_l, wo_l) = _hgrn_mixer(
            xf, mix_norm, hgrn_lower_bounds, w_in_l, hgrn_off, hgrn_head_norm, l, batch,
            guest_qkv=None if guest is None else groups[guest][1:],
            casts=[(xattn_wkv, l), (xattn_wq, l), (xattn_wo, l)])
        o_groups, s_groups = [], []
        for gi, (kind, *arrs) in enumerate(groups):
            if kind == "attn":
                o_g, s_g = arrs
            elif gi == guest:
                o_g, s_g = guest_out
            else:
                o_g, s_g = _attn_group(*arrs)
            o_groups.append(o_g)
            s_groups.append(s_g)
        xf, _ = _mix(xf, mix_norm, o_groups, s_groups, o_b, w_in_l, gate_off,
                     w_att_l, w_hgrn_l, w_mix_l, l)

        xf, (ffn2_gu_l, ffn2_down_l) = _xattn(xf, xattn_norm, memf, mem_norm, wkv_l, wq_l, wo_l,
                                              l, batch,
                                              casts=[(ffn2_w_gu, l), (ffn2_w_down, l)])

        last = l == depth - 1
        xf, ffn1_w = _ffn(xf, ffn2_norm, ffn2_gu_l, ffn2_down_l, l,
                          final_g=final_norm.reshape(1, -1) if last else None,
                          casts=[] if last else [(ffn1_w_gu, l + 1), (ffn1_w_down, l + 1)])

    return xf.reshape(batch, seq, d)
```

```python
import functools
import math

import jax
import jax.numpy as jnp
from jax import lax
from jax.experimental import pallas as pl
from jax.experimental.pallas import tpu as pltpu

F32 = jnp.float32
BF16 = jnp.bfloat16

ATT_GROUPS = ((128, 1), (512, 4), (2048, 16))
ATT_HEADS = 4
ATT_HEAD_DIM = 128
ATT_GROUP_WIDTH = ATT_HEADS * ATT_HEAD_DIM
HGRN_HEAD_DIM = 128
X_HEADS = 4
ROPE_THETA = 10000.0
EPS = 1e-6

LANES = 128
SUBLANES = 8
MXU_TILE = 256
VMEM_LIMIT_BYTES = 60 * 1024 * 1024

TOKEN_TILE = 1024
FFN_CHUNK_TILES = 3
MXU_PERMUTE_MIN_STRIDE = 8
ATT_BLOCK = 128
ATT_CHAINS = 8
HGRN_CHUNK = 128
HGRN_SAFE_EXPONENT = 115.0
NEG_BIG = -1e30


def _params(*sem):
    return pltpu.CompilerParams(dimension_semantics=sem,
                                vmem_limit_bytes=VMEM_LIMIT_BYTES)


def _rms(x, g):
    ms = jnp.mean(x * x, axis=-1, keepdims=True)
    return x * lax.rsqrt(ms + EPS) * g


def _dot(a, b):
    return jnp.dot(a, b, preferred_element_type=F32)


def _dot_nt(a, b):
    return lax.dot_general(a, b, (((1,), (1,)), ((), ())), preferred_element_type=F32)


def _dot_tn(a, b):
    return lax.dot_general(a, b, (((0,), (0,)), ((), ())), preferred_element_type=F32)


def _resident(a):
    return pl.BlockSpec(a.shape, lambda *_: (0,) * a.ndim, pipeline_mode=pl.Buffered(1))


def _layer_block(a, layer):
    _, rows, width = a.shape
    return pl.BlockSpec((None, rows, width), lambda *_: (layer, 0, 0),
                        pipeline_mode=pl.Buffered(1))


def _col_block(a, off, size):
    return pl.BlockSpec((pl.Element(a.shape[0]), pl.Element(size)), lambda *_: (0, off),
                        pipeline_mode=pl.Buffered(1))


def _pallas(body, *, name, grid, sem, in_specs, args, out_shape, out_specs, scratch=(),
            casts=(), step_of=None):
    single = not isinstance(out_shape, (tuple, list))
    out_shape = [out_shape] if single else list(out_shape)
    out_specs = [out_specs] if single else list(out_specs)
    n_in, n_out, n_cast = len(in_specs), len(out_shape), len(casts)
    steps = math.prod(grid)
    cast_in, cast_out, cast_shape = [], [], []
    for a, layer in casts:
        _, rows, cols = a.shape
        chunk = rows // steps
        assert chunk * steps == rows and chunk % (2 * SUBLANES) == 0
        cast_in.append(pl.BlockSpec((None, chunk, cols),
                                    lambda *g, layer=layer: (layer, step_of(*g), 0)))
        cast_out.append(pl.BlockSpec((chunk, cols), lambda *g: (step_of(*g), 0)))
        cast_shape.append(jax.ShapeDtypeStruct((rows, cols), BF16))

    def hosted(*refs):
        ins = refs[:n_in]
        srcs = refs[n_in:n_in + n_cast]
        outs = refs[n_in + n_cast:n_in + n_cast + n_out]
        dsts = refs[n_in + n_cast + n_out:n_in + 2 * n_cast + n_out]
        for src, dst in zip(srcs, dsts):
            dst[...] = src[...].astype(BF16)
        body(*ins, *outs, *refs[n_in + 2 * n_cast + n_out:])

    res = pl.pallas_call(
        hosted,
        out_shape=tuple(out_shape + cast_shape),
        grid=grid,
        in_specs=list(in_specs) + cast_in,
        out_specs=tuple(out_specs + cast_out),
        scratch_shapes=list(scratch),
        compiler_params=_params(*sem),
        name=name,
    )(*args, *[a for a, _ in casts])
    main = res[:n_out]
    return (main[0] if single else tuple(main)), list(res[n_out:])


def _rope_table_kernel(pos_a_ref, pos_b_ref, cos_ref, sin_ref):
    half = ATT_HEAD_DIM // 2
    rows = pos_a_ref.shape[0]
    lane = lax.broadcasted_iota(jnp.int32, (1, ATT_HEAD_DIM), 1)
    upper = lane >= half
    idx = jnp.where(upper, lane - half, lane).astype(F32)
    inv_freq = jnp.exp(idx * (-2.0 * math.log(ROPE_THETA) / ATT_HEAD_DIM))
    ang = jnp.where(upper, pos_b_ref[...], pos_a_ref[...]) * inv_freq
    sign = jnp.where(upper, 1.0, -1.0)
    for val, out_ref, mult in ((jnp.cos(ang), cos_ref, 1.0), (jnp.sin(ang), sin_ref, sign)):
        swapped = pltpu.roll(val, half, 1)
        out_ref[0:rows, :] = jnp.where(upper, swapped, val) * mult
        out_ref[rows:2 * rows, :] = jnp.where(upper, val, swapped) * mult


def _rope_tables(pos_col, casts=()):
    n = pos_col.shape[0]
    tm = min(n, 2048)
    out = jax.ShapeDtypeStruct((n, ATT_HEAD_DIM), F32)
    return _pallas(
        _rope_table_kernel,
        name="rope_tables",
        grid=(n // tm,), sem=("parallel",),
        in_specs=[pl.BlockSpec((tm // 2, 1), lambda i: (2 * i, 0)),
                  pl.BlockSpec((tm // 2, 1), lambda i: (2 * i + 1, 0))],
        args=[pos_col, pos_col],
        out_shape=(out, out),
        out_specs=(pl.BlockSpec((tm, ATT_HEAD_DIM), lambda i: (i, 0)),) * 2,
        casts=casts, step_of=lambda i: i,
    )


def _ffn_chunks(d_ff):
    assert d_ff % MXU_TILE == 0
    tiles = d_ff // MXU_TILE
    n_chunks = -(-tiles // FFN_CHUNK_TILES)
    bounds = [MXU_TILE * (tiles * k // n_chunks) for k in range(n_chunks + 1)]
    return list(zip(bounds[:-1], bounds[1:]))


def _ffn_kernel(*refs, final):
    if final:
        x_ref, g_ref, wg_ref, wu_ref, wd_ref, fg_ref, o_ref = refs
    else:
        x_ref, g_ref, wg_ref, wu_ref, wd_ref, o_ref = refs
    x = x_ref[...]
    h = _rms(x, g_ref[...]).astype(BF16)
    acc = None
    for lo, hi in _ffn_chunks(wg_ref.shape[1]):
        gate = _dot(h, wg_ref[:, lo:hi])
        up = _dot(h, wu_ref[:, lo:hi])
        act = (gate * jax.nn.sigmoid(gate) * up).astype(BF16)
        part = _dot(act, wd_ref[lo:hi, :])
        acc = part if acc is None else acc + part
    y = x + 0.5 * acc
    if final:
        y = _rms(y, fg_ref[...])
    o_ref[...] = y


def _ffn(x, norm, w_gu, w_down, layer, final_g=None, casts=()):
    n, d = x.shape
    d_ff = w_down.shape[0]
    tm = min(TOKEN_TILE, n)
    final = final_g is not None
    row = pl.BlockSpec((tm, d), lambda i: (i, 0))
    in_specs = [row, _layer_block(norm, layer), _col_block(w_gu, 0, d_ff),
                _col_block(w_gu, d_ff, d_ff), _resident(w_down)]
    args = [x, norm, w_gu, w_gu, w_down]
    if final:
        in_specs.append(_resident(final_g))
        args.append(final_g)
    return _pallas(
        functools.partial(_ffn_kernel, final=final),
        name="ffn_final" if final else "ffn",
        grid=(n // tm,), sem=("parallel",),
        in_specs=in_specs, args=args,
        out_shape=jax.ShapeDtypeStruct((n, d), F32), out_specs=row,
        casts=casts, step_of=lambda i: i,
    )


def _class_permutation(dil):
    per = MXU_TILE // dil
    i = lax.broadcasted_iota(jnp.int32, (MXU_TILE, MXU_TILE), 0)
    j = lax.broadcasted_iota(jnp.int32, (MXU_TILE, MXU_TILE), 1)
    return jnp.where(j == (i % per) * dil + i // per, 1.0, 0.0).astype(BF16)


def _store_by_class(out_ref, scr_ref, val, dil):
    if dil == 1:
        out_ref[0] = val.astype(out_ref.dtype)
        return
    if dil < MXU_PERMUTE_MIN_STRIDE:
        rows = val.shape[0] // dil
        for c in range(val.shape[1] // LANES):
            cols = slice(c * LANES, (c + 1) * LANES)
            scr_ref[c] = val[:, cols]
            for r in range(dil):
                out_ref[r, :, cols] = (
                    scr_ref[c, pl.ds(r, rows, stride=dil), :].astype(out_ref.dtype))
        return
    val = val.astype(out_ref.dtype)
    perm = _class_permutation(dil)
    per = MXU_TILE // dil
    for s in range(val.shape[0] // MXU_TILE):
        y = _dot(perm, val[s * MXU_TILE:(s + 1) * MXU_TILE, :]).astype(out_ref.dtype)
        for r in range(dil):
            out_ref[r, s * per:(s + 1) * per, :] = y[r * per:(r + 1) * per, :]


def _proj_attn_kernel(x_ref, g_ref, cos_ref, sin_ref, w_ref, *rest, dils, fused, strided):
    n_g = len(dils)
    gw = ATT_GROUP_WIDTH
    n_out = sum(2 if f else 3 for f in fused)
    outs, scr = list(rest[:n_out]), list(rest[n_out:])
    stage = [scr.pop(0) for _ in range(3)] if strided else [None] * 3
    first = pl.program_id(1) == 0
    h = _rms(x_ref[...], g_ref[...]).astype(BF16)
    cos = cos_ref[...]
    sin = sin_ref[...]
    scale = ATT_HEAD_DIM ** -0.5 * math.log2(math.e)
    for gi, dil in enumerate(dils):
        part = lambda p: w_ref[:, (p * n_g + gi) * gw:(p * n_g + gi + 1) * gw]
        zq = _dot(h, part(0))
        zk = _dot(h, part(1))
        rq, rk = [], []
        for hd in range(ATT_HEADS):
            sl = slice(hd * ATT_HEAD_DIM, (hd + 1) * ATT_HEAD_DIM)
            xq = zq[:, sl]
            xk = zk[:, sl]
            rq.append((xq * cos + pltpu.roll(xq, ATT_HEAD_DIM // 2, 1) * sin) * scale)
            rk.append(xk * cos + pltpu.roll(xk, ATT_HEAD_DIM // 2, 1) * sin)
        if fused[gi]:
            o_ref, st_ref = outs.pop(0), outs.pop(0)
            q_ref, k_ref, v_ref, kp_ref, vp_ref = (scr.pop(0) for _ in range(5))
        else:
            q_ref, k_ref, v_ref = outs.pop(0), outs.pop(0), outs.pop(0)
        _store_by_class(q_ref, stage[0], jnp.concatenate(rq, axis=-1), dil)
        _store_by_class(k_ref, stage[1], jnp.concatenate(rk, axis=-1), dil)
        _store_by_class(v_ref, stage[2], _dot(h, part(2)), dil)
        if fused[gi]:
            @pl.when(first)
            def _():
                kp_ref[...] = jnp.zeros_like(kp_ref)
                vp_ref[...] = jnp.zeros_like(vp_ref)

            rows = q_ref.shape[1]
            _attn_window(q_ref, kp_ref, k_ref, vp_ref, v_ref, o_ref, st_ref,
                         first, dil, rows // ATT_BLOCK)
            kp_ref[...] = k_ref[:, rows - ATT_BLOCK:, :]
            vp_ref[...] = v_ref[:, rows - ATT_BLOCK:, :]


def _proj_attn(x, norm, cos2, sin2, w_in, layer, batch, casts=()):
    n, d = x.shape
    seq = n // batch
    tm = min(TOKEN_TILE, seq)
    per_b = seq // tm
    gw = ATT_GROUP_WIDTH
    dils = tuple(dil for _, dil in ATT_GROUPS)
    fused = tuple((tm // dil) % ATT_BLOCK == 0 for dil in dils)
    strided = any(1 < dil < MXU_PERMUTE_MIN_STRIDE for dil in dils)
    row = lambda b, i: (b * per_b + i, 0)
    by_class = lambda dil, width: pl.BlockSpec((None, dil, tm // dil, width),
                                               lambda b, i: (b, 0, i, 0))
    out_shape, out_specs = [], []
    scratch = [pltpu.VMEM((gw // LANES, tm, LANES), F32)] * 3 if strided else []
    for dil, f in zip(dils, fused):
        act = jax.ShapeDtypeStruct((batch, dil, seq // dil, gw), BF16)
        if f:
            out_shape += [act, jax.ShapeDtypeStruct((batch, dil, seq // dil, LANES), F32)]
            out_specs += [by_class(dil, gw), by_class(dil, LANES)]
            scratch += [pltpu.VMEM((dil, tm // dil, gw), BF16)] * 3
            scratch += [pltpu.VMEM((dil, ATT_BLOCK, gw), BF16)] * 2
        else:
            out_shape += [act] * 3
            out_specs += [by_class(dil, gw)] * 3
    outs, cast_res = _pallas(
        functools.partial(_proj_attn_kernel, dils=dils, fused=fused, strided=strided),
        name="proj_attn",
        grid=(batch, per_b), sem=("parallel", "arbitrary"),
        in_specs=[
            pl.BlockSpec((tm, d), row),
            _layer_block(norm, layer),
            pl.BlockSpec((tm, ATT_HEAD_DIM), row),
            pl.BlockSpec((tm, ATT_HEAD_DIM), row),
            _col_block(w_in, 0, 3 * len(dils) * gw),
        ],
        args=[x, norm, cos2, sin2, w_in],
        out_shape=out_shape, out_specs=out_specs, scratch=scratch,
        casts=casts, step_of=lambda b, i: b * per_b + i,
    )
    outs, groups = list(outs), []
    for f in fused:
        groups.append(("attn", outs.pop(0), outs.pop(0)) if f
                      else ("qkv", outs.pop(0), outs.pop(0), outs.pop(0)))
    return groups, cast_res


def _hgrn_mixer_kernel(x_ref, g_ref, lbraw_ref, w_ref, hn_ref, *rest,
                       layer, tiles_per_seq, chunk, heads, guest_attention):
    rest = list(rest)
    if guest_attention:
        aq_ref, ak_ref, av_ref = rest[:3]
        o_ref, ao_ref, ast_ref = rest[3:6]
        rest = rest[6:]

        def guest():
            _attn_window(aq_ref, ak_ref, ak_ref, av_ref, av_ref, ao_ref, ast_ref, True,
                         aq_ref.shape[0], aq_ref.shape[1] // ATT_BLOCK)
    else:
        o_ref = rest.pop(0)
        guest = lambda: None
    q_ref, i_ref, og_ref, kk_ref, b_ref, st_ref, reach_ref = rest
    h = _rms(x_ref[...], g_ref[...]).astype(BF16)
    raw = lbraw_ref[...]
    p = jnp.exp(raw - jnp.max(raw, axis=0, keepdims=True))
    p = p / jnp.sum(p, axis=0, keepdims=True)
    lb = jnp.sum(p[:layer + 1], axis=0, keepdims=True) - p[0:1]

    hw = q_ref.shape[1]
    sec = lambda k: w_ref[:, k * hw:(k + 1) * hw]
    q_ref[...] = (_dot(h, sec(0)) * (HGRN_HEAD_DIM ** -0.5)).astype(BF16)
    f = lb + (1.0 - lb) * jax.nn.sigmoid(_dot(h, sec(1)))
    kk_ref[...] = 1.0 - f
    i_ref[...] = _dot(h, sec(2)).astype(BF16)
    og = _dot(h, sec(3))
    og_ref[...] = (og * jax.nn.sigmoid(og)).astype(BF16)
    _hgrn_decay(jnp.log2(f), b_ref, reach_ref, chunk)

    @pl.when(pl.program_id(0) % tiles_per_seq == 0)
    def _():
        st_ref[...] = jnp.zeros_like(st_ref)

    n_chunks = x_ref.shape[0] // chunk
    reach = reach_ref[0]
    for k in range(1, n_chunks):
        reach = jnp.maximum(reach, reach_ref[k])
    all_midref = reach < HGRN_SAFE_EXPONENT

    @pl.when(all_midref)
    def _():
        guest()
        for k in range(n_chunks):
            rows = slice(k * chunk, (k + 1) * chunk)
            _hgrn_chunk(q_ref, i_ref, og_ref, hn_ref, o_ref, st_ref,
                        b_ref[rows, :], kk_ref[rows, :], rows, _hgrn_scores_midref, heads)

    @pl.when(jnp.logical_not(all_midref))
    def _():
        guest()

        def one_chunk(ci, carry):
            rows = pl.ds(pl.multiple_of(ci * chunk, chunk), chunk)
            midref_ok = reach_ref[ci] < HGRN_SAFE_EXPONENT
            for take, score_fn in ((midref_ok, _hgrn_scores_midref),
                                   (jnp.logical_not(midref_ok), _hgrn_scores_bounded)):
                @pl.when(take)
                def _():
                    _hgrn_chunk(q_ref, i_ref, og_ref, hn_ref, o_ref, st_ref,
                                b_ref[rows, :], kk_ref[rows, :], rows, score_fn, heads)
            return carry

        lax.fori_loop(0, n_chunks, one_chunk, 0)


def _can_host_attention(qkv, n_tokens, batch):
    dil, cls = qkv[0].shape[1:3]
    tiles = (n_tokens // batch) // min(TOKEN_TILE, n_tokens // batch)
    return dil % tiles == 0 and cls % ATT_BLOCK == 0


def _hgrn_mixer(x, norm, lb_raw, w_in, col_off, head_gain, layer, batch, guest_qkv=None,
                casts=()):
    n, d = x.shape
    seq = n // batch
    tm = min(TOKEN_TILE, seq)
    tiles = seq // tm
    width = lb_raw.shape[1]
    chunk = min(HGRN_CHUNK, tm)
    heads = width // HGRN_HEAD_DIM
    row = lambda i: (i, 0)
    tile = lambda dtype: pltpu.VMEM((tm, width), dtype)
    in_specs = [pl.BlockSpec((tm, d), row), _layer_block(norm, layer), _resident(lb_raw),
                _col_block(w_in, col_off, 4 * width), _layer_block(head_gain, layer)]
    args = [x, norm, lb_raw, w_in, head_gain]
    out_shape = [jax.ShapeDtypeStruct((n, width), BF16)]
    out_specs = [pl.BlockSpec((tm, width), row)]
    if guest_qkv is not None:
        _, dil, cls, gw = guest_qkv[0].shape
        share = lambda w: pl.BlockSpec((None, dil // tiles, cls, w),
                                       lambda i: (i // tiles, i % tiles, 0, 0))
        in_specs += [share(gw)] * 3
        args += list(guest_qkv)
        out_shape += [jax.ShapeDtypeStruct((batch, dil, cls, gw), BF16),
                      jax.ShapeDtypeStruct((batch, dil, cls, LANES), F32)]
        out_specs += [share(gw), share(LANES)]
    outs, cast_res = _pallas(
        functools.partial(_hgrn_mixer_kernel, layer=layer, tiles_per_seq=tiles,
                          chunk=chunk, heads=heads, guest_attention=guest_qkv is not None),
        name="hgrn_mixer",
        grid=(n // tm,), sem=("arbitrary",),
        in_specs=in_specs, args=args, out_shape=out_shape, out_specs=out_specs,
        scratch=[tile(BF16), tile(BF16), tile(BF16), tile(F32), tile(F32),
                 pltpu.VMEM((heads, HGRN_HEAD_DIM, HGRN_HEAD_DIM), F32),
                 pltpu.SMEM((tm // chunk,), F32)],
        casts=casts, step_of=lambda i: i,
    )
    return outs[0], list(outs[1:]), cast_res


def _attn_kernel(q_ref, kp_ref, kc_ref, vp_ref, vc_ref, o_ref, st_ref, *, nres, nblk):
    _attn_window(q_ref, kp_ref, kc_ref, vp_ref, vc_ref, o_ref, st_ref,
                 pl.program_id(2) == 0, nres, nblk)


def _attn_whole_class_kernel(q_ref, k_ref, v_ref, o_ref, st_ref, *, nres, nblk):
    _attn_window(q_ref, k_ref, k_ref, v_ref, v_ref, o_ref, st_ref, True, nres, nblk)


def _attn_window(q_ref, kp_ref, kc_ref, vp_ref, vc_ref, o_ref, st_ref, first, nres, nblk):
    row = lax.broadcasted_iota(jnp.int32, (ATT_BLOCK, 2 * ATT_BLOCK), 0)
    col = lax.broadcasted_iota(jnp.int32, (ATT_BLOCK, 2 * ATT_BLOCK), 1)
    band = jnp.logical_and(col >= row, col <= row + ATT_BLOCK)
    low = jnp.where(first, ATT_BLOCK, 0)
    band_first = jnp.logical_and(col >= jnp.maximum(row, low), col <= row + ATT_BLOCK)
    ones = jnp.ones((2 * ATT_BLOCK, ATT_HEAD_DIM), BF16)
    lane = lax.broadcasted_iota(jnp.int32, (ATT_BLOCK, LANES), 1)
    for res in range(nres):
        for blk in range(nblk):
            rows = slice(blk * ATT_BLOCK, (blk + 1) * ATT_BLOCK)
            stats = jnp.ones((ATT_BLOCK, LANES), F32)
            for hd in range(ATT_HEADS):
                sl = slice(hd * ATT_HEAD_DIM, (hd + 1) * ATT_HEAD_DIM)
                if blk == 0:
                    head = slice(0, ATT_BLOCK)
                    k2 = jnp.concatenate([kp_ref[res, head, sl], kc_ref[res, head, sl]], axis=0)
                    v2 = jnp.concatenate([vp_ref[res, head, sl], vc_ref[res, head, sl]], axis=0)
                    mask = band_first
                else:
                    win = slice((blk - 1) * ATT_BLOCK, (blk + 1) * ATT_BLOCK)
                    k2 = kc_ref[res, win, sl]
                    v2 = vc_ref[res, win, sl]
                    mask = band
                s = jnp.where(mask, _dot_nt(q_ref[res, rows, sl], k2), NEG_BIG)
                m = jnp.max(s, axis=-1, keepdims=True)
                p = jnp.exp2(s - m).astype(BF16)
                o_den = _dot(p, jnp.concatenate([v2, ones], axis=1))
                den = o_den[:, ATT_HEAD_DIM:]
                o_ref[res, rows, sl] = o_den[:, :ATT_HEAD_DIM].astype(BF16)
                stats = jnp.where(lane == hd, m, jnp.where(lane == ATT_HEADS + hd, den, stats))
            st_ref[res, rows, :] = stats


def _attn_group(q, k, v):
    batch, dil, cls, gw = q.shape
    assert cls % ATT_BLOCK == 0
    nblk = min(ATT_CHAINS, cls // ATT_BLOCK)
    nres = max(1, min(dil, ATT_CHAINS // nblk))
    qb = nblk * ATT_BLOCK
    cur = lambda b, r, i: (b, r, i, 0)
    prev = lambda b, r, i: (b, r, jnp.maximum(i * nblk - 1, 0), 0)
    cur_spec = pl.BlockSpec((None, nres, qb, gw), cur)
    prev_spec = pl.BlockSpec((None, nres, ATT_BLOCK, gw), prev)
    whole_class = cls == qb
    body = _attn_whole_class_kernel if whole_class else _attn_kernel
    in_specs = ([cur_spec] * 3 if whole_class
                else [cur_spec, prev_spec, cur_spec, prev_spec, cur_spec])
    args = (q, k, v) if whole_class else (q, k, k, v, v)
    return pl.pallas_call(
        functools.partial(body, nres=nres, nblk=nblk),
        out_shape=(jax.ShapeDtypeStruct((batch, dil, cls, gw), BF16),
                   jax.ShapeDtypeStruct((batch, dil, cls, LANES), F32)),
        grid=(batch, dil // nres, cls // qb),
        in_specs=in_specs,
        out_specs=(pl.BlockSpec((None, nres, qb, gw), cur),
                   pl.BlockSpec((None, nres, qb, LANES), cur)),
        compiler_params=_params("parallel", "parallel", "arbitrary"),
        name=f"attn_dil{dil}",
    )(*args)


def _hgrn_decay(lf_all, b_ref, reach_ref, c):
    r_i = lax.broadcasted_iota(jnp.int32, (c, c), 0)
    c_i = lax.broadcasted_iota(jnp.int32, (c, c), 1)
    tri = jnp.where(c_i <= r_i, 1.0, 0.0).astype(BF16)
    for k in range(lf_all.shape[0] // c):
        lf = lf_all[k * c:(k + 1) * c, :]
        lf_hi = lf.astype(BF16)
        lf_lo = (lf - lf_hi.astype(F32)).astype(BF16)
        b = _dot(tri, lf_hi) + _dot(tri, lf_lo)
        b_ref[k * c:(k + 1) * c, :] = b
        b_mid = b[c // 2 - 1:c // 2, :]
        reach_ref[k] = jnp.max(jnp.maximum(b[0:1, :] - b_mid, b_mid - b[c - 1:c, :]))


def _hgrn_scores_midref(q, kk, b, heads):
    c = b.shape[0]
    b_mid = b[c // 2 - 1:c // 2, :]
    b_last = b[c - 1:c, :]
    q_s = q * jnp.exp2(b - b_mid).astype(BF16)
    k_s = (kk * jnp.exp2(b_mid - b)).astype(BF16)
    q_in = q_s * jnp.exp2(b_mid).astype(BF16)
    k_out = k_s * jnp.exp2(b_last - b_mid).astype(BF16)
    r_i = lax.broadcasted_iota(jnp.int32, (c, c), 0)
    c_i = lax.broadcasted_iota(jnp.int32, (c, c), 1)
    causal = c_i <= r_i
    scores = []
    for hd in range(heads):
        sl = slice(hd * HGRN_HEAD_DIM, (hd + 1) * HGRN_HEAD_DIM)
        scores.append(jnp.where(causal, _dot_nt(q_s[:, sl], k_s[:, sl]), 0.0).astype(BF16))
    return scores, q_in, k_out


def _hgrn_scores_bounded(q, kk, b, heads):
    c, w = b.shape
    q = q.astype(F32)
    q_in = (q * jnp.exp2(b)).astype(BF16)
    k_out = (kk * jnp.exp2(b[c - 1:c, :] - b)).astype(BF16)
    r_i = lax.broadcasted_iota(jnp.int32, (c, c), 0)
    c_i = lax.broadcasted_iota(jnp.int32, (c, c), 1)
    rows = lax.broadcasted_iota(jnp.int32, (c, w), 0)

    levels = []
    half = SUBLANES
    while half < c:
        span = 2 * half
        pieces = [jnp.broadcast_to(b[s + half - 1:s + half, :], (span, w))
                  for s in range(0, c, span)]
        bref = pieces[0] if len(pieces) == 1 else jnp.concatenate(pieces, axis=0)
        upper = (rows & (span - 1)) >= half
        e = jnp.exp2(jnp.where(upper, b - bref, bref - b))
        q_up = jnp.where(upper, q * e, 0.0).astype(BF16)
        k_lo = jnp.where(upper, 0.0, kk * e).astype(BF16)
        same = (r_i & ~(span - 1)) == (c_i & ~(span - 1))
        levels.append((q_up, k_lo, same))
        half = span

    nb = c // SUBLANES
    sub = lax.broadcasted_iota(jnp.int32, (nb, SUBLANES, w), 1)
    q3 = q.reshape(nb, SUBLANES, w)
    k3 = kk.reshape(nb, SUBLANES, w)
    b3 = b.reshape(nb, SUBLANES, w)
    near = []
    for dist in range(SUBLANES):
        if dist == 0:
            prod = q3 * k3
        else:
            k_r = pltpu.roll(k3, dist, 1)
            b_r = pltpu.roll(b3, dist, 1)
            prod = jnp.where(sub >= dist, q3 * k_r * jnp.exp2(b3 - b_r), 0.0)
        near.append(prod.reshape(c, w))

    scores = []
    for hd in range(heads):
        sl = slice(hd * HGRN_HEAD_DIM, (hd + 1) * HGRN_HEAD_DIM)
        a = jnp.zeros((c, c), F32)
        for q_up, k_lo, same in levels:
            a = a + jnp.where(same, _dot_nt(q_up[:, sl], k_lo[:, sl]), 0.0)
        for dist in range(SUBLANES):
            col_val = jnp.sum(near[dist][:, sl], axis=-1, keepdims=True)
            a = a + jnp.where(c_i == r_i - dist, col_val, 0.0)
        scores.append(a.astype(BF16))
    return scores, q_in, k_out


def _hgrn_chunk(q_ref, i_ref, og_ref, hn_ref, o_ref, st_ref, b, kk, rows, score_fn, heads):
    c = b.shape[0]
    v_bf = i_ref[rows, :]
    scores, q_in, k_out = score_fn(q_ref[rows, :], kk, b, heads)
    st_decay = jnp.exp2(b[c - 1:c, :])
    gain = hn_ref[...]
    for hd in range(heads):
        sl = slice(hd * HGRN_HEAD_DIM, (hd + 1) * HGRN_HEAD_DIM)
        v_h = v_bf[:, sl]
        st_h = st_ref[hd]
        o_h = _dot(scores[hd], v_h) + _dot_nt(q_in[:, sl], st_h.astype(BF16))
        st_ref[hd] = st_h * st_decay[:, sl] + _dot_tn(v_h, k_out[:, sl])
        o_n = _rms(o_h, gain[:, sl])
        o_ref[rows, sl] = (o_n * og_ref[rows, sl].astype(F32)).astype(BF16)


def _load_by_token(src_ref, scr_ref):
    dil, rows, width = src_ref.shape
    if dil == 1:
        return src_ref[0].astype(F32)
    parts = []
    for c in range(width // LANES):
        cols = slice(c * LANES, (c + 1) * LANES)
        for r in range(dil):
            scr_ref[c, pl.ds(r, rows, stride=dil), :] = src_ref[r, :, cols].astype(F32)
        parts.append(scr_ref[c])
    return parts[0] if len(parts) == 1 else jnp.concatenate(parts, axis=-1)


def _mix_kernel(x_ref, g_ref, o0_ref, o1_ref, o2_ref, s0_ref, s1_ref, s2_ref, ob_ref,
                wg_ref, wa_ref, wb_ref, wo_ref, out_ref, *scr):
    x = x_ref[...]
    d = x.shape[1]
    h = _rms(x, g_ref[...]).astype(BF16)
    gate_a = jax.nn.sigmoid(_dot(h, wg_ref[:, :d]))
    gate_b = jax.nn.sigmoid(_dot(h, wg_ref[:, d:]))

    scr = list(scr)
    take = lambda ref: _load_by_token(ref, scr.pop(0) if ref.shape[0] > 1 else None)
    o0, o1, o2 = take(o0_ref), take(o1_ref), take(o2_ref)
    s0, s1, s2 = take(s0_ref), take(s1_ref), take(s2_ref)
    head_lane = lax.broadcasted_iota(jnp.int32, s0.shape, 1) < ATT_HEADS
    lse = [s + jnp.log2(jnp.where(head_lane, pltpu.roll(s, LANES - ATT_HEADS, 1), 1.0))
           for s in (s0, s1, s2)]
    mx = jnp.maximum(jnp.maximum(lse[0], lse[1]), lse[2])
    inv = 1.0 / (jnp.exp2(lse[0] - mx) + jnp.exp2(lse[1] - mx) + jnp.exp2(lse[2] - mx))
    w0, w1, w2 = (jnp.exp2(s - mx) * inv for s in (s0, s1, s2))
    parts = []
    for hd in range(ATT_HEADS):
        sl = slice(hd * ATT_HEAD_DIM, (hd + 1) * ATT_HEAD_DIM)
        parts.append(o0[:, sl] * w0[:, hd:hd + 1] + o1[:, sl] * w1[:, hd:hd + 1]
                     + o2[:, sl] * w2[:, hd:hd + 1])
    att = jnp.concatenate(parts, axis=-1).astype(BF16)

    y_a = _dot(att, wa_ref[...])
    y_b = _dot(ob_ref[...], wb_ref[...])
    merged = (gate_a * y_a + gate_b * y_b).astype(BF16)
    out_ref[...] = x + _dot(merged, wo_ref[...])


def _mix(x, norm, o_groups, s_groups, ob, w_in, gate_off, wa, wb, wo, layer, casts=()):
    n, d = x.shape
    batch = o_groups[0].shape[0]
    seq = n // batch
    tm = min(TOKEN_TILE, seq)
    per_b = seq // tm
    row = lambda b, i: (b * per_b + i, 0)
    rows = lambda a: pl.BlockSpec((tm, a.shape[1]), row)
    by_class = lambda a: pl.BlockSpec((None, a.shape[1], tm // a.shape[1], a.shape[3]),
                                      lambda b, i: (b, 0, i, 0))
    grouped = [*o_groups, *s_groups]
    args = [x, norm, *grouped, ob, w_in, wa, wb, wo]
    in_specs = ([rows(x), _layer_block(norm, layer)] + [by_class(a) for a in grouped]
                + [rows(ob), _col_block(w_in, gate_off, 2 * d),
                   _resident(wa), _resident(wb), _resident(wo)])
    scratch = [pltpu.VMEM((a.shape[3] // LANES, tm, LANES), F32)
               for a in grouped if a.shape[1] > 1]
    return _pallas(
        _mix_kernel,
        name="mix_out",
        grid=(batch, per_b), sem=("parallel", "parallel"),
        in_specs=in_specs, args=args,
        out_shape=jax.ShapeDtypeStruct((n, d), F32), out_specs=pl.BlockSpec((tm, d), row),
        scratch=scratch,
        casts=casts, step_of=lambda b, i: b * per_b + i,
    )


def _xattn_kernel(x_ref, g_ref, m_ref, mg_ref, wk_ref, wv_ref, wq_ref, wo_ref, out_ref,
                  k_ref, v_ref):
    @pl.when(pl.program_id(1) == 0)
    def _():
        hm = _rms(m_ref[...], mg_ref[...]).astype(BF16)
        k_ref[...] = _dot(hm, wk_ref[...]).astype(BF16)
        v_ref[...] = _dot(hm, wv_ref[...]).astype(BF16)

    x = x_ref[...]
    h = _rms(x, g_ref[...]).astype(BF16)
    dh = wq_ref.shape[1] // X_HEADS
    q = (_dot(h, wq_ref[...]) * (dh ** -0.5 * math.log2(math.e))).astype(BF16)
    outs = []
    for hd in range(X_HEADS):
        sl = slice(hd * dh, (hd + 1) * dh)
        s = _dot_nt(q[:, sl], k_ref[:, sl])
        m = jnp.max(s, axis=-1, keepdims=True)
        p = jnp.exp2(s - m)
        den = jnp.sum(p, axis=-1, keepdims=True)
        outs.append(_dot(p.astype(BF16), v_ref[:, sl]) / den)
    o = jnp.concatenate(outs, axis=-1).astype(BF16)
    out_ref[...] = x + _dot(o, wo_ref[...])


def _xattn(x, norm, mem, mem_norm, wkv, wq, wo, layer, batch, casts=()):
    n, d = x.shape
    seq = n // batch
    tm = min(TOKEN_TILE, seq)
    n_mem = mem.shape[0] // batch
    width = wkv.shape[1] // 2
    per_b = seq // tm
    row = lambda b, i: (b * per_b + i, 0)
    kv_scratch = pltpu.VMEM((n_mem, width), BF16)
    return _pallas(
        _xattn_kernel,
        name="xattn",
        grid=(batch, per_b), sem=("parallel", "arbitrary"),
        in_specs=[pl.BlockSpec((tm, d), row), _layer_block(norm, layer),
                  pl.BlockSpec((n_mem, d), lambda b, i: (b, 0)), _layer_block(mem_norm, layer),
                  _col_block(wkv, 0, width), _col_block(wkv, width, width),
                  _resident(wq), _resident(wo)],
        args=[x, norm, mem, mem_norm, wkv, wkv, wq, wo],
        out_shape=jax.ShapeDtypeStruct((n, d), F32), out_specs=pl.BlockSpec((tm, d), row),
        scratch=[kv_scratch, kv_scratch],
        casts=casts, step_of=lambda b, i: b * per_b + i,
    )


def kernel(x, mem, positions, ffn1_norm, ffn1_w_gu, ffn1_w_down, mix_norm, w_in,
           hgrn_lower_bounds, hgrn_head_norm, w_att_branch, w_hgrn_branch, w_mix_out,
           xattn_norm, mem_norm, xattn_wq, xattn_wkv, xattn_wo,
           ffn2_norm, ffn2_w_gu, ffn2_w_down, final_norm):
    batch, seq, d = x.shape
    depth = w_in.shape[0]
    n = batch * seq
    att_w = len(ATT_GROUPS) * ATT_GROUP_WIDTH
    hg_w = hgrn_head_norm.shape[1]
    hgrn_off = 3 * att_w
    gate_off = hgrn_off + 4 * hg_w

    xf = x.reshape(n, d)
    memf = mem.reshape(batch * mem.shape[1], d)
    (cos2, sin2), ffn1_w = _rope_tables(positions.reshape(n, 1).astype(F32),
                                        casts=[(ffn1_w_gu, 0), (ffn1_w_down, 0)])

    gain = lambda a: a.reshape(a.shape[0], 1, a.shape[1])
    ffn1_norm, mix_norm, hgrn_head_norm, xattn_norm, mem_norm, ffn2_norm = map(
        gain, (ffn1_norm, mix_norm, hgrn_head_norm, xattn_norm, mem_norm, ffn2_norm))

    for l in range(depth):
        xf, (w_in_l,) = _ffn(xf, ffn1_norm, *ffn1_w, l, casts=[(w_in, l)])

        groups, (w_att_l, w_hgrn_l, w_mix_l) = _proj_attn(
            xf, mix_norm, cos2, sin2, w_in_l, l, batch,
            casts=[(w_att_branch, l), (w_hgrn_branch, l), (w_mix_out, l)])
        guest = next((gi for gi, (kind, *arrs) in enumerate(groups)
                      if kind == "qkv" and _can_host_attention(arrs, n, batch)), None)
        o_b, guest_out, (wkv_l, wq_l, wo_l) = _hgrn_mixer(
            xf, mix_norm, hgrn_lower_bounds, w_in_l, hgrn_off, hgrn_head_norm, l, batch,
            guest_qkv=None if guest is None else groups[guest][1:],
            casts=[(xattn_wkv, l), (xattn_wq, l), (xattn_wo, l)])
        o_groups, s_groups = [], []
        for gi, (kind, *arrs) in enumerate(groups):
            if kind == "attn":
                o_g, s_g = arrs
            elif gi == guest:
                o_g, s_g = guest_out
            else:
                o_g, s_g = _attn_group(*arrs)
            o_groups.append(o_g)
            s_groups.append(s_g)
        xf, _ = _mix(xf, mix_norm, o_groups, s_groups, o_b, w_in_l, gate_off,
                     w_att_l, w_hgrn_l, w_mix_l, l)

        xf, (ffn2_gu_l, ffn2_down_l) = _xattn(xf, xattn_norm, memf, mem_norm, wkv_l, wq_l, wo_l,
                                              l, batch,
                                              casts=[(ffn2_w_gu, l), (ffn2_w_down, l)])

        last = l == depth - 1
        xf, ffn1_w = _ffn(xf, ffn2_norm, ffn2_gu_l, ffn2_down_l, l,
                          final_g=final_norm.reshape(1, -1) if last else None,
                          casts=[] if last else [(ffn1_w_gu, l + 1), (ffn1_w_down, l + 1)])

    return xf.reshape(batch, seq, d)
```
